```python
import jax
import jax.numpy as jnp
from jax import lax
import numpy as np

D_MODEL = 1024
BATCH = 8
SEQ = 2048
DEPTH = 2
DEC_BATCH = 128
DEC_SEQ = 1
PAST_LEN = 16384
PAGE_SIZE = 128

N_BRANCH = 4
BR_W = D_MODEL // 2
LRU_HEADS = 8
LRU_HD = BR_W // LRU_HEADS
LRU_CONV = 4
LRU_C = 8.0
RWKV_HD = 64
RWKV_HEADS = BR_W // RWKV_HD
RWKV_RANK_W = D_MODEL // 16
RWKV_RANK_A = D_MODEL // 16
RWKV_SHIFT_W = 3 * BR_W + RWKV_RANK_W + RWKV_RANK_A
RWKV_DECAY_SCALE = 0.606531
RWKV_LNX_EPS = 64e-5
CHUNK = 128
GMLP_GROUPS = 8
CONF_K = 31
LN_EPS = 1e-5

OFF_LRU_X = 0
OFF_LRU_G = OFF_LRU_X + BR_W
OFF_RWKV = OFF_LRU_G + BR_W
OFF_RWKV_G = OFF_RWKV + RWKV_SHIFT_W
OFF_GMLP_U = OFF_RWKV_G + BR_W
OFF_GMLP_V = OFF_GMLP_U + BR_W
OFF_GMLP_G = OFF_GMLP_V + BR_W
OFF_CONF_A = OFF_GMLP_G + BR_W
OFF_CONF_B = OFF_CONF_A + BR_W
OFF_CONF_G = OFF_CONF_B + BR_W
OFF_MERGE = OFF_CONF_G + BR_W
N_IN = OFF_MERGE + N_BRANCH * D_MODEL

kernel_name = 'hybrid_rglru_rwkv7_gmlp_conformer_step'


def _layernorm(x, g, b, eps=LN_EPS):
    xf = x.astype(jnp.float32)
    mu = jnp.mean(xf, axis=-1, keepdims=True)
    var = jnp.mean(jnp.square(xf - mu), axis=-1, keepdims=True)
    return ((xf - mu) * lax.rsqrt(var + eps) * g + b).astype(x.dtype)


def _causal_dwconv(x, buf, w, b):
    k = w.shape[0]
    xp = jnp.concatenate([buf.astype(x.dtype), x], axis=1)
    y = lax.conv_general_dilated(xp, w[:, None, :].astype(x.dtype), window_strides=(1,),
                                 padding='VALID', dimension_numbers=('NWC', 'WIO', 'NWC'),
                                 feature_group_count=x.shape[-1])
    return y + b, xp[:, xp.shape[1] - (k - 1):]


def _rglru_branch(xb, conv_buf, h0, conv_w, conv_b, wr, br, wi, bi, lam):
    B, T, _ = xb.shape
    xc, new_buf = _causal_dwconv(xb, conv_buf, conv_w, conv_b)
    xh = xc.reshape(B, T, LRU_HEADS, LRU_HD)
    r = jax.nn.sigmoid(jnp.einsum('bthi,hij->bthj', xh, wr).reshape(B, T, BR_W) + br)
    i = jax.nn.sigmoid(jnp.einsum('bthi,hij->bthj', xh, wi).reshape(B, T, BR_W) + bi)
    log_a = -LRU_C * r.astype(jnp.float32) * jax.nn.softplus(-lam.astype(jnp.float32))
    a = jnp.exp(log_a)
    u = jnp.sqrt(-jnp.expm1(2.0 * log_a)) * (i * xc).astype(jnp.float32)
    u = u.at[:, 0].add(a[:, 0] * h0.astype(jnp.float32))

    def comb(left, right):
        return left[0] * right[0], right[0] * left[1] + right[1]

    _, h = lax.associative_scan(comb, (a, u), axis=1)
    return h.astype(xb.dtype), new_buf, h[:, -1]


def _rwkv7_branch(p, shift0, S0, mu, w0, ww, a0, wa, k_k, k_a, r_k, lnx_g, lnx_b):
    B, T, _ = p.shape
    f32 = jnp.float32
    prev = jnp.concatenate([shift0[:, None].astype(p.dtype), p[:, :-1]], axis=1)
    xs = p + (prev - p) * mu
    r, k, v, dw, da = jnp.split(xs, [BR_W, 2 * BR_W, 3 * BR_W, 3 * BR_W + RWKV_RANK_W], axis=-1)
    log_w = -RWKV_DECAY_SCALE * jax.nn.sigmoid((w0 + jnp.tanh(dw) @ ww).astype(f32))
    a = jax.nn.sigmoid((a0 + da @ wa).astype(f32))

    def hs(t):
        return t.astype(f32).reshape(B, T, RWKV_HEADS, RWKV_HD)

    def ph(t):
        return t.astype(f32).reshape(RWKV_HEADS, RWKV_HD)

    r, k, v, a, w = hs(r), hs(k), hs(v), hs(a), hs(jnp.exp(log_w))
    kk = k * ph(k_k)
    kk = kk / jnp.maximum(jnp.sqrt(jnp.sum(kk * kk, axis=-1, keepdims=True)), 1e-12)
    k = k * (1.0 + (a - 1.0) * ph(k_a))

    def step(S, inp):
        r_t, w_t, k_t, v_t, kk_t, a_t = inp
        sa = jnp.einsum('bhvk,bhk->bhv', S, -kk_t)
        S = (S * w_t[:, :, None, :] + sa[..., None] * (kk_t * a_t)[:, :, None, :]
             + v_t[..., None] * k_t[:, :, None, :])
        return S, jnp.einsum('bhvk,bhk->bhv', S, r_t)

    def tm(t):
        return jnp.moveaxis(t, 1, 0)

    S, o = lax.scan(step, S0.astype(f32), (tm(r), tm(w), tm(k), tm(v), tm(kk), tm(a)))
    o = jnp.moveaxis(o, 0, 1)
    mo = jnp.mean(o, axis=-1, keepdims=True)
    vo = jnp.mean(jnp.square(o - mo), axis=-1, keepdims=True)
    o = (o - mo) * lax.rsqrt(vo + RWKV_LNX_EPS) * ph(lnx_g) + ph(lnx_b)
    o = o + jnp.sum(r * k * r_k.astype(f32), axis=-1, keepdims=True) * v
    return o.reshape(B, T, BR_W).astype(p.dtype), p[:, -1], S


def _chunk_spatial(v, ws, bs):
    B, T, W = v.shape
    Tp = -(-T // CHUNK) * CHUNK
    vp = jnp.pad(v, ((0, 0), (0, Tp - T), (0, 0))).reshape(B, Tp // CHUNK, CHUNK, GMLP_GROUPS, W // GMLP_GROUPS)
    mask = jnp.tril(jnp.ones((CHUNK, CHUNK), dtype=bool))
    wm = jnp.where(mask[None], ws, jnp.zeros_like(ws))
    z = jnp.einsum('gts,bnsgc->bntgc', wm, vp) + bs.T[None, None, :, :, None]
    return z.reshape(B, Tp, W)[:, :T]


def _gmlp_branch(u, v, ln_g, ln_b, ws, bs):
    vn = _layernorm(v, ln_g, ln_b)
    return u * _chunk_spatial(vn, ws, bs), vn


def _conformer_branch(ga, gb, buf, dw_w, dw_b, ln_g, ln_b):
    glu = ga * jax.nn.sigmoid(gb)
    y, new_buf = _causal_dwconv(glu, buf, dw_w, dw_b)
    return jax.nn.silu(_layernorm(y, ln_g, ln_b)), new_buf


def _layer(x, c, lru_buf, lru_h, rw_shift, rw_S, cf_buf, lp, alpha):
    B, T, _ = x.shape
    mod = jax.nn.silu(c) @ lp['w_cond'] + lp['b_cond']
    shift, scale, gate = jnp.split(mod, 3, axis=-1)
    h = x * (1.0 + scale[:, None]) + shift[:, None]
    proj = h @ lp['w_in']

    def col(off, n):
        return proj[..., off:off + n]

    o_a, lru_buf, lru_h = _rglru_branch(col(OFF_LRU_X, BR_W), lru_buf, lru_h, lp['lru_conv_w'], lp['lru_conv_b'],
                                        lp['lru_wr'], lp['lru_br'], lp['lru_wi'], lp['lru_bi'], lp['lru_lambda'])
    o_b, rw_shift, rw_S = _rwkv7_branch(col(OFF_RWKV, RWKV_SHIFT_W), rw_shift, rw_S, lp['rwkv_mu'], lp['rwkv_w0'],
                                        lp['rwkv_ww'], lp['rwkv_a0'], lp['rwkv_wa'], lp['rwkv_kk'], lp['rwkv_ka'],
                                        lp['rwkv_rk'], lp['rwkv_lnx_g'], lp['rwkv_lnx_b'])
    o_c, v_rows = _gmlp_branch(col(OFF_GMLP_U, BR_W), col(OFF_GMLP_V, BR_W), lp['gmlp_ln_g'], lp['gmlp_ln_b'],
                               lp['gmlp_ws'], lp['gmlp_bs'])
    o_d, cf_buf = _conformer_branch(col(OFF_CONF_A, BR_W), col(OFF_CONF_B, BR_W), cf_buf, lp['conf_dw_w'],
                                    lp['conf_dw_b'], lp['conf_ln_g'], lp['conf_ln_b'])
    silu_gates = jax.nn.silu(jnp.stack([col(OFF_LRU_G, BR_W), col(OFF_RWKV_G, BR_W),
                                        col(OFF_GMLP_G, BR_W), col(OFF_CONF_G, BR_W)], axis=2))
    o = jnp.stack([o_a, o_b, o_c, o_d], axis=2) * silu_gates
    g = jax.nn.sigmoid(col(OFF_MERGE, N_BRANCH * D_MODEL).reshape(B, T, N_BRANCH, D_MODEL))
    m = jnp.sum(g * jnp.einsum('btnw,nwd->btnd', o, lp['w_branch']), axis=2)
    y = m @ lp['w_out'] + lp['b_out']
    x = _layernorm(alpha * x + gate[:, None] * y, lp['ln_g'], lp['ln_b'])
    return x, (lru_buf, lru_h, rw_shift, rw_S, cf_buf, v_rows)


def setup_inputs(seed: int = 0) -> dict:
    key = jax.random.key(seed)
    ks = iter(jax.random.split(key, 64))
    L = DEPTH
    beta = (8.0 * DEPTH) ** -0.25

    def nrm(shape, s=1.0):
        return s * jax.random.normal(next(ks), shape, jnp.float32)

    def near_one(shape, s=0.1):
        return 1.0 + nrm(shape, s)

    a_c = jax.random.uniform(next(ks), (L, BR_W), jnp.float32, 0.9, 0.999)
    s_l = a_c ** (1.0 / LRU_C)
    lru_lambda = jnp.log(s_l) - jnp.log1p(-s_l)
    rwkv_mu = jax.random.uniform(next(ks), (L, RWKV_SHIFT_W), jnp.float32)
    return {
        'x_prompt': nrm((BATCH, SEQ, D_MODEL)),
        'x_sample': nrm((DEC_BATCH, DEC_SEQ, D_MODEL)),
        'state_lru_conv': nrm((L, DEC_BATCH, LRU_CONV - 1, BR_W)),
        'state_lru_h': nrm((L, DEC_BATCH, BR_W), 0.5),
        'state_rwkv_shift': nrm((L, DEC_BATCH, RWKV_SHIFT_W)),
        'state_rwkv_S': nrm((L, DEC_BATCH, RWKV_HEADS, RWKV_HD, RWKV_HD)),
        'state_conf_conv': nrm((L, DEC_BATCH, CONF_K - 1, BR_W), 0.5),
        'c_prompt': nrm((BATCH, D_MODEL)),
        'c_sample': nrm((DEC_BATCH, D_MODEL)),
        'w_cond': nrm((L, D_MODEL, 3 * D_MODEL), 0.5 * D_MODEL ** -0.5),
        'b_cond': nrm((L, 3 * D_MODEL), 0.01),
        'w_in': nrm((L, D_MODEL, N_IN), D_MODEL ** -0.5),
        'lru_conv_w': nrm((L, LRU_CONV, BR_W), LRU_CONV ** -0.5),
        'lru_conv_b': nrm((L, BR_W), 0.01),
        'lru_wr': nrm((L, LRU_HEADS, LRU_HD, LRU_HD), LRU_HD ** -0.5),
        'lru_br': nrm((L, BR_W), 0.01),
        'lru_wi': nrm((L, LRU_HEADS, LRU_HD, LRU_HD), LRU_HD ** -0.5),
        'lru_bi': nrm((L, BR_W), 0.01),
        'lru_lambda': lru_lambda,
        'rwkv_mu': rwkv_mu,
        'rwkv_w0': nrm((L, BR_W)),
        'rwkv_ww': nrm((L, RWKV_RANK_W, BR_W), 0.5 * RWKV_RANK_W ** -0.5),
        'rwkv_a0': nrm((L, BR_W), 0.5),
        'rwkv_wa': nrm((L, RWKV_RANK_A, BR_W), 0.5 * RWKV_RANK_A ** -0.5),
        'rwkv_kk': 0.85 + nrm((L, BR_W), 0.05),
        'rwkv_ka': near_one((L, BR_W), 0.05),
        'rwkv_rk': nrm((L, RWKV_HEADS, RWKV_HD), 0.1),
        'rwkv_lnx_g': near_one((L, BR_W)),
        'rwkv_lnx_b': nrm((L, BR_W), 0.01),
        'gmlp_ln_g': near_one((L, BR_W)),
        'gmlp_ln_b': nrm((L, BR_W), 0.01),
        'gmlp_ws': nrm((L, GMLP_GROUPS, CHUNK, CHUNK), CHUNK ** -0.5),
        'gmlp_bs': near_one((L, GMLP_GROUPS, CHUNK)),
        'conf_dw_w': nrm((L, CONF_K, BR_W), CONF_K ** -0.5),
        'conf_dw_b': nrm((L, BR_W), 0.01),
        'conf_ln_g': near_one((L, BR_W)),
        'conf_ln_b': nrm((L, BR_W), 0.01),
        'w_branch': nrm((L, N_BRANCH, BR_W, D_MODEL), beta * BR_W ** -0.5),
        'w_out': nrm((L, D_MODEL, D_MODEL), beta * D_MODEL ** -0.5),
        'b_out': nrm((L, D_MODEL), 0.01),
        'ln_g': near_one((L, D_MODEL)),
        'ln_b': nrm((L, D_MODEL), 0.01),
    }


def reference(x_prompt, x_sample, state_lru_conv, state_lru_h, state_rwkv_shift, state_rwkv_S, state_conf_conv,
              c_prompt, c_sample, w_cond, b_cond, w_in, lru_conv_w, lru_conv_b, lru_wr, lru_br, lru_wi, lru_bi,
              lru_lambda, rwkv_mu, rwkv_w0, rwkv_ww, rwkv_a0, rwkv_wa, rwkv_kk, rwkv_ka, rwkv_rk, rwkv_lnx_g,
              rwkv_lnx_b, gmlp_ln_g, gmlp_ln_b, gmlp_ws, gmlp_bs, conf_dw_w, conf_dw_b, conf_ln_g, conf_ln_b,
              w_branch, w_out, b_out, ln_g, ln_b):
    alpha = (2.0 * DEPTH) ** 0.25
    dt = x_prompt.dtype
    sdt = state_rwkv_S.dtype
    nb = x_prompt.shape[0]
    xp, xs = x_prompt, x_sample
    outs_p, outs_s = [], []
    for l in range(DEPTH):
        lp = dict(w_cond=w_cond[l], b_cond=b_cond[l], w_in=w_in[l], lru_conv_w=lru_conv_w[l],
                  lru_conv_b=lru_conv_b[l], lru_wr=lru_wr[l], lru_br=lru_br[l], lru_wi=lru_wi[l],
                  lru_bi=lru_bi[l], lru_lambda=lru_lambda[l], rwkv_mu=rwkv_mu[l], rwkv_w0=rwkv_w0[l],
                  rwkv_ww=rwkv_ww[l], rwkv_a0=rwkv_a0[l], rwkv_wa=rwkv_wa[l], rwkv_kk=rwkv_kk[l],
                  rwkv_ka=rwkv_ka[l], rwkv_rk=rwkv_rk[l], rwkv_lnx_g=rwkv_lnx_g[l], rwkv_lnx_b=rwkv_lnx_b[l],
                  gmlp_ln_g=gmlp_ln_g[l], gmlp_ln_b=gmlp_ln_b[l], gmlp_ws=gmlp_ws[l], gmlp_bs=gmlp_bs[l],
                  conf_dw_w=conf_dw_w[l], conf_dw_b=conf_dw_b[l], conf_ln_g=conf_ln_g[l],
                  conf_ln_b=conf_ln_b[l], w_branch=w_branch[l], w_out=w_out[l], b_out=b_out[l],
                  ln_g=ln_g[l], ln_b=ln_b[l])
        xp, sp = _layer(xp, c_prompt,
                        jnp.zeros((nb, LRU_CONV - 1, BR_W), dt), jnp.zeros((nb, BR_W), dt),
                        jnp.zeros((nb, RWKV_SHIFT_W), dt), jnp.zeros((nb, RWKV_HEADS, RWKV_HD, RWKV_HD), dt),
                        jnp.zeros((nb, CONF_K - 1, BR_W), dt), lp, alpha)
        xs, ss = _layer(xs, c_sample, state_lru_conv[l], state_lru_h[l], state_rwkv_shift[l], state_rwkv_S[l],
                        state_conf_conv[l], lp, alpha)
        outs_p.append(sp)
        outs_s.append(ss)

    def stk(outs, i, dtype):
        return jnp.stack([o[i] for o in outs]).astype(dtype)

    return (xp, xs,
            stk(outs_p, 0, dt), stk(outs_s, 0, sdt),
            stk(outs_p, 1, dt), stk(outs_s, 1, sdt),
            stk(outs_p, 2, dt), stk(outs_s, 2, sdt),
            stk(outs_p, 3, dt), stk(outs_s, 3, sdt),
            stk(outs_p, 4, dt), stk(outs_s, 4, sdt),
            stk(outs_s, 5, sdt))
```

```python
import functools

import jax
import jax.numpy as jnp
from jax import lax
from jax.experimental import pallas as pl
from jax.experimental.pallas import tpu as pltpu
from jax.scipy.linalg import block_diag

f32 = jnp.float32
bf16 = jnp.bfloat16

D_MODEL = 1024
N_BRANCH = 4
BR_W = D_MODEL // 2
LRU_HEADS = 8
LRU_CONV = 4
LRU_C = 8.0
RWKV_HD = 64
RWKV_HEADS = BR_W // RWKV_HD
RWKV_RANK = D_MODEL // 16
RWKV_SHIFT_W = 3 * BR_W + 2 * RWKV_RANK
RWKV_DECAY_SCALE = 0.606531
RWKV_LNX_EPS = 64e-5
GMLP_CHUNK = 128
GMLP_GROUPS = 8
CONF_K = 31
LN_EPS = 1e-5

OFF_LRU = 0
OFF_RWKV = OFF_LRU + 2 * BR_W
OFF_GMLP = OFF_RWKV + RWKV_SHIFT_W + BR_W
OFF_CONF = OFF_GMLP + 3 * BR_W
OFF_MERGE = OFF_CONF + 3 * BR_W
N_IN = OFF_MERGE + N_BRANCH * D_MODEL

LANES = 128
SUBLANES = 8
VMEM_LIMIT_BYTES = 60 * 1024 * 1024

TIME_TILE = 256
RWKV_CHUNK = 64
CHUNK_SHIFT = RWKV_CHUNK.bit_length() - 1
SAMPLE_BLOCK = 8
CONV_ROWS = 64
LRU_HALO = SUBLANES
CONF_HALO = 32

V_LCW, V_LCB, V_LBR, V_LBI, V_LAM = 0, 4, 5, 6, 7
V_W0, V_A0, V_KK, V_KA, V_RK, V_LNXG, V_LNXB = 8, 9, 10, 11, 12, 13, 14
V_GLG, V_GLB, V_CDB, V_CLG, V_CLB, V_GWS0, V_GBS0 = 15, 16, 17, 18, 19, 20, 21
V_CDW = 24
V_ROWS = 56
M_BOUT, M_LNG, M_LNB = 0, 1, 2


def _sigmoid(x):
    return jax.nn.sigmoid(x)


def _silu(x):
    return x * jax.nn.sigmoid(x)


def _softplus(z):
    return jnp.maximum(z, 0.0) + jnp.log1p(jnp.exp(-jnp.abs(z)))


def _layernorm(x, g, b, eps=LN_EPS):
    mu = jnp.mean(x, axis=-1, keepdims=True)
    xc = x - mu
    var = jnp.mean(xc * xc, axis=-1, keepdims=True)
    return xc * lax.rsqrt(var + eps) * g + b


def _bdot(a, b):
    return jnp.dot(a.astype(bf16), b.astype(bf16), preferred_element_type=f32)


def _bdot_nt(a, b):
    return lax.dot_general(a.astype(bf16), b.astype(bf16), (((1,), (1,)), ((), ())),
                           preferred_element_type=f32)


def _bdot_tn(a, b):
    return lax.dot_general(a.astype(bf16), b.astype(bf16), (((0,), (0,)), ((), ())),
                           preferred_element_type=f32)


def _segsum(x, seg_ref):
    hi = x.astype(bf16)
    lo = (x - hi.astype(f32)).astype(bf16)
    seg = seg_ref[...]
    return (jnp.dot(hi, seg, preferred_element_type=f32)
            + jnp.dot(lo, seg, preferred_element_type=f32))


def _lru_gates(xc, wlru_ref, vec):
    rg = jnp.dot(xc.astype(bf16), wlru_ref[...], preferred_element_type=f32)
    r = _sigmoid(rg[:, :BR_W] + vec(V_LBR))
    ig = _sigmoid(rg[:, BR_W:] + vec(V_LBI))
    log_a = -LRU_C * r * _softplus(-vec(V_LAM))
    a = jnp.exp(log_a)
    mult = jnp.sqrt(-jnp.tanh(log_a) * (a * a + 1.0))
    return a, mult * (ig * xc)


def _rwkv_prep(xs, vec, wlora_ref, seg_ref):
    r = xs[:, 0:BR_W]
    k = xs[:, BR_W:2 * BR_W]
    v = xs[:, 2 * BR_W:3 * BR_W]
    dwa = xs[:, 3 * BR_W:3 * BR_W + 2 * RWKV_RANK]
    lane = lax.broadcasted_iota(jnp.int32, dwa.shape, 1)
    lora_in = jnp.where(lane < RWKV_RANK, jnp.tanh(dwa), dwa)
    wa = jnp.dot(lora_in.astype(bf16), wlora_ref[...], preferred_element_type=f32)
    logw = -RWKV_DECAY_SCALE * _sigmoid(vec(V_W0) + wa[:, :BR_W])
    a = _sigmoid(vec(V_A0) + wa[:, BR_W:])
    kk = k * vec(V_KK)
    kkn = kk / jnp.maximum(jnp.sqrt(_segsum(kk * kk, seg_ref)), 1e-12)
    k2 = k * (1.0 + (a - 1.0) * vec(V_KA))
    bonus = _segsum(r * k2 * vec(V_RK), seg_ref) * v
    return r, k2, v, logw, a, kkn, bonus


def _rwkv_post(o, bonus, vec, seg_ref):
    inv_n = 1.0 / RWKV_HD
    mean = _segsum(o, seg_ref) * inv_n
    oc = o - mean
    var = _segsum(oc * oc, seg_ref) * inv_n
    return oc * lax.rsqrt(var + RWKV_LNX_EPS) * vec(V_LNXG) + vec(V_LNXB) + bonus


def _scan_rows(a, u):
    n = a.shape[0]
    rows = lax.broadcasted_iota(jnp.int32, a.shape, 0)
    d = 1
    while d < n:
        keep = rows >= d
        a_s = jnp.where(keep, pltpu.roll(a, d, axis=0), 1.0)
        u_s = jnp.where(keep, pltpu.roll(u, d, axis=0), 0.0)
        u = u + a * u_s
        a = a * a_s
        d *= 2
    return a, u


def _cond_kernel(c_ref, w_ref, b_ref, o_ref):
    o_ref[0] = _bdot(_silu(c_ref[...]), w_ref[0]) + b_ref[0]


def _cond_call(c_all, w_cond, b_cond):
    depth = w_cond.shape[0]
    n = c_all.shape[0]
    return pl.pallas_call(
        _cond_kernel,
        grid=(depth,),
        in_specs=[pl.BlockSpec((n, D_MODEL), lambda l: (0, 0)),
                  pl.BlockSpec((1, D_MODEL, 3 * D_MODEL), lambda l: (l, 0, 0)),
                  pl.BlockSpec((1, 1, 3 * D_MODEL), lambda l: (l, 0, 0))],
        out_specs=pl.BlockSpec((1, n, 3 * D_MODEL), lambda l: (l, 0, 0)),
        out_shape=jax.ShapeDtypeStruct((depth, n, 3 * D_MODEL), f32),
        compiler_params=pltpu.CompilerParams(dimension_semantics=("arbitrary",),
                                             vmem_limit_bytes=VMEM_LIMIT_BYTES),
        name="cond",
    )(c_all, w_cond, b_cond.reshape(depth, 1, 3 * D_MODEL))


def _prompt_kernel(x_ref, mod_ref, win_ref, wlru_ref, wlora_ref, wg_ref, bsx_ref, seg_ref, ltri_ref,
                   wbr_ref, wout_ref, v512_ref, mu_ref, v1024_ref,
                   y_ref, lconv_ref, lh_ref, rshift_ref, rs_ref, cconv_ref,
                   h_ref, lext_ref, lhc_ref, rprev_ref, s_ref, cext_ref, ycv_ref, ob_ref, rw_ref,
                   ecc_ref, o_ref, *, alpha):
    T = TIME_TILE
    C = RWKV_CHUNK
    i = pl.program_id(1)
    last = pl.num_programs(1) - 1

    def vec(row):
        return v512_ref[row:row + 1, :]

    @pl.when(i == 0)
    def _init():
        lext_ref[0:LRU_HALO, :] = jnp.zeros((LRU_HALO, BR_W), f32)
        lhc_ref[...] = jnp.zeros(lhc_ref.shape, f32)
        rprev_ref[...] = jnp.zeros(rprev_ref.shape, f32)
        s_ref[...] = jnp.zeros(s_ref.shape, f32)
        cext_ref[0:CONF_HALO, :] = jnp.zeros((CONF_HALO, BR_W), f32)

    shift = mod_ref[0, 0:1, :]
    scale = mod_ref[0, 1:2, :]
    gate = mod_ref[0, 2:3, :]
    h_ref[...] = (x_ref[0] * (1.0 + scale) + shift).astype(bf16)

    def proj(off, n):
        return jnp.dot(h_ref[...], win_ref[:, off:off + n], preferred_element_type=f32)

    pa = proj(OFF_LRU, 2 * BR_W)
    xb = pa[:, :BR_W]
    lext_ref[LRU_HALO:LRU_HALO + T, :] = xb
    xc = vec(V_LCB)
    for j in range(LRU_CONV):
        xc = xc + vec(V_LCW + j) * lext_ref[pl.ds(LRU_HALO - (LRU_CONV - 1) + j, T), :]

    @pl.when(i == last)
    def _lru_conv_state():
        lconv_ref[0] = lext_ref[pl.ds(LRU_HALO + T - (LRU_CONV - 1), LRU_CONV - 1), :]

    lext_ref[0:LRU_HALO, :] = lext_ref[T:T + LRU_HALO, :]
    a, u = _lru_gates(xc, wlru_ref, vec)
    a_cum, hs = _scan_rows(a, u)
    hfull = hs + a_cum * lhc_ref[0:1, :]
    lhc_ref[0:1, :] = hfull[T - 1:T, :]

    @pl.when(i == last)
    def _lru_h_state():
        lh_ref[0] = hfull[T - 1:T, :]

    ob_ref[0] = (hfull * _silu(pa[:, BR_W:])).astype(bf16)

    pb = proj(OFF_RWKV, RWKV_SHIFT_W + BR_W)
    p = pb[:, :RWKV_SHIFT_W]
    rows = lax.broadcasted_iota(jnp.int32, p.shape, 0)
    prev = jnp.where(rows == 0, rprev_ref[0:1, :], pltpu.roll(p, 1, axis=0))
    rprev_ref[0:1, :] = p[T - 1:T, :]

    @pl.when(i == last)
    def _rwkv_shift_state():
        rshift_ref[0] = p[T - 1:T, :]

    xs = p + (prev - p) * mu_ref[...]
    r, k2, v, logw, ag, kkn, bonus = _rwkv_prep(xs, vec, wlora_ref, seg_ref)
    lw1 = logw.astype(bf16)
    rem = logw - lw1.astype(f32)
    lw2 = rem.astype(bf16)
    lw3 = (rem - lw2.astype(f32)).astype(bf16)
    ltri = ltri_ref[...]
    c = (jnp.dot(ltri, lw1, preferred_element_type=f32) + jnp.dot(ltri, lw2, preferred_element_type=f32)
         + jnp.dot(ltri, lw3, preferred_element_type=f32))
    cend = jnp.concatenate(
        [jnp.broadcast_to(c[(n + 1) * C - 1:(n + 1) * C, :], (C, BR_W)) for n in range(T // C)], axis=0)
    for n in range(T // C):
        ecc_ref[n * SUBLANES:(n + 1) * SUBLANES, :] = jnp.broadcast_to(
            jnp.exp(c[(n + 1) * C - 1:(n + 1) * C, :]), (SUBLANES, BR_W))
    e_neg = jnp.exp(-c)
    e_end = jnp.exp(cend - c)
    beta = kkn * ag
    rw_ref[0] = (-kkn * jnp.exp(c - logw)).astype(bf16)
    rw_ref[1] = (r * jnp.exp(c)).astype(bf16)
    rw_ref[2] = (beta * e_neg).astype(bf16)
    rw_ref[3] = (k2 * e_neg).astype(bf16)
    rw_ref[4] = v.astype(bf16)
    rw_ref[5] = (beta * e_end).astype(bf16)
    rw_ref[6] = (k2 * e_end).astype(bf16)

    R = lax.broadcasted_iota(jnp.int32, (2 * C, LANES), 0)
    Cc = lax.broadcasted_iota(jnp.int32, (2 * C, LANES), 1)
    t_idx = R & (C - 1)
    s_idx = Cc & (C - 1)
    mask_a = jnp.where(R < C, (s_idx < t_idx).astype(f32), (s_idx <= t_idx).astype(f32))
    block_mask = ((R >> CHUNK_SHIFT) == (Cc >> CHUNK_SHIFT)).astype(f32)
    eye = (R == Cc).astype(f32)
    lvl_masks = [((R >> 1) == (Cc >> 1)).astype(f32)]
    sh = 1
    while (1 << sh) < C:
        lvl_masks.append((((R >> (sh + 1)) == (Cc >> (sh + 1))) & ((R >> sh) != (Cc >> sh))).astype(f32))
        sh += 1
    left64 = lax.broadcasted_iota(jnp.int32, (C, LANES), 1) < C
    left128 = Cc < C

    def sel_l(x, m):
        return jnp.where(m, x, jnp.zeros_like(x))

    def sel_r(x, m):
        return jnp.where(m, jnp.zeros_like(x), x)

    def chunk_body(n, carry):
        rs = pl.ds(pl.multiple_of(n * C, C), C)
        for q in range(RWKV_HEADS // 2):
            ls = slice(q * LANES, (q + 1) * LANES)
            at = rw_ref[0, rs, ls]
            rt = rw_ref[1, rs, ls]
            bt = rw_ref[2, rs, ls]
            kt = rw_ref[3, rs, ls]
            vv = rw_ref[4, rs, ls]
            b_end = rw_ref[5, rs, ls]
            k_end = rw_ref[6, rs, ls]
            sp = s_ref[q]
            ar = jnp.concatenate([at, rt], axis=0)
            out_e = _bdot_nt(sel_l(ar, left128), jnp.concatenate([bt, kt], axis=0)) * mask_a
            out_o = _bdot_nt(sel_r(ar, left128), jnp.concatenate([kt, bt], axis=0)) * mask_a
            s_in = _bdot_nt(ar, sp)
            ae = out_e[0:C]
            ao = out_o[0:C]
            v_l = sel_l(vv, left64)
            v_r = sel_r(vv, left64)
            g = s_in[0:C] + _bdot(jnp.where(left64, ao, ae), jnp.concatenate([v_r, v_l], axis=0))
            a_pair = jnp.concatenate([sel_l(ae, left64), sel_r(ao, left64)], axis=0)
            tm = eye + a_pair * lvl_masks[0]
            for lm in lvl_masks[1:]:
                w = _bdot(a_pair * lm, tm)
                tm = tm + _bdot(tm, w)
            t_row = tm[0:C] + tm[C:2 * C]
            uu = _bdot(t_row, jnp.concatenate([sel_l(g, left64), sel_r(g, left64)], axis=0)).astype(bf16)
            u_l = sel_l(uu, left64)
            u_r = sel_r(uu, left64)
            o_blk = s_in[C:2 * C] + _bdot(jnp.concatenate([out_e[C:2 * C], out_o[C:2 * C]], axis=1),
                                          jnp.concatenate([u_l, v_l, v_r, u_r], axis=0))
            o_ref[rs, ls] = o_blk
            upd = _bdot_tn(jnp.concatenate([uu, vv], axis=0), jnp.concatenate([b_end, k_end], axis=0))
            decay = ecc_ref[pl.ds(pl.multiple_of(n * SUBLANES, SUBLANES), SUBLANES), ls][0:1, :]
            s_ref[q] = sp * decay + upd * block_mask
        return carry

    lax.fori_loop(0, T // C, chunk_body, 0)

    @pl.when(i == last)
    def _rwkv_s_state():
        for q in range(RWKV_HEADS // 2):
            sp = s_ref[q]
            rs_ref[0, 2 * q] = sp[0:C, 0:C]
            rs_ref[0, 2 * q + 1] = sp[C:2 * C, C:2 * C]

    ob_ref[1] = (_rwkv_post(o_ref[...], bonus, vec, seg_ref) * _silu(pb[:, RWKV_SHIFT_W:])).astype(bf16)

    pc = proj(OFF_GMLP, 3 * BR_W)
    vn = _layernorm(pc[:, BR_W:2 * BR_W], vec(V_GLG), vec(V_GLB))
    left_g = lax.broadcasted_iota(jnp.int32, (GMLP_CHUNK, LANES), 1) < (LANES // 2)
    z_rows = []
    for n in range(T // GMLP_CHUNK):
        z_cols = []
        for q in range(GMLP_GROUPS // 2):
            vq = vn[n * GMLP_CHUNK:(n + 1) * GMLP_CHUNK, q * LANES:(q + 1) * LANES].astype(bf16)
            rhs = jnp.concatenate([sel_l(vq, left_g), sel_r(vq, left_g)], axis=0)
            z_cols.append(jnp.dot(wg_ref[q], rhs, preferred_element_type=f32))
        z_rows.append(jnp.concatenate(z_cols, axis=1) + bsx_ref[...])
    z = jnp.concatenate(z_rows, axis=0)
    ob_ref[2] = (pc[:, :BR_W] * z * _silu(pc[:, 2 * BR_W:])).astype(bf16)

    pd = proj(OFF_CONF, 3 * BR_W)
    cext_ref[CONF_HALO:CONF_HALO + T, :] = pd[:, :BR_W] * _sigmoid(pd[:, BR_W:2 * BR_W])
    base = CONF_HALO - (CONF_K - 1)
    for rb in range(T // CONV_ROWS):
        for lb in range(BR_W // LANES):
            ls = slice(lb * LANES, (lb + 1) * LANES)
            acc = jnp.broadcast_to(v512_ref[V_CDB:V_CDB + 1, ls], (CONV_ROWS, LANES))
            for j in range(CONF_K):
                acc = acc + v512_ref[V_CDW + j:V_CDW + j + 1, ls] * cext_ref[pl.ds(rb * CONV_ROWS + base + j, CONV_ROWS), ls]
            ycv_ref[rb * CONV_ROWS:(rb + 1) * CONV_ROWS, ls] = acc

    @pl.when(i == last)
    def _conf_state():
        cconv_ref[0] = cext_ref[pl.ds(T + base, CONF_K - 1), :]

    cext_ref[0:CONF_HALO, :] = cext_ref[T:T + CONF_HALO, :]
    ob_ref[3] = (_silu(_layernorm(ycv_ref[...], vec(V_CLG), vec(V_CLB))) * _silu(pd[:, 2 * BR_W:])).astype(bf16)

    m = None
    for nb in range(N_BRANCH):
        term = _sigmoid(proj(OFF_MERGE + nb * D_MODEL, D_MODEL)) * jnp.dot(
            ob_ref[nb], wbr_ref[nb], preferred_element_type=f32)
        m = term if m is None else m + term
    yv = jnp.dot(m.astype(bf16), wout_ref[...], preferred_element_type=f32) + v1024_ref[M_BOUT:M_BOUT + 1, :]
    xn = alpha * x_ref[0] + gate * yv
    y_ref[0] = _layernorm(xn, v1024_ref[M_LNG:M_LNG + 1, :], v1024_ref[M_LNB:M_LNB + 1, :])


def _resident(shape):
    nd = len(shape)
    return pl.BlockSpec(shape, lambda b, i: (0,) * nd, pipeline_mode=pl.Buffered(1))


def _prompt_call(x, mod3, lw, alpha):
    nb, seq, _ = x.shape
    T = TIME_TILE
    assert seq % T == 0 and T % GMLP_CHUNK == 0 and T % RWKV_CHUNK == 0
    nt = seq // T
    in_specs = [
        pl.BlockSpec((1, T, D_MODEL), lambda b, i: (b, i, 0)),
        pl.BlockSpec((1, 3, D_MODEL), lambda b, i: (b, 0, 0)),
        _resident(lw["win"].shape), _resident(lw["wlru"].shape), _resident(lw["wlora"].shape),
        _resident(lw["wg"].shape), _resident(lw["bsx"].shape), _resident(lw["seg"].shape),
        _resident(lw["ltri"].shape), _resident(lw["wbr"].shape), _resident(lw["wout"].shape),
        _resident(lw["v512"].shape), _resident(lw["mu"].shape), _resident(lw["v1024"].shape),
    ]
    out_shape = (
        jax.ShapeDtypeStruct((nb, seq, D_MODEL), f32),
        jax.ShapeDtypeStruct((nb, LRU_CONV - 1, BR_W), f32),
        jax.ShapeDtypeStruct((nb, 1, BR_W), f32),
        jax.ShapeDtypeStruct((nb, 1, RWKV_SHIFT_W), f32),
        jax.ShapeDtypeStruct((nb, RWKV_HEADS, RWKV_HD, RWKV_HD), f32),
        jax.ShapeDtypeStruct((nb, CONF_K - 1, BR_W), f32),
    )
    out_specs = (
        pl.BlockSpec((1, T, D_MODEL), lambda b, i: (b, i, 0)),
        pl.BlockSpec((1, LRU_CONV - 1, BR_W), lambda b, i: (b, 0, 0)),
        pl.BlockSpec((1, 1, BR_W), lambda b, i: (b, 0, 0)),
        pl.BlockSpec((1, 1, RWKV_SHIFT_W), lambda b, i: (b, 0, 0)),
        pl.BlockSpec((1, RWKV_HEADS, RWKV_HD, RWKV_HD), lambda b, i: (b, 0, 0, 0)),
        pl.BlockSpec((1, CONF_K - 1, BR_W), lambda b, i: (b, 0, 0)),
    )
    scratch = [
        pltpu.VMEM((T, D_MODEL), bf16),
        pltpu.VMEM((T + LRU_HALO, BR_W), f32),
        pltpu.VMEM((SUBLANES, BR_W), f32),
        pltpu.VMEM((SUBLANES, RWKV_SHIFT_W), f32),
        pltpu.VMEM((RWKV_HEADS // 2, 2 * RWKV_HD, 2 * RWKV_HD), f32),
        pltpu.VMEM((T + CONF_HALO, BR_W), f32),
        pltpu.VMEM((T, BR_W), f32),
        pltpu.VMEM((N_BRANCH, T, BR_W), bf16),
        pltpu.VMEM((7, T, BR_W), bf16),
        pltpu.VMEM((T // RWKV_CHUNK * SUBLANES, BR_W), f32),
        pltpu.VMEM((T, BR_W), f32),
    ]
    return pl.pallas_call(
        functools.partial(_prompt_kernel, alpha=alpha),
        grid=(nb, nt),
        in_specs=in_specs,
        out_specs=out_specs,
        out_shape=out_shape,
        scratch_shapes=scratch,
        compiler_params=pltpu.CompilerParams(dimension_semantics=("arbitrary", "arbitrary"),
                                             vmem_limit_bytes=VMEM_LIMIT_BYTES),
        name="prompt_layer",
    )(x, mod3, lw["win"], lw["wlru"], lw["wlora"], lw["wg"], lw["bsx"], lw["seg"], lw["ltri"],
      lw["wbr"], lw["wout"], lw["v512"], lw["mu"], lw["v1024"])


def _sample_proj_kernel(x_ref, mod_ref, w_ref, o_ref):
    shift = mod_ref[:, 0:D_MODEL]
    scale = mod_ref[:, D_MODEL:2 * D_MODEL]
    h = (x_ref[...] * (1.0 + scale) + shift).astype(bf16)
    o_ref[...] = jnp.dot(h, w_ref[...], preferred_element_type=f32)


SAMPLE_PROJ_COLS = 1152


def _sample_proj_call(x, mod, win):
    n = x.shape[0]
    assert N_IN % SAMPLE_PROJ_COLS == 0
    return pl.pallas_call(
        _sample_proj_kernel,
        grid=(N_IN // SAMPLE_PROJ_COLS,),
        in_specs=[pl.BlockSpec((n, D_MODEL), lambda j: (0, 0)),
                  pl.BlockSpec((n, 3 * D_MODEL), lambda j: (0, 0)),
                  pl.BlockSpec((D_MODEL, SAMPLE_PROJ_COLS), lambda j: (0, j))],
        out_specs=pl.BlockSpec((n, SAMPLE_PROJ_COLS), lambda j: (0, j)),
        out_shape=jax.ShapeDtypeStruct((n, N_IN), f32),
        compiler_params=pltpu.CompilerParams(dimension_semantics=("arbitrary",),
                                             vmem_limit_bytes=VMEM_LIMIT_BYTES),
        name="sample_proj",
    )(x, mod, win)


def _sample_state_kernel(p_ref, lconv_ref, lh_ref, rshift_ref, rs_ref, cconv_ref,
                         wlru_ref, wlora_ref, seg_ref, v512_ref, mu_ref,
                         ob_ref, lconv_o, lh_o, rshift_o, rs_o, cconv_o, gv_o):
    nb = SAMPLE_BLOCK

    def vec(row):
        return v512_ref[row:row + 1, :]

    xb = p_ref[:, OFF_LRU:OFF_LRU + BR_W]
    xc = vec(V_LCB) + vec(V_LCW + LRU_CONV - 1) * xb
    for j in range(LRU_CONV - 1):
        xc = xc + vec(V_LCW + j) * lconv_ref[:, j, :]
    lconv_o[:, 0:LRU_CONV - 2, :] = lconv_ref[:, 1:LRU_CONV - 1, :]
    lconv_o[:, LRU_CONV - 2, :] = xb
    a, u = _lru_gates(xc, wlru_ref, vec)
    hn = a * lh_ref[...] + u
    lh_o[...] = hn
    ob_ref[:, 0:BR_W] = hn * _silu(p_ref[:, OFF_LRU + BR_W:OFF_LRU + 2 * BR_W])

    p = p_ref[:, OFF_RWKV:OFF_RWKV + RWKV_SHIFT_W]
    xs = p + (rshift_ref[...] - p) * mu_ref[...]
    rshift_o[...] = p
    r, k2, v, logw, ag, kkn, bonus = _rwkv_prep(xs, vec, wlora_ref, seg_ref)
    w = jnp.exp(logw)
    beta = kkn * ag
    n = RWKV_HD
    diag = (lax.broadcasted_iota(jnp.int32, (n, n), 0) == lax.broadcasted_iota(jnp.int32, (n, n), 1))
    o_rows = []
    for b in range(nb):
        o_heads = []
        for hd in range(RWKV_HEADS):
            ls = slice(hd * n, (hd + 1) * n)
            s0 = rs_ref[b, hd]
            sa = -jnp.sum(s0 * kkn[b:b + 1, ls], axis=-1, keepdims=True)
            v_col = jnp.sum(jnp.where(diag, v[b:b + 1, ls], 0.0), axis=-1, keepdims=True)
            sn = s0 * w[b:b + 1, ls] + sa * beta[b:b + 1, ls] + v_col * k2[b:b + 1, ls]
            rs_o[b, hd] = sn
            o_col = jnp.sum(sn * r[b:b + 1, ls], axis=-1, keepdims=True)
            o_heads.append(jnp.sum(jnp.where(diag, o_col, 0.0), axis=0, keepdims=True))
        o_rows.append(jnp.concatenate(o_heads, axis=1))
    o = jnp.concatenate(o_rows, axis=0)
    gb = p_ref[:, OFF_RWKV + RWKV_SHIFT_W:OFF_RWKV + RWKV_SHIFT_W + BR_W]
    ob_ref[:, BR_W:2 * BR_W] = _rwkv_post(o, bonus, vec, seg_ref) * _silu(gb)

    vn = _layernorm(p_ref[:, OFF_GMLP + BR_W:OFF_GMLP + 2 * BR_W], vec(V_GLG), vec(V_GLB))
    gv_o[...] = vn
    z = vec(V_GWS0) * vn + vec(V_GBS0)
    ob_ref[:, 2 * BR_W:3 * BR_W] = (p_ref[:, OFF_GMLP:OFF_GMLP + BR_W] * z
                                    * _silu(p_ref[:, OFF_GMLP + 2 * BR_W:OFF_GMLP + 3 * BR_W]))

    glu = p_ref[:, OFF_CONF:OFF_CONF + BR_W] * _sigmoid(p_ref[:, OFF_CONF + BR_W:OFF_CONF + 2 * BR_W])
    taps = v512_ref[V_CDW:V_CDW + CONF_K - 1, :]
    y = (vec(V_CDB) + vec(V_CDW + CONF_K - 1) * glu
         + jnp.sum(cconv_ref[...] * taps[None, :, :], axis=1))
    cconv_o[:, 0:CONF_K - 2, :] = cconv_ref[:, 1:CONF_K - 1, :]
    cconv_o[:, CONF_K - 2, :] = glu
    ob_ref[:, 3 * BR_W:4 * BR_W] = (_silu(_layernorm(y, vec(V_CLG), vec(V_CLB)))
                                    * _silu(p_ref[:, OFF_CONF + 2 * BR_W:OFF_CONF + 3 * BR_W]))


def _sample_state_call(proj, lconv, lh, rshift, rs, cconv, lw):
    n = proj.shape[0]
    nb = SAMPLE_BLOCK
    assert n % nb == 0

    def blk(shape):
        nd = len(shape)
        return pl.BlockSpec((nb,) + shape, lambda i: (i,) + (0,) * nd)

    def res(shape):
        nd = len(shape)
        return pl.BlockSpec(shape, lambda i: (0,) * nd, pipeline_mode=pl.Buffered(1))

    state_specs = [blk((LRU_CONV - 1, BR_W)), blk((BR_W,)), blk((RWKV_SHIFT_W,)),
                   blk((RWKV_HEADS, RWKV_HD, RWKV_HD)), blk((CONF_K - 1, BR_W))]
    in_specs = ([pl.BlockSpec((nb, OFF_MERGE), lambda i: (i, 0))] + state_specs
                + [res(lw["wlru"].shape), res(lw["wlora"].shape), res(lw["seg"].shape),
                   res(lw["v512"].shape), res(lw["mu"].shape)])
    out_specs = [blk((N_BRANCH * BR_W,))] + state_specs + [blk((BR_W,))]
    out_shape = [jax.ShapeDtypeStruct((n, N_BRANCH * BR_W), f32),
                 jax.ShapeDtypeStruct(lconv.shape, f32), jax.ShapeDtypeStruct(lh.shape, f32),
                 jax.ShapeDtypeStruct(rshift.shape, f32), jax.ShapeDtypeStruct(rs.shape, f32),
                 jax.ShapeDtypeStruct(cconv.shape, f32), jax.ShapeDtypeStruct((n, BR_W), f32)]
    return pl.pallas_call(
        _sample_state_kernel,
        grid=(n // nb,),
        in_specs=in_specs,
        out_specs=out_specs,
        out_shape=out_shape,
        compiler_params=pltpu.CompilerParams(dimension_semantics=("arbitrary",),
                                             vmem_limit_bytes=VMEM_LIMIT_BYTES),
        name="sample_state",
    )(proj, lconv, lh, rshift, rs, cconv, lw["wlru"], lw["wlora"], lw["seg"], lw["v512"], lw["mu"])


def _sample_merge_kernel(x_ref, mod_ref, p_ref, ob_ref, wbr_ref, wout_ref, v1024_ref, y_ref, *, alpha):
    m = None
    for nb in range(N_BRANCH):
        off = OFF_MERGE + nb * D_MODEL
        term = _sigmoid(p_ref[:, off:off + D_MODEL]) * _bdot(
            ob_ref[:, nb * BR_W:(nb + 1) * BR_W], wbr_ref[nb])
        m = term if m is None else m + term
    yv = _bdot(m, wout_ref[...]) + v1024_ref[M_BOUT:M_BOUT + 1, :]
    xn = alpha * x_ref[...] + mod_ref[:, 2 * D_MODEL:3 * D_MODEL] * yv
    y_ref[...] = _layernorm(xn, v1024_ref[M_LNG:M_LNG + 1, :], v1024_ref[M_LNB:M_LNB + 1, :])


def _sample_merge_call(x, mod, proj, ob, lw, alpha):
    n = x.shape[0]
    return pl.pallas_call(
        functools.partial(_sample_merge_kernel, alpha=alpha),
        grid=(1,),
        in_specs=[pl.BlockSpec((n, D_MODEL), lambda i: (0, 0)),
                  pl.BlockSpec((n, 3 * D_MODEL), lambda i: (0, 0)),
                  pl.BlockSpec((n, N_IN), lambda i: (0, 0)),
                  pl.BlockSpec((n, N_BRANCH * BR_W), lambda i: (0, 0)),
                  pl.BlockSpec(lw["wbr"].shape, lambda i: (0, 0, 0)),
                  pl.BlockSpec(lw["wout"].shape, lambda i: (0, 0)),
                  pl.BlockSpec(lw["v1024"].shape, lambda i: (0, 0))],
        out_specs=pl.BlockSpec((n, D_MODEL), lambda i: (0, 0)),
        out_shape=jax.ShapeDtypeStruct((n, D_MODEL), f32),
        compiler_params=pltpu.CompilerParams(dimension_semantics=("arbitrary",),
                                             vmem_limit_bytes=VMEM_LIMIT_BYTES),
        name="sample_merge",
    )(x, mod, proj, ob, lw["wbr"], lw["wout"], lw["v1024"])


def _pack_layer(l, w_in, lru_conv_w, lru_conv_b, lru_wr, lru_br, lru_wi, lru_bi, lru_lambda, rwkv_mu,
                rwkv_w0, rwkv_ww, rwkv_a0, rwkv_wa, rwkv_kk, rwkv_ka, rwkv_rk, rwkv_lnx_g, rwkv_lnx_b,
                gmlp_ln_g, gmlp_ln_b, gmlp_ws, gmlp_bs, conf_dw_w, conf_dw_b, conf_ln_g, conf_ln_b,
                w_branch, w_out, b_out, ln_g, ln_b):
    rep = BR_W // GMLP_GROUPS
    wr = block_diag(*[lru_wr[l, h] for h in range(LRU_HEADS)])
    wi = block_diag(*[lru_wi[l, h] for h in range(LRU_HEADS)])
    zero_lora = jnp.zeros((RWKV_RANK, BR_W), f32)
    wlora = jnp.concatenate([jnp.concatenate([rwkv_ww[l], zero_lora], axis=1),
                             jnp.concatenate([zero_lora, rwkv_wa[l]], axis=1)], axis=0)
    tril = jnp.tril(jnp.ones((GMLP_CHUNK, GMLP_CHUNK), dtype=bool))
    wm = jnp.where(tril[None], gmlp_ws[l], 0.0)
    wg = jnp.stack([jnp.concatenate([wm[2 * q], wm[2 * q + 1]], axis=1) for q in range(GMLP_GROUPS // 2)])
    rows = [lru_conv_w[l], lru_conv_b[l][None], lru_br[l][None], lru_bi[l][None], lru_lambda[l][None],
            rwkv_w0[l][None], rwkv_a0[l][None], rwkv_kk[l][None], rwkv_ka[l][None],
            rwkv_rk[l].reshape(1, BR_W), rwkv_lnx_g[l][None], rwkv_lnx_b[l][None],
            gmlp_ln_g[l][None], gmlp_ln_b[l][None], conf_dw_b[l][None], conf_ln_g[l][None],
            conf_ln_b[l][None], jnp.repeat(gmlp_ws[l, :, 0, 0], rep)[None],
            jnp.repeat(gmlp_bs[l, :, 0], rep)[None], jnp.zeros((V_CDW - V_GBS0 - 1, BR_W), f32),
            conf_dw_w[l], jnp.zeros((V_ROWS - V_CDW - CONF_K, BR_W), f32)]
    v1024 = jnp.concatenate([b_out[l][None], ln_g[l][None], ln_b[l][None],
                             jnp.zeros((SUBLANES - 3, D_MODEL), f32)], axis=0)
    nchunk = TIME_TILE // RWKV_CHUNK
    return dict(
        win=w_in[l].astype(bf16),
        wlru=jnp.concatenate([wr, wi], axis=1).astype(bf16),
        wlora=wlora.astype(bf16),
        wg=wg.astype(bf16),
        bsx=jnp.repeat(gmlp_bs[l].T, rep, axis=1),
        seg=jnp.kron(jnp.eye(RWKV_HEADS, dtype=f32), jnp.ones((RWKV_HD, RWKV_HD), f32)).astype(bf16),
        ltri=jnp.kron(jnp.eye(nchunk, dtype=f32), jnp.tril(jnp.ones((RWKV_CHUNK, RWKV_CHUNK), f32))).astype(bf16),
        wbr=w_branch[l].astype(bf16),
        wout=w_out[l].astype(bf16),
        v512=jnp.concatenate(rows, axis=0),
        mu=rwkv_mu[l][None],
        v1024=v1024,
    )


def kernel(x_prompt, x_sample, state_lru_conv, state_lru_h, state_rwkv_shift, state_rwkv_S, state_conf_conv, c_prompt, c_sample, w_cond, b_cond, w_in, lru_conv_w, lru_conv_b, lru_wr, lru_br, lru_wi, lru_bi, lru_lambda, rwkv_mu, rwkv_w0, rwkv_ww, rwkv_a0, rwkv_wa, rwkv_kk, rwkv_ka, rwkv_rk, rwkv_lnx_g, rwkv_lnx_b, gmlp_ln_g, gmlp_ln_b, gmlp_ws, gmlp_bs, conf_dw_w, conf_dw_b, conf_ln_g, conf_ln_b, w_branch, w_out, b_out, ln_g, ln_b):
    depth = w_in.shape[0]
    alpha = (2.0 * depth) ** 0.25
    nb = x_prompt.shape[0]
    ns = x_sample.shape[0]

    mod = _cond_call(jnp.concatenate([c_prompt, c_sample], axis=0), w_cond.astype(bf16), b_cond)
    xp = x_prompt
    xs = x_sample.reshape(ns, D_MODEL)
    outs_p, outs_s = [], []
    for l in range(depth):
        lw = _pack_layer(l, w_in, lru_conv_w, lru_conv_b, lru_wr, lru_br, lru_wi, lru_bi, lru_lambda, rwkv_mu,
                         rwkv_w0, rwkv_ww, rwkv_a0, rwkv_wa, rwkv_kk, rwkv_ka, rwkv_rk, rwkv_lnx_g,
                         rwkv_lnx_b, gmlp_ln_g, gmlp_ln_b, gmlp_ws, gmlp_bs, conf_dw_w, conf_dw_b,
                         conf_ln_g, conf_ln_b, w_branch, w_out, b_out, ln_g, ln_b)
        mod_p = mod[l, :nb].reshape(nb, 3, D_MODEL)
        mod_s = mod[l, nb:]
        xp, lconv_p, lh_p, rshift_p, rs_p, cconv_p = _prompt_call(xp, mod_p, lw, alpha)
        outs_p.append((lconv_p, lh_p.reshape(nb, BR_W), rshift_p.reshape(nb, RWKV_SHIFT_W), rs_p, cconv_p))

        proj_s = _sample_proj_call(xs, mod_s, lw["win"])
        ob, lconv_s, lh_s, rshift_s, rs_s, cconv_s, gv_s = _sample_state_call(
            proj_s, state_lru_conv[l], state_lru_h[l], state_rwkv_shift[l], state_rwkv_S[l],
            state_conf_conv[l], lw)
        xs = _sample_merge_call(xs, mod_s, proj_s, ob, lw, alpha)
        outs_s.append((lconv_s, lh_s, rshift_s, rs_s, cconv_s, gv_s.reshape(ns, 1, BR_W)))

    def stk(outs, j):
        return jnp.stack([o[j] for o in outs])

    return (xp, xs.reshape(ns, 1, D_MODEL),
            stk(outs_p, 0), stk(outs_s, 0),
            stk(outs_p, 1), stk(outs_s, 1),
            stk(outs_p, 2), stk(outs_s, 2),
            stk(outs_p, 3), stk(outs_s, 3),
            stk(outs_p, 4), stk(outs_s, 4),
            stk(outs_s, 5))
```

```python
import functools

import jax
import jax.numpy as jnp
from jax import lax
from jax.experimental import pallas as pl
from jax.experimental.pallas import tpu as pltpu
from jax.scipy.linalg import block_diag

f32 = jnp.float32
bf16 = jnp.bfloat16

D_MODEL = 1024
N_BRANCH = 4
BR_W = D_MODEL // 2
LRU_HEADS = 8
LRU_CONV = 4
LRU_C = 8.0
RWKV_HD = 64
RWKV_HEADS = BR_W // RWKV_HD
RWKV_RANK = D_MODEL // 16
RWKV_SHIFT_W = 3 * BR_W + 2 * RWKV_RANK
RWKV_DECAY_SCALE = 0.606531
RWKV_LNX_EPS = 64e-5
GMLP_CHUNK = 128
GMLP_GROUPS = 8
CONF_K = 31
LN_EPS = 1e-5

OFF_LRU = 0
OFF_RWKV = OFF_LRU + 2 * BR_W
OFF_GMLP = OFF_RWKV + RWKV_SHIFT_W + BR_W
OFF_CONF = OFF_GMLP + 3 * BR_W
OFF_MERGE = OFF_CONF + 3 * BR_W
N_IN = OFF_MERGE + N_BRANCH * D_MODEL

LANES = 128
SUBLANES = 8
VMEM_LIMIT_BYTES = 60 * 1024 * 1024

TIME_TILE = 256
RWKV_CHUNK = 64
CHUNK_SHIFT = RWKV_CHUNK.bit_length() - 1
RWKV_GROUP = 8
SAMPLE_BLOCK = 8
CONV_ROWS = 64
LRU_HALO = SUBLANES
CONF_HALO = 32

V_LCW, V_LCB, V_LBR, V_LBI, V_LAM = 0, 4, 5, 6, 7
V_W0, V_A0, V_KK, V_KA, V_RK, V_LNXG, V_LNXB = 8, 9, 10, 11, 12, 13, 14
V_GLG, V_GLB, V_CDB, V_CLG, V_CLB, V_GWS0, V_GBS0 = 15, 16, 17, 18, 19, 20, 21
V_CDW = 24
V_ROWS = 56
M_BOUT, M_LNG, M_LNB = 0, 1, 2


def _sigmoid(x):
    return jax.nn.sigmoid(x)


def _silu(x):
    return x * jax.nn.sigmoid(x)


def _softplus(z):
    return jnp.maximum(z, 0.0) + jnp.log1p(jnp.exp(-jnp.abs(z)))


def _layernorm(x, g, b, eps=LN_EPS):
    mu = jnp.mean(x, axis=-1, keepdims=True)
    xc = x - mu
    var = jnp.mean(xc * xc, axis=-1, keepdims=True)
    return xc * lax.rsqrt(var + eps) * g + b


def _bdot(a, b):
    return jnp.dot(a.astype(bf16), b.astype(bf16), preferred_element_type=f32)


def _bdot_nt(a, b):
    return lax.dot_general(a.astype(bf16), b.astype(bf16), (((1,), (1,)), ((), ())),
                           preferred_element_type=f32)


def _bdot_tn(a, b):
    return lax.dot_general(a.astype(bf16), b.astype(bf16), (((0,), (0,)), ((), ())),
                           preferred_element_type=f32)


def _segsum(x, seg_ref):
    hi = x.astype(bf16)
    lo = (x - hi.astype(f32)).astype(bf16)
    seg = seg_ref[...]
    return (jnp.dot(hi, seg, preferred_element_type=f32)
            + jnp.dot(lo, seg, preferred_element_type=f32))


def _lru_gates(xc, wlru_ref, vec):
    rg = jnp.dot(xc.astype(bf16), wlru_ref[...], preferred_element_type=f32)
    r = _sigmoid(rg[:, :BR_W] + vec(V_LBR))
    ig = _sigmoid(rg[:, BR_W:] + vec(V_LBI))
    log_a = -LRU_C * r * _softplus(-vec(V_LAM))
    a = jnp.exp(log_a)
    mult = jnp.sqrt(-jnp.tanh(log_a) * (a * a + 1.0))
    return a, mult * (ig * xc)


def _rwkv_prep(xs, vec, wlora_ref, seg_ref):
    r = xs[:, 0:BR_W]
    k = xs[:, BR_W:2 * BR_W]
    v = xs[:, 2 * BR_W:3 * BR_W]
    dwa = xs[:, 3 * BR_W:3 * BR_W + 2 * RWKV_RANK]
    lane = lax.broadcasted_iota(jnp.int32, dwa.shape, 1)
    lora_in = jnp.where(lane < RWKV_RANK, jnp.tanh(dwa), dwa)
    wa = jnp.dot(lora_in.astype(bf16), wlora_ref[...], preferred_element_type=f32)
    logw = -RWKV_DECAY_SCALE * _sigmoid(vec(V_W0) + wa[:, :BR_W])
    a = _sigmoid(vec(V_A0) + wa[:, BR_W:])
    kk = k * vec(V_KK)
    kkn = kk / jnp.maximum(jnp.sqrt(_segsum(kk * kk, seg_ref)), 1e-12)
    k2 = k * (1.0 + (a - 1.0) * vec(V_KA))
    bonus = _segsum(r * k2 * vec(V_RK), seg_ref) * v
    return r, k2, v, logw, a, kkn, bonus


def _rwkv_post(o, bonus, vec, seg_ref):
    inv_n = 1.0 / RWKV_HD
    mean = _segsum(o, seg_ref) * inv_n
    oc = o - mean
    var = _segsum(oc * oc, seg_ref) * inv_n
    return oc * lax.rsqrt(var + RWKV_LNX_EPS) * vec(V_LNXG) + vec(V_LNXB) + bonus


def _scan_rows(a, u):
    n = a.shape[0]
    rows = lax.broadcasted_iota(jnp.int32, a.shape, 0)
    d = 1
    while d < n:
        keep = rows >= d
        a_s = jnp.where(keep, pltpu.roll(a, d, axis=0), 1.0)
        u_s = jnp.where(keep, pltpu.roll(u, d, axis=0), 0.0)
        u = u + a * u_s
        a = a * a_s
        d *= 2
    return a, u


def _cond_kernel(c_ref, w_ref, b_ref, o_ref):
    o_ref[0] = _bdot(_silu(c_ref[...]), w_ref[0]) + b_ref[0]


def _cond_call(c_all, w_cond, b_cond):
    depth = w_cond.shape[0]
    n = c_all.shape[0]
    return pl.pallas_call(
        _cond_kernel,
        grid=(depth,),
        in_specs=[pl.BlockSpec((n, D_MODEL), lambda l: (0, 0)),
                  pl.BlockSpec((1, D_MODEL, 3 * D_MODEL), lambda l: (l, 0, 0)),
                  pl.BlockSpec((1, 1, 3 * D_MODEL), lambda l: (l, 0, 0))],
        out_specs=pl.BlockSpec((1, n, 3 * D_MODEL), lambda l: (l, 0, 0)),
        out_shape=jax.ShapeDtypeStruct((depth, n, 3 * D_MODEL), f32),
        compiler_params=pltpu.CompilerParams(dimension_semantics=("arbitrary",),
                                             vmem_limit_bytes=VMEM_LIMIT_BYTES),
        name="cond",
    )(c_all, w_cond, b_cond.reshape(depth, 1, 3 * D_MODEL))


def _prompt_kernel(x_ref, mod_ref, win_ref, wlru_ref, wlora_ref, wg_ref, bsx_ref, seg_ref, ltri_ref,
                   wbr_ref, wout_ref, v512_ref, mu_ref, v1024_ref,
                   y_ref, lconv_ref, lh_ref, rshift_ref, rs_ref, cconv_ref,
                   h_ref, lext_ref, lhc_ref, rprev_ref, s_ref, cext_ref, ycv_ref, ob_ref, rw_ref,
                   ecc_ref, o_ref, rhat_ref, mp_ref, nn_ref, *, alpha):
    T = TIME_TILE
    C = RWKV_CHUNK
    i = pl.program_id(1)
    last = pl.num_programs(1) - 1

    def vec(row):
        return v512_ref[row:row + 1, :]

    @pl.when(i == 0)
    def _init():
        lext_ref[0:LRU_HALO, :] = jnp.zeros((LRU_HALO, BR_W), f32)
        lhc_ref[...] = jnp.zeros(lhc_ref.shape, f32)
        rprev_ref[...] = jnp.zeros(rprev_ref.shape, f32)
        s_ref[...] = jnp.zeros(s_ref.shape, f32)
        cext_ref[0:CONF_HALO, :] = jnp.zeros((CONF_HALO, BR_W), f32)

    shift = mod_ref[0, 0:1, :]
    scale = mod_ref[0, 1:2, :]
    gate = mod_ref[0, 2:3, :]
    h_ref[...] = (x_ref[0] * (1.0 + scale) + shift).astype(bf16)

    def proj(off, n):
        return jnp.dot(h_ref[...], win_ref[:, off:off + n], preferred_element_type=f32)

    pa = proj(OFF_LRU, 2 * BR_W)
    xb = pa[:, :BR_W]
    lext_ref[LRU_HALO:LRU_HALO + T, :] = xb
    xc = vec(V_LCB)
    for j in range(LRU_CONV):
        xc = xc + vec(V_LCW + j) * lext_ref[pl.ds(LRU_HALO - (LRU_CONV - 1) + j, T), :]

    @pl.when(i == last)
    def _lru_conv_state():
        lconv_ref[0] = lext_ref[pl.ds(LRU_HALO + T - (LRU_CONV - 1), LRU_CONV - 1), :]

    lext_ref[0:LRU_HALO, :] = lext_ref[T:T + LRU_HALO, :]
    a, u = _lru_gates(xc, wlru_ref, vec)
    a_cum, hs = _scan_rows(a, u)
    hfull = hs + a_cum * lhc_ref[0:1, :]
    lhc_ref[0:1, :] = hfull[T - 1:T, :]

    @pl.when(i == last)
    def _lru_h_state():
        lh_ref[0] = hfull[T - 1:T, :]

    ob_ref[0] = (hfull * _silu(pa[:, BR_W:])).astype(bf16)

    pb = proj(OFF_RWKV, RWKV_SHIFT_W + BR_W)
    p = pb[:, :RWKV_SHIFT_W]
    rows = lax.broadcasted_iota(jnp.int32, p.shape, 0)
    prev = jnp.where(rows == 0, rprev_ref[0:1, :], pltpu.roll(p, 1, axis=0))
    rprev_ref[0:1, :] = p[T - 1:T, :]

    @pl.when(i == last)
    def _rwkv_shift_state():
        rshift_ref[0] = p[T - 1:T, :]

    xs = p + (prev - p) * mu_ref[...]
    r, k2, v, logw, ag, kkn, bonus = _rwkv_prep(xs, vec, wlora_ref, seg_ref)
    lw1 = logw.astype(bf16)
    rem = logw - lw1.astype(f32)
    lw2 = rem.astype(bf16)
    lw3 = (rem - lw2.astype(f32)).astype(bf16)
    ltri = ltri_ref[...]
    c = (jnp.dot(ltri, lw1, preferred_element_type=f32) + jnp.dot(ltri, lw2, preferred_element_type=f32)
         + jnp.dot(ltri, lw3, preferred_element_type=f32))
    cend = jnp.concatenate(
        [jnp.broadcast_to(c[(n + 1) * C - 1:(n + 1) * C, :], (C, BR_W)) for n in range(T // C)], axis=0)
    for n in range(T // C):
        ecc_ref[n * SUBLANES:(n + 1) * SUBLANES, :] = jnp.broadcast_to(
            jnp.exp(c[(n + 1) * C - 1:(n + 1) * C, :]), (SUBLANES, BR_W))
    e_neg = jnp.exp(-c)
    e_end = jnp.exp(cend - c)
    beta = kkn * ag
    rw_ref[0] = (-kkn * jnp.exp(c - logw)).astype(bf16)
    rw_ref[1] = (r * jnp.exp(c)).astype(bf16)
    rw_ref[2] = (beta * e_neg).astype(bf16)
    rw_ref[3] = (k2 * e_neg).astype(bf16)
    rw_ref[4] = v.astype(bf16)
    rw_ref[5] = (beta * e_end).astype(bf16)
    rw_ref[6] = (k2 * e_end).astype(bf16)

    R = lax.broadcasted_iota(jnp.int32, (2 * C, LANES), 0)
    Cc = lax.broadcasted_iota(jnp.int32, (2 * C, LANES), 1)
    t_idx = R & (C - 1)
    s_idx = Cc & (C - 1)
    mask_a = jnp.where(R < C, (s_idx < t_idx).astype(f32), (s_idx <= t_idx).astype(f32))
    block_mask = ((R >> CHUNK_SHIFT) == (Cc >> CHUNK_SHIFT)).astype(f32)
    eye = (R == Cc).astype(f32)
    lvl_masks = [((R >> 1) == (Cc >> 1)).astype(f32)]
    sh = 1
    while (1 << sh) < C:
        lvl_masks.append((((R >> (sh + 1)) == (Cc >> (sh + 1))) & ((R >> sh) != (Cc >> sh))).astype(f32))
        sh += 1
    left64 = lax.broadcasted_iota(jnp.int32, (C, LANES), 1) < C
    left128 = Cc < C

    def sel_l(x, m):
        return jnp.where(m, x, jnp.zeros_like(x))

    def sel_r(x, m):
        return jnp.where(m, jnp.zeros_like(x), x)

    left256 = (lax.broadcasted_iota(jnp.int32, (C, 2 * LANES), 1) & (LANES - 1)) < C
    npair = RWKV_HEADS // 2

    def split_lr(x, m):
        return jnp.concatenate([sel_l(x, m), sel_r(x, m)], axis=0)

    items = [(n, q) for n in range(T // C) for q in range(npair)]
    for g0 in range(0, len(items), RWKV_GROUP):
        grp = items[g0:g0 + RWKV_GROUP]

        def ld(kind):
            return [rw_ref[kind, n * C:(n + 1) * C, q * LANES:(q + 1) * LANES] for n, q in grp]

        at, rt, bt, kt, vv, b_end, k_end = [ld(kind) for kind in range(7)]
        ar = [jnp.concatenate([a_, r_], axis=0) for a_, r_ in zip(at, rt)]
        out_e = [_bdot_nt(sel_l(x, left128), jnp.concatenate([b_, k_], axis=0)) * mask_a
                 for x, b_, k_ in zip(ar, bt, kt)]
        out_o = [_bdot_nt(sel_r(x, left128), jnp.concatenate([k_, b_], axis=0)) * mask_a
                 for x, b_, k_ in zip(ar, bt, kt)]
        ae = [x[0:C] for x in out_e]
        ao = [x[0:C] for x in out_o]
        re = [x[C:2 * C] for x in out_e]
        ro = [x[C:2 * C] for x in out_o]
        v_rl = [jnp.concatenate([sel_r(x, left64), sel_l(x, left64)], axis=0) for x in vv]
        a_pair = [jnp.concatenate([sel_l(e_, left64), sel_r(o_, left64)], axis=0) for e_, o_ in zip(ae, ao)]
        tm = [eye + x * lvl_masks[0] for x in a_pair]
        for lm in lvl_masks[1:]:
            w = [_bdot(x * lm, t_) for x, t_ in zip(a_pair, tm)]
            tm = [t_ + _bdot(t_, w_) for t_, w_ in zip(tm, w)]
        t_row = [(t_[0:C] + t_[C:2 * C]).astype(bf16) for t_ in tm]
        av = [_bdot(jnp.where(left64, o_, e_), x) for o_, e_, x in zip(ao, ae, v_rl)]
        tx = [jnp.dot(t_, split_lr(jnp.concatenate([a_, x.astype(bf16)], axis=1), left256),
                      preferred_element_type=f32).astype(bf16)
              for t_, a_, x in zip(t_row, at, av)]
        arb = [_bdot(jnp.where(left64, e_, o_), split_lr(x, left256)) for e_, o_, x in zip(re, ro, tx)]
        ark = [_bdot(jnp.where(left64, o_, e_), x) for o_, e_, x in zip(ro, re, v_rl)]
        mp = [_bdot_tn(x[:, 0:LANES], b_) * block_mask for x, b_ in zip(tx, b_end)]
        nn = [_bdot_tn(jnp.concatenate([x[:, LANES:2 * LANES], v_], axis=0),
                       jnp.concatenate([b_, k_], axis=0)) * block_mask
              for x, v_, b_, k_ in zip(tx, vv, b_end, k_end)]
        for idx, (n, q) in enumerate(grp):
            rs = slice(n * C, (n + 1) * C)
            ls = slice(q * LANES, (q + 1) * LANES)
            rhat_ref[rs, ls] = (rt[idx].astype(f32) + arb[idx][:, 0:LANES]).astype(bf16)
            o_ref[rs, ls] = arb[idx][:, LANES:2 * LANES] + ark[idx]
            mp_ref[n * npair + q] = mp[idx].astype(bf16)
            nn_ref[n * npair + q] = nn[idx]

    for n in range(T // C):
        rs = slice(n * C, (n + 1) * C)
        sp = [s_ref[q] for q in range(npair)]
        sb = [x.astype(bf16) for x in sp]
        o_blk = [_bdot_nt(rhat_ref[rs, q * LANES:(q + 1) * LANES], sb[q]) for q in range(npair)]
        s_new = [jnp.dot(sb[q], mp_ref[n * npair + q], preferred_element_type=f32) for q in range(npair)]
        for q in range(npair):
            ls = slice(q * LANES, (q + 1) * LANES)
            o_ref[rs, ls] = o_ref[rs, ls] + o_blk[q]
            s_ref[q] = sp[q] * ecc_ref[n * SUBLANES:n * SUBLANES + 1, ls] + s_new[q] + nn_ref[n * npair + q]

    @pl.when(i == last)
    def _rwkv_s_state():
        for q in range(RWKV_HEADS // 2):
            sp = s_ref[q]
            rs_ref[0, 2 * q] = sp[0:C, 0:C]
            rs_ref[0, 2 * q + 1] = sp[C:2 * C, C:2 * C]

    ob_ref[1] = (_rwkv_post(o_ref[...], bonus, vec, seg_ref) * _silu(pb[:, RWKV_SHIFT_W:])).astype(bf16)

    pc = proj(OFF_GMLP, 3 * BR_W)
    vn = _layernorm(pc[:, BR_W:2 * BR_W], vec(V_GLG), vec(V_GLB))
    left_g = lax.broadcasted_iota(jnp.int32, (GMLP_CHUNK, LANES), 1) < (LANES // 2)
    z_rows = []
    for n in range(T // GMLP_CHUNK):
        z_cols = []
        for q in range(GMLP_GROUPS // 2):
            vq = vn[n * GMLP_CHUNK:(n + 1) * GMLP_CHUNK, q * LANES:(q + 1) * LANES].astype(bf16)
            rhs = jnp.concatenate([sel_l(vq, left_g), sel_r(vq, left_g)], axis=0)
            z_cols.append(jnp.dot(wg_ref[q], rhs, preferred_element_type=f32))
        z_rows.append(jnp.concatenate(z_cols, axis=1) + bsx_ref[...])
    z = jnp.concatenate(z_rows, axis=0)
    ob_ref[2] = (pc[:, :BR_W] * z * _silu(pc[:, 2 * BR_W:])).astype(bf16)

    pd = proj(OFF_CONF, 3 * BR_W)
    cext_ref[CONF_HALO:CONF_HALO + T, :] = pd[:, :BR_W] * _sigmoid(pd[:, BR_W:2 * BR_W])
    base = CONF_HALO - (CONF_K - 1)
    for rb in range(T // CONV_ROWS):
        for lb in range(BR_W // LANES):
            ls = slice(lb * LANES, (lb + 1) * LANES)
            acc = jnp.broadcast_to(v512_ref[V_CDB:V_CDB + 1, ls], (CONV_ROWS, LANES))
            for j in range(CONF_K):
                acc = acc + v512_ref[V_CDW + j:V_CDW + j + 1, ls] * cext_ref[pl.ds(rb * CONV_ROWS + base + j, CONV_ROWS), ls]
            ycv_ref[rb * CONV_ROWS:(rb + 1) * CONV_ROWS, ls] = acc

    @pl.when(i == last)
    def _conf_state():
        cconv_ref[0] = cext_ref[pl.ds(T + base, CONF_K - 1), :]

    cext_ref[0:CONF_HALO, :] = cext_ref[T:T + CONF_HALO, :]
    ob_ref[3] = (_silu(_layernorm(ycv_ref[...], vec(V_CLG), vec(V_CLB))) * _silu(pd[:, 2 * BR_W:])).astype(bf16)

    m = None
    for nb in range(N_BRANCH):
        term = _sigmoid(proj(OFF_MERGE + nb * D_MODEL, D_MODEL)) * jnp.dot(
            ob_ref[nb], wbr_ref[nb], preferred_element_type=f32)
        m = term if m is None else m + term
    yv = jnp.dot(m.astype(bf16), wout_ref[...], preferred_element_type=f32) + v1024_ref[M_BOUT:M_BOUT + 1, :]
    xn = alpha * x_ref[0] + gate * yv
    y_ref[0] = _layernorm(xn, v1024_ref[M_LNG:M_LNG + 1, :], v1024_ref[M_LNB:M_LNB + 1, :])


def _resident(shape):
    nd = len(shape)
    return pl.BlockSpec(shape, lambda b, i: (0,) * nd, pipeline_mode=pl.Buffered(1))


def _prompt_call(x, mod3, lw, alpha):
    nb, seq, _ = x.shape
    T = TIME_TILE
    assert seq % T == 0 and T % GMLP_CHUNK == 0 and T % RWKV_CHUNK == 0
    nt = seq // T
    in_specs = [
        pl.BlockSpec((1, T, D_MODEL), lambda b, i: (b, i, 0)),
        pl.BlockSpec((1, 3, D_MODEL), lambda b, i: (b, 0, 0)),
        _resident(lw["win"].shape), _resident(lw["wlru"].shape), _resident(lw["wlora"].shape),
        _resident(lw["wg"].shape), _resident(lw["bsx"].shape), _resident(lw["seg"].shape),
        _resident(lw["ltri"].shape), _resident(lw["wbr"].shape), _resident(lw["wout"].shape),
        _resident(lw["v512"].shape), _resident(lw["mu"].shape), _resident(lw["v1024"].shape),
    ]
    out_shape = (
        jax.ShapeDtypeStruct((nb, seq, D_MODEL), f32),
        jax.ShapeDtypeStruct((nb, LRU_CONV - 1, BR_W), f32),
        jax.ShapeDtypeStruct((nb, 1, BR_W), f32),
        jax.ShapeDtypeStruct((nb, 1, RWKV_SHIFT_W), f32),
        jax.ShapeDtypeStruct((nb, RWKV_HEADS, RWKV_HD, RWKV_HD), f32),
        jax.ShapeDtypeStruct((nb, CONF_K - 1, BR_W), f32),
    )
    out_specs = (
        pl.BlockSpec((1, T, D_MODEL), lambda b, i: (b, i, 0)),
        pl.BlockSpec((1, LRU_CONV - 1, BR_W), lambda b, i: (b, 0, 0)),
        pl.BlockSpec((1, 1, BR_W), lambda b, i: (b, 0, 0)),
        pl.BlockSpec((1, 1, RWKV_SHIFT_W), lambda b, i: (b, 0, 0)),
        pl.BlockSpec((1, RWKV_HEADS, RWKV_HD, RWKV_HD), lambda b, i: (b, 0, 0, 0)),
        pl.BlockSpec((1, CONF_K - 1, BR_W), lambda b, i: (b, 0, 0)),
    )
    scratch = [
        pltpu.VMEM((T, D_MODEL), bf16),
        pltpu.VMEM((T + LRU_HALO, BR_W), f32),
        pltpu.VMEM((SUBLANES, BR_W), f32),
        pltpu.VMEM((SUBLANES, RWKV_SHIFT_W), f32),
        pltpu.VMEM((RWKV_HEADS // 2, 2 * RWKV_HD, 2 * RWKV_HD), f32),
        pltpu.VMEM((T + CONF_HALO, BR_W), f32),
        pltpu.VMEM((T, BR_W), f32),
        pltpu.VMEM((N_BRANCH, T, BR_W), bf16),
        pltpu.VMEM((7, T, BR_W), bf16),
        pltpu.VMEM((T // RWKV_CHUNK * SUBLANES, BR_W), f32),
        pltpu.VMEM((T, BR_W), f32),
        pltpu.VMEM((T, BR_W), bf16),
        pltpu.VMEM((T // RWKV_CHUNK * (RWKV_HEADS // 2), 2 * RWKV_HD, 2 * RWKV_HD), bf16),
        pltpu.VMEM((T // RWKV_CHUNK * (RWKV_HEADS // 2), 2 * RWKV_HD, 2 * RWKV_HD), f32),
    ]
    return pl.pallas_call(
        functools.partial(_prompt_kernel, alpha=alpha),
        grid=(nb, nt),
        in_specs=in_specs,
        out_specs=out_specs,
        out_shape=out_shape,
        scratch_shapes=scratch,
        compiler_params=pltpu.CompilerParams(dimension_semantics=("arbitrary", "arbitrary"),
                                             vmem_limit_bytes=VMEM_LIMIT_BYTES),
        name="prompt_layer",
    )(x, mod3, lw["win"], lw["wlru"], lw["wlora"], lw["wg"], lw["bsx"], lw["seg"], lw["ltri"],
      lw["wbr"], lw["wout"], lw["v512"], lw["mu"], lw["v1024"])


def _sample_proj_kernel(x_ref, mod_ref, w_ref, o_ref):
    shift = mod_ref[:, 0:D_MODEL]
    scale = mod_ref[:, D_MODEL:2 * D_MODEL]
    h = (x_ref[...] * (1.0 + scale) + shift).astype(bf16)
    o_ref[...] = jnp.dot(h, w_ref[...], preferred_element_type=f32)


SAMPLE_PROJ_COLS = 1152


def _sample_proj_call(x, mod, win):
    n = x.shape[0]
    assert N_IN % SAMPLE_PROJ_COLS == 0
    return pl.pallas_call(
        _sample_proj_kernel,
        grid=(N_IN // SAMPLE_PROJ_COLS,),
        in_specs=[pl.BlockSpec((n, D_MODEL), lambda j: (0, 0)),
                  pl.BlockSpec((n, 3 * D_MODEL), lambda j: (0, 0)),
                  pl.BlockSpec((D_MODEL, SAMPLE_PROJ_COLS), lambda j: (0, j))],
        out_specs=pl.BlockSpec((n, SAMPLE_PROJ_COLS), lambda j: (0, j)),
        out_shape=jax.ShapeDtypeStruct((n, N_IN), f32),
        compiler_params=pltpu.CompilerParams(dimension_semantics=("arbitrary",),
                                             vmem_limit_bytes=VMEM_LIMIT_BYTES),
        name="sample_proj",
    )(x, mod, win)


def _sample_state_kernel(p_ref, lconv_ref, lh_ref, rshift_ref, rs_ref, cconv_ref,
                         wlru_ref, wlora_ref, seg_ref, v512_ref, mu_ref,
                         ob_ref, lconv_o, lh_o, rshift_o, rs_o, cconv_o, gv_o):
    nb = SAMPLE_BLOCK

    def vec(row):
        return v512_ref[row:row + 1, :]

    xb = p_ref[:, OFF_LRU:OFF_LRU + BR_W]
    xc = vec(V_LCB) + vec(V_LCW + LRU_CONV - 1) * xb
    for j in range(LRU_CONV - 1):
        xc = xc + vec(V_LCW + j) * lconv_ref[:, j, :]
    lconv_o[:, 0:LRU_CONV - 2, :] = lconv_ref[:, 1:LRU_CONV - 1, :]
    lconv_o[:, LRU_CONV - 2, :] = xb
    a, u = _lru_gates(xc, wlru_ref, vec)
    hn = a * lh_ref[...] + u
    lh_o[...] = hn
    ob_ref[:, 0:BR_W] = hn * _silu(p_ref[:, OFF_LRU + BR_W:OFF_LRU + 2 * BR_W])

    p = p_ref[:, OFF_RWKV:OFF_RWKV + RWKV_SHIFT_W]
    xs = p + (rshift_ref[...] - p) * mu_ref[...]
    rshift_o[...] = p
    r, k2, v, logw, ag, kkn, bonus = _rwkv_prep(xs, vec, wlora_ref, seg_ref)
    w = jnp.exp(logw)
    beta = kkn * ag
    n = RWKV_HD
    diag = (lax.broadcasted_iota(jnp.int32, (n, n), 0) == lax.broadcasted_iota(jnp.int32, (n, n), 1))
    o_rows = []
    for b in range(nb):
        o_heads = []
        for hd in range(RWKV_HEADS):
            ls = slice(hd * n, (hd + 1) * n)
            s0 = rs_ref[b, hd]
            sa = -jnp.sum(s0 * kkn[b:b + 1, ls], axis=-1, keepdims=True)
            v_col = jnp.sum(jnp.where(diag, v[b:b + 1, ls], 0.0), axis=-1, keepdims=True)
            sn = s0 * w[b:b + 1, ls] + sa * beta[b:b + 1, ls] + v_col * k2[b:b + 1, ls]
            rs_o[b, hd] = sn
            o_col = jnp.sum(sn * r[b:b + 1, ls], axis=-1, keepdims=True)
            o_heads.append(jnp.sum(jnp.where(diag, o_col, 0.0), axis=0, keepdims=True))
        o_rows.append(jnp.concatenate(o_heads, axis=1))
    o = jnp.concatenate(o_rows, axis=0)
    gb = p_ref[:, OFF_RWKV + RWKV_SHIFT_W:OFF_RWKV + RWKV_SHIFT_W + BR_W]
    ob_ref[:, BR_W:2 * BR_W] = _rwkv_post(o, bonus, vec, seg_ref) * _silu(gb)

    vn = _layernorm(p_ref[:, OFF_GMLP + BR_W:OFF_GMLP + 2 * BR_W], vec(V_GLG), vec(V_GLB))
    gv_o[...] = vn
    z = vec(V_GWS0) * vn + vec(V_GBS0)
    ob_ref[:, 2 * BR_W:3 * BR_W] = (p_ref[:, OFF_GMLP:OFF_GMLP + BR_W] * z
                                    * _silu(p_ref[:, OFF_GMLP + 2 * BR_W:OFF_GMLP + 3 * BR_W]))

    glu = p_ref[:, OFF_CONF:OFF_CONF + BR_W] * _sigmoid(p_ref[:, OFF_CONF + BR_W:OFF_CONF + 2 * BR_W])
    taps = v512_ref[V_CDW:V_CDW + CONF_K - 1, :]
    y = (vec(V_CDB) + vec(V_CDW + CONF_K - 1) * glu
         + jnp.sum(cconv_ref[...] * taps[None, :, :], axis=1))
    cconv_o[:, 0:CONF_K - 2, :] = cconv_ref[:, 1:CONF_K - 1, :]
    cconv_o[:, CONF_K - 2, :] = glu
    ob_ref[:, 3 * BR_W:4 * BR_W] = (_silu(_layernorm(y, vec(V_CLG), vec(V_CLB)))
                                    * _silu(p_ref[:, OFF_CONF + 2 * BR_W:OFF_CONF + 3 * BR_W]))


def _sample_state_call(proj, lconv, lh, rshift, rs, cconv, lw):
    n = proj.shape[0]
    nb = SAMPLE_BLOCK
    assert n % nb == 0

    def blk(shape):
        nd = len(shape)
        return pl.BlockSpec((nb,) + shape, lambda i: (i,) + (0,) * nd)

    def res(shape):
        nd = len(shape)
        return pl.BlockSpec(shape, lambda i: (0,) * nd, pipeline_mode=pl.Buffered(1))

    state_specs = [blk((LRU_CONV - 1, BR_W)), blk((BR_W,)), blk((RWKV_SHIFT_W,)),
                   blk((RWKV_HEADS, RWKV_HD, RWKV_HD)), blk((CONF_K - 1, BR_W))]
    in_specs = ([pl.BlockSpec((nb, OFF_MERGE), lambda i: (i, 0))] + state_specs
                + [res(lw["wlru"].shape), res(lw["wlora"].shape), res(lw["seg"].shape),
                   res(lw["v512"].shape), res(lw["mu"].shape)])
    out_specs = [blk((N_BRANCH * BR_W,))] + state_specs + [blk((BR_W,))]
    out_shape = [jax.ShapeDtypeStruct((n, N_BRANCH * BR_W), f32),
                 jax.ShapeDtypeStruct(lconv.shape, f32), jax.ShapeDtypeStruct(lh.shape, f32),
                 jax.ShapeDtypeStruct(rshift.shape, f32), jax.ShapeDtypeStruct(rs.shape, f32),
                 jax.ShapeDtypeStruct(cconv.shape, f32), jax.ShapeDtypeStruct((n, BR_W), f32)]
    return pl.pallas_call(
        _sample_state_kernel,
        grid=(n // nb,),
        in_specs=in_specs,
        out_specs=out_specs,
        out_shape=out_shape,
        compiler_params=pltpu.CompilerParams(dimension_semantics=("arbitrary",),
                                             vmem_limit_bytes=VMEM_LIMIT_BYTES),
        name="sample_state",
    )(proj, lconv, lh, rshift, rs, cconv, lw["wlru"], lw["wlora"], lw["seg"], lw["v512"], lw["mu"])


def _sample_merge_kernel(x_ref, mod_ref, p_ref, ob_ref, wbr_ref, wout_ref, v1024_ref, y_ref, *, alpha):
    m = None
    for nb in range(N_BRANCH):
        off = OFF_MERGE + nb * D_MODEL
        term = _sigmoid(p_ref[:, off:off + D_MODEL]) * _bdot(
            ob_ref[:, nb * BR_W:(nb + 1) * BR_W], wbr_ref[nb])
        m = term if m is None else m + term
    yv = _bdot(m, wout_ref[...]) + v1024_ref[M_BOUT:M_BOUT + 1, :]
    xn = alpha * x_ref[...] + mod_ref[:, 2 * D_MODEL:3 * D_MODEL] * yv
    y_ref[...] = _layernorm(xn, v1024_ref[M_LNG:M_LNG + 1, :], v1024_ref[M_LNB:M_LNB + 1, :])


def _sample_merge_call(x, mod, proj, ob, lw, alpha):
    n = x.shape[0]
    return pl.pallas_call(
        functools.partial(_sample_merge_kernel, alpha=alpha),
        grid=(1,),
        in_specs=[pl.BlockSpec((n, D_MODEL), lambda i: (0, 0)),
                  pl.BlockSpec((n, 3 * D_MODEL), lambda i: (0, 0)),
                  pl.BlockSpec((n, N_IN), lambda i: (0, 0)),
                  pl.BlockSpec((n, N_BRANCH * BR_W), lambda i: (0, 0)),
                  pl.BlockSpec(lw["wbr"].shape, lambda i: (0, 0, 0)),
                  pl.BlockSpec(lw["wout"].shape, lambda i: (0, 0)),
                  pl.BlockSpec(lw["v1024"].shape, lambda i: (0, 0))],
        out_specs=pl.BlockSpec((n, D_MODEL), lambda i: (0, 0)),
        out_shape=jax.ShapeDtypeStruct((n, D_MODEL), f32),
        compiler_params=pltpu.CompilerParams(dimension_semantics=("arbitrary",),
                                             vmem_limit_bytes=VMEM_LIMIT_BYTES),
        name="sample_merge",
    )(x, mod, proj, ob, lw["wbr"], lw["wout"], lw["v1024"])


def _pack_layer(l, w_in, lru_conv_w, lru_conv_b, lru_wr, lru_br, lru_wi, lru_bi, lru_lambda, rwkv_mu,
                rwkv_w0, rwkv_ww, rwkv_a0, rwkv_wa, rwkv_kk, rwkv_ka, rwkv_rk, rwkv_lnx_g, rwkv_lnx_b,
                gmlp_ln_g, gmlp_ln_b, gmlp_ws, gmlp_bs, conf_dw_w, conf_dw_b, conf_ln_g, conf_ln_b,
                w_branch, w_out, b_out, ln_g, ln_b):
    rep = BR_W // GMLP_GROUPS
    wr = block_diag(*[lru_wr[l, h] for h in range(LRU_HEADS)])
    wi = block_diag(*[lru_wi[l, h] for h in range(LRU_HEADS)])
    zero_lora = jnp.zeros((RWKV_RANK, BR_W), f32)
    wlora = jnp.concatenate([jnp.concatenate([rwkv_ww[l], zero_lora], axis=1),
                             jnp.concatenate([zero_lora, rwkv_wa[l]], axis=1)], axis=0)
    tril = jnp.tril(jnp.ones((GMLP_CHUNK, GMLP_CHUNK), dtype=bool))
    wm = jnp.where(tril[None], gmlp_ws[l], 0.0)
    wg = jnp.stack([jnp.concatenate([wm[2 * q], wm[2 * q + 1]], axis=1) for q in range(GMLP_GROUPS // 2)])
    rows = [lru_conv_w[l], lru_conv_b[l][None], lru_br[l][None], lru_bi[l][None], lru_lambda[l][None],
            rwkv_w0[l][None], rwkv_a0[l][None], rwkv_kk[l][None], rwkv_ka[l][None],
            rwkv_rk[l].reshape(1, BR_W), rwkv_lnx_g[l][None], rwkv_lnx_b[l][None],
            gmlp_ln_g[l][None], gmlp_ln_b[l][None], conf_dw_b[l][None], conf_ln_g[l][None],
            conf_ln_b[l][None], jnp.repeat(gmlp_ws[l, :, 0, 0], rep)[None],
            jnp.repeat(gmlp_bs[l, :, 0], rep)[None], jnp.zeros((V_CDW - V_GBS0 - 1, BR_W), f32),
            conf_dw_w[l], jnp.zeros((V_ROWS - V_CDW - CONF_K, BR_W), f32)]
    v1024 = jnp.concatenate([b_out[l][None], ln_g[l][None], ln_b[l][None],
                             jnp.zeros((SUBLANES - 3, D_MODEL), f32)], axis=0)
    nchunk = TIME_TILE // RWKV_CHUNK
    return dict(
        win=w_in[l].astype(bf16),
        wlru=jnp.concatenate([wr, wi], axis=1).astype(bf16),
        wlora=wlora.astype(bf16),
        wg=wg.astype(bf16),
        bsx=jnp.repeat(gmlp_bs[l].T, rep, axis=1),
        seg=jnp.kron(jnp.eye(RWKV_HEADS, dtype=f32), jnp.ones((RWKV_HD, RWKV_HD), f32)).astype(bf16),
        ltri=jnp.kron(jnp.eye(nchunk, dtype=f32), jnp.tril(jnp.ones((RWKV_CHUNK, RWKV_CHUNK), f32))).astype(bf16),
        wbr=w_branch[l].astype(bf16),
        wout=w_out[l].astype(bf16),
        v512=jnp.concatenate(rows, axis=0),
        mu=rwkv_mu[l][None],
        v1024=v1024,
    )


def kernel(x_prompt, x_sample, state_lru_conv, state_lru_h, state_rwkv_shift, state_rwkv_S, state_conf_conv, c_prompt, c_sample, w_cond, b_cond, w_in, lru_conv_w, lru_conv_b, lru_wr, lru_br, lru_wi, lru_bi, lru_lambda, rwkv_mu, rwkv_w0, rwkv_ww, rwkv_a0, rwkv_wa, rwkv_kk, rwkv_ka, rwkv_rk, rwkv_lnx_g, rwkv_lnx_b, gmlp_ln_g, gmlp_ln_b, gmlp_ws, gmlp_bs, conf_dw_w, conf_dw_b, conf_ln_g, conf_ln_b, w_branch, w_out, b_out, ln_g, ln_b):
    depth = w_in.shape[0]
    alpha = (2.0 * depth) ** 0.25
    nb = x_prompt.shape[0]
    ns = x_sample.shape[0]

    mod = _cond_call(jnp.concatenate([c_prompt, c_sample], axis=0), w_cond.astype(bf16), b_cond)
    xp = x_prompt
    xs = x_sample.reshape(ns, D_MODEL)
    outs_p, outs_s = [], []
    for l in range(depth):
        lw = _pack_layer(l, w_in, lru_conv_w, lru_conv_b, lru_wr, lru_br, lru_wi, lru_bi, lru_lambda, rwkv_mu,
                         rwkv_w0, rwkv_ww, rwkv_a0, rwkv_wa, rwkv_kk, rwkv_ka, rwkv_rk, rwkv_lnx_g,
                         rwkv_lnx_b, gmlp_ln_g, gmlp_ln_b, gmlp_ws, gmlp_bs, conf_dw_w, conf_dw_b,
                         conf_ln_g, conf_ln_b, w_branch, w_out, b_out, ln_g, ln_b)
        mod_p = mod[l, :nb].reshape(nb, 3, D_MODEL)
        mod_s = mod[l, nb:]
        xp, lconv_p, lh_p, rshift_p, rs_p, cconv_p = _prompt_call(xp, mod_p, lw, alpha)
        outs_p.append((lconv_p, lh_p.reshape(nb, BR_W), rshift_p.reshape(nb, RWKV_SHIFT_W), rs_p, cconv_p))

        proj_s = _sample_proj_call(xs, mod_s, lw["win"])
        ob, lconv_s, lh_s, rshift_s, rs_s, cconv_s, gv_s = _sample_state_call(
            proj_s, state_lru_conv[l], state_lru_h[l], state_rwkv_shift[l], state_rwkv_S[l],
            state_conf_conv[l], lw)
        xs = _sample_merge_call(xs, mod_s, proj_s, ob, lw, alpha)
        outs_s.append((lconv_s, lh_s, rshift_s, rs_s, cconv_s, gv_s.reshape(ns, 1, BR_W)))

    def stk(outs, j):
        return jnp.stack([o[j] for o in outs])

    return (xp, xs.reshape(ns, 1, D_MODEL),
            stk(outs_p, 0), stk(outs_s, 0),
            stk(outs_p, 1), stk(outs_s, 1),
            stk(outs_p, 2), stk(outs_s, 2),
            stk(outs_p, 3), stk(outs_s, 3),
            stk(outs_p, 4), stk(outs_s, 4),
            stk(outs_s, 5))
```

```python
import functools

import jax
import jax.numpy as jnp
from jax import lax
from jax.experimental import pallas as pl
from jax.experimental.pallas import tpu as pltpu
from jax.scipy.linalg import block_diag

f32 = jnp.float32
bf16 = jnp.bfloat16

D_MODEL = 1024
N_BRANCH = 4
BR_W = D_MODEL // 2
LRU_HEADS = 8
LRU_CONV = 4
LRU_C = 8.0
RWKV_HD = 64
RWKV_HEADS = BR_W // RWKV_HD
RWKV_RANK = D_MODEL // 16
RWKV_SHIFT_W = 3 * BR_W + 2 * RWKV_RANK
RWKV_DECAY_SCALE = 0.606531
RWKV_LNX_EPS = 64e-5
GMLP_CHUNK = 128
GMLP_GROUPS = 8
CONF_K = 31
LN_EPS = 1e-5

OFF_LRU = 0
OFF_RWKV = OFF_LRU + 2 * BR_W
OFF_GMLP = OFF_RWKV + RWKV_SHIFT_W + BR_W
OFF_CONF = OFF_GMLP + 3 * BR_W
OFF_MERGE = OFF_CONF + 3 * BR_W
N_IN = OFF_MERGE + N_BRANCH * D_MODEL

LANES = 128
SUBLANES = 8
VMEM_LIMIT_BYTES = 60 * 1024 * 1024

TIME_TILE = 256
RWKV_CHUNK = 64
CHUNK_SHIFT = RWKV_CHUNK.bit_length() - 1
RWKV_GROUP = 16
SAMPLE_BLOCK = 8
SAMPLE_GROUP = 2
CONV_ROWS = 64
LRU_HALO = SUBLANES
CONF_HALO = 32

V_LCW, V_LCB, V_LBR, V_LBI, V_LAM = 0, 4, 5, 6, 7
V_W0, V_A0, V_KK, V_KA, V_RK, V_LNXG, V_LNXB = 8, 9, 10, 11, 12, 13, 14
V_GLG, V_GLB, V_CDB, V_CLG, V_CLB, V_GWS0, V_GBS0 = 15, 16, 17, 18, 19, 20, 21
V_CDW = 24
V_ROWS = 56
M_BOUT, M_LNG, M_LNB = 0, 1, 2


def _sigmoid(x):
    return 0.5 * jnp.tanh(0.5 * x) + 0.5


def _silu(x):
    return x * _sigmoid(x)


def _softplus(z):
    return jnp.maximum(z, 0.0) + jnp.log1p(jnp.exp(-jnp.abs(z)))


def _layernorm(x, g, b, eps=LN_EPS):
    mu = jnp.mean(x, axis=-1, keepdims=True)
    xc = x - mu
    var = jnp.mean(xc * xc, axis=-1, keepdims=True)
    return xc * lax.rsqrt(var + eps) * g + b


def _bdot(a, b):
    return jnp.dot(a.astype(bf16), b.astype(bf16), preferred_element_type=f32)


def _bdot_nt(a, b):
    return lax.dot_general(a.astype(bf16), b.astype(bf16), (((1,), (1,)), ((), ())),
                           preferred_element_type=f32)


def _bdot_tn(a, b):
    return lax.dot_general(a.astype(bf16), b.astype(bf16), (((0,), (0,)), ((), ())),
                           preferred_element_type=f32)


def _segsum(x, seg_ref):
    hi = x.astype(bf16)
    lo = (x - hi.astype(f32)).astype(bf16)
    seg = seg_ref[...]
    return (jnp.dot(hi, seg, preferred_element_type=f32)
            + jnp.dot(lo, seg, preferred_element_type=f32))


def _lru_gates(xc, rg, vec):
    r = _sigmoid(rg[:, :BR_W] + vec(V_LBR))
    ig = _sigmoid(rg[:, BR_W:] + vec(V_LBI))
    log_a = -LRU_C * r * _softplus(-vec(V_LAM))
    a = jnp.exp(log_a)
    mult = jnp.sqrt(-jnp.tanh(log_a) * (a * a + 1.0))
    return a, mult * (ig * xc)


def _rwkv_prep(xs, vec, wlora_ref, seg_ref):
    r = xs[:, 0:BR_W]
    k = xs[:, BR_W:2 * BR_W]
    v = xs[:, 2 * BR_W:3 * BR_W]
    dwa = xs[:, 3 * BR_W:3 * BR_W + 2 * RWKV_RANK]
    lane = lax.broadcasted_iota(jnp.int32, dwa.shape, 1)
    lora_in = jnp.where(lane < RWKV_RANK, jnp.tanh(dwa), dwa)
    wa = jnp.dot(lora_in.astype(bf16), wlora_ref[...], preferred_element_type=f32)
    logw = -RWKV_DECAY_SCALE * _sigmoid(vec(V_W0) + wa[:, :BR_W])
    a = _sigmoid(vec(V_A0) + wa[:, BR_W:])
    kk = k * vec(V_KK)
    kkn = kk / jnp.maximum(jnp.sqrt(_segsum(kk * kk, seg_ref)), 1e-12)
    k2 = k * (1.0 + (a - 1.0) * vec(V_KA))
    bonus = _segsum(r * k2 * vec(V_RK), seg_ref) * v
    return r, k2, v, logw, a, kkn, bonus


def _rwkv_post(o, bonus, vec, seg_ref):
    inv_n = 1.0 / RWKV_HD
    mean = _segsum(o, seg_ref) * inv_n
    oc = o - mean
    var = _segsum(oc * oc, seg_ref) * inv_n
    return oc * lax.rsqrt(var + RWKV_LNX_EPS) * vec(V_LNXG) + vec(V_LNXB) + bonus


def _scan_rows(a, u):
    n = a.shape[0]
    rows = lax.broadcasted_iota(jnp.int32, a.shape, 0)
    d = 1
    while d < n:
        keep = rows >= d
        a_s = jnp.where(keep, pltpu.roll(a, d, axis=0), 1.0)
        u_s = jnp.where(keep, pltpu.roll(u, d, axis=0), 0.0)
        u = u + a * u_s
        a = a * a_s
        d *= 2
    return a, u


def _cond_kernel(c_ref, w_ref, b_ref, o_ref):
    o_ref[0] = _bdot(_silu(c_ref[...]), w_ref[0]) + b_ref[0]


def _cond_call(c_all, w_cond, b_cond):
    depth = w_cond.shape[0]
    n = c_all.shape[0]
    return pl.pallas_call(
        _cond_kernel,
        grid=(depth,),
        in_specs=[pl.BlockSpec((n, D_MODEL), lambda l: (0, 0)),
                  pl.BlockSpec((1, D_MODEL, 3 * D_MODEL), lambda l: (l, 0, 0)),
                  pl.BlockSpec((1, 1, 3 * D_MODEL), lambda l: (l, 0, 0))],
        out_specs=pl.BlockSpec((1, n, 3 * D_MODEL), lambda l: (l, 0, 0)),
        out_shape=jax.ShapeDtypeStruct((depth, n, 3 * D_MODEL), f32),
        compiler_params=pltpu.CompilerParams(dimension_semantics=("arbitrary",),
                                             vmem_limit_bytes=VMEM_LIMIT_BYTES),
        name="cond",
    )(c_all, w_cond, b_cond.reshape(depth, 1, 3 * D_MODEL))


def _prompt_kernel(x_ref, mod_ref, win_ref, wlru_ref, wlora_ref, wg_ref, bsx_ref, seg_ref, ltri_ref,
                   wbr_ref, wout_ref, v512_ref, mu_ref, v1024_ref,
                   y_ref, lconv_ref, lh_ref, rshift_ref, rs_ref, cconv_ref,
                   h_ref, lext_ref, lhc_ref, rprev_ref, s_ref, cext_ref, ycv_ref, ob_ref, rw_ref,
                   ecc_ref, o_ref, rhat_ref, mp_ref, nn_ref, *, alpha):
    T = TIME_TILE
    C = RWKV_CHUNK
    npair = RWKV_HEADS // 2
    first = pl.program_id(1) == 0

    def vec(row):
        return v512_ref[row:row + 1, :]

    def carried(x):
        return jnp.where(first, jnp.zeros_like(x), x)

    lext_ref[0:LRU_HALO, :] = carried(lext_ref[0:LRU_HALO, :])
    cext_ref[0:CONF_HALO, :] = carried(cext_ref[0:CONF_HALO, :])
    for q in range(npair):
        s_ref[q] = carried(s_ref[q])

    shift = mod_ref[0, 0:1, :]
    scale = mod_ref[0, 1:2, :]
    gate = mod_ref[0, 2:3, :]
    h_ref[...] = (x_ref[0] * (1.0 + scale) + shift).astype(bf16)

    def proj(off, n):
        return jnp.dot(h_ref[...], win_ref[:, off:off + n], preferred_element_type=f32)

    pa = proj(OFF_LRU, 2 * BR_W)
    lext_ref[LRU_HALO:LRU_HALO + T, :] = pa[:, :BR_W]
    xc = vec(V_LCB)
    for j in range(LRU_CONV):
        xc = xc + vec(V_LCW + j) * lext_ref[pl.ds(LRU_HALO - (LRU_CONV - 1) + j, T), :]
    lext_ref[0:LRU_HALO, :] = lext_ref[T:T + LRU_HALO, :]
    rg = jnp.dot(xc.astype(bf16), wlru_ref[...], preferred_element_type=f32)
    pb = proj(OFF_RWKV, RWKV_SHIFT_W + BR_W)
    pd = proj(OFF_CONF, 3 * BR_W)
    pc = proj(OFF_GMLP, 3 * BR_W)

    a, u = _lru_gates(xc, rg, vec)
    a_cum, hs = _scan_rows(a, u)
    hfull = hs + a_cum * carried(lhc_ref[0:1, :])
    lhc_ref[0:1, :] = hfull[T - 1:T, :]
    ob_ref[0] = (hfull * _silu(pa[:, BR_W:])).astype(bf16)

    p = pb[:, :RWKV_SHIFT_W]
    rows = lax.broadcasted_iota(jnp.int32, p.shape, 0)
    prev = jnp.where(rows == 0, carried(rprev_ref[0:1, :]), pltpu.roll(p, 1, axis=0))
    rprev_ref[0:1, :] = p[T - 1:T, :]
    xs = p + (prev - p) * mu_ref[...]
    r, k2, v, logw, ag, kkn, bonus = _rwkv_prep(xs, vec, wlora_ref, seg_ref)
    lw1 = logw.astype(bf16)
    rem = logw - lw1.astype(f32)
    lw2 = rem.astype(bf16)
    lw3 = (rem - lw2.astype(f32)).astype(bf16)
    ltri = ltri_ref[...]
    c = (jnp.dot(ltri, lw1, preferred_element_type=f32) + jnp.dot(ltri, lw2, preferred_element_type=f32)
         + jnp.dot(ltri, lw3, preferred_element_type=f32))
    cend = jnp.concatenate(
        [jnp.broadcast_to(c[(n + 1) * C - 1:(n + 1) * C, :], (C, BR_W)) for n in range(T // C)], axis=0)
    for n in range(T // C):
        ecc_ref[n * SUBLANES:(n + 1) * SUBLANES, :] = jnp.broadcast_to(
            jnp.exp(c[(n + 1) * C - 1:(n + 1) * C, :]), (SUBLANES, BR_W))
    e_neg = jnp.exp(-c)
    e_end = jnp.exp(cend - c)
    beta = kkn * ag
    rw_ref[0] = (-kkn * jnp.exp(c - logw)).astype(bf16)
    rw_ref[1] = (r * jnp.exp(c)).astype(bf16)
    rw_ref[2] = (beta * e_neg).astype(bf16)
    rw_ref[3] = (k2 * e_neg).astype(bf16)
    rw_ref[4] = v.astype(bf16)
    rw_ref[5] = (beta * e_end).astype(bf16)
    rw_ref[6] = (k2 * e_end).astype(bf16)

    R = lax.broadcasted_iota(jnp.int32, (2 * C, LANES), 0)
    Cc = lax.broadcasted_iota(jnp.int32, (2 * C, LANES), 1)
    t_idx = R & (C - 1)
    s_idx = Cc & (C - 1)
    mask_a = jnp.where(R < C, (s_idx < t_idx).astype(f32), (s_idx <= t_idx).astype(f32))
    block_mask = ((R >> CHUNK_SHIFT) == (Cc >> CHUNK_SHIFT)).astype(f32)
    eye = (R == Cc).astype(f32)
    lvl_masks = [((R >> 1) == (Cc >> 1)).astype(f32)]
    sh = 1
    while (1 << sh) < C:
        lvl_masks.append((((R >> (sh + 1)) == (Cc >> (sh + 1))) & ((R >> sh) != (Cc >> sh))).astype(f32))
        sh += 1
    left64 = lax.broadcasted_iota(jnp.int32, (C, LANES), 1) < C
    left128 = Cc < C

    def sel_l(x, m):
        return jnp.where(m, x, jnp.zeros_like(x))

    def sel_r(x, m):
        return jnp.where(m, jnp.zeros_like(x), x)

    left256 = (lax.broadcasted_iota(jnp.int32, (C, 2 * LANES), 1) & (LANES - 1)) < C

    def split_lr(x, m):
        return jnp.concatenate([sel_l(x, m), sel_r(x, m)], axis=0)

    cext_ref[CONF_HALO:CONF_HALO + T, :] = pd[:, :BR_W] * _sigmoid(pd[:, BR_W:2 * BR_W])
    conf_gate = _silu(pd[:, 2 * BR_W:])

    vn = _layernorm(pc[:, BR_W:2 * BR_W], vec(V_GLG), vec(V_GLB))
    left_g = lax.broadcasted_iota(jnp.int32, (GMLP_CHUNK, LANES), 1) < (LANES // 2)
    z_rows = []
    for n in range(T // GMLP_CHUNK):
        z_cols = []
        for q in range(GMLP_GROUPS // 2):
            vq = vn[n * GMLP_CHUNK:(n + 1) * GMLP_CHUNK, q * LANES:(q + 1) * LANES].astype(bf16)
            rhs = jnp.concatenate([sel_l(vq, left_g), sel_r(vq, left_g)], axis=0)
            z_cols.append(jnp.dot(wg_ref[q], rhs, preferred_element_type=f32))
        z_rows.append(jnp.concatenate(z_cols, axis=1) + bsx_ref[...])
    z = jnp.concatenate(z_rows, axis=0)
    ob_ref[2] = (pc[:, :BR_W] * z * _silu(pc[:, 2 * BR_W:])).astype(bf16)

    items = [(n, q) for n in range(T // C) for q in range(npair)]
    for g0 in range(0, len(items), RWKV_GROUP):
        grp = items[g0:g0 + RWKV_GROUP]

        def ld(kind):
            return [rw_ref[kind, n * C:(n + 1) * C, q * LANES:(q + 1) * LANES] for n, q in grp]

        at, rt, bt, kt, vv, b_end, k_end = [ld(kind) for kind in range(7)]
        ar = [jnp.concatenate([a_, r_], axis=0) for a_, r_ in zip(at, rt)]
        out_e = [_bdot_nt(sel_l(x, left128), jnp.concatenate([b_, k_], axis=0)) * mask_a
                 for x, b_, k_ in zip(ar, bt, kt)]
        out_o = [_bdot_nt(sel_r(x, left128), jnp.concatenate([k_, b_], axis=0)) * mask_a
                 for x, b_, k_ in zip(ar, bt, kt)]
        ae = [x[0:C] for x in out_e]
        ao = [x[0:C] for x in out_o]
        re = [x[C:2 * C] for x in out_e]
        ro = [x[C:2 * C] for x in out_o]
        v_rl = [jnp.concatenate([sel_r(x, left64), sel_l(x, left64)], axis=0) for x in vv]
        a_pair = [jnp.concatenate([sel_l(e_, left64), sel_r(o_, left64)], axis=0) for e_, o_ in zip(ae, ao)]
        tm = [eye + x * lvl_masks[0] for x in a_pair]
        for lm in lvl_masks[1:]:
            w = [_bdot(x * lm, t_) for x, t_ in zip(a_pair, tm)]
            tm = [t_ + _bdot(t_, w_) for t_, w_ in zip(tm, w)]
        t_row = [(t_[0:C] + t_[C:2 * C]).astype(bf16) for t_ in tm]
        av = [_bdot(jnp.where(left64, o_, e_), x) for o_, e_, x in zip(ao, ae, v_rl)]
        tx = [jnp.dot(t_, split_lr(jnp.concatenate([a_, x.astype(bf16)], axis=1), left256),
                      preferred_element_type=f32).astype(bf16)
              for t_, a_, x in zip(t_row, at, av)]
        arb = [_bdot(jnp.where(left64, e_, o_), split_lr(x, left256)) for e_, o_, x in zip(re, ro, tx)]
        ark = [_bdot(jnp.where(left64, o_, e_), x) for o_, e_, x in zip(ro, re, v_rl)]
        mp = [_bdot_tn(x[:, 0:LANES], b_) * block_mask for x, b_ in zip(tx, b_end)]
        nn = [_bdot_tn(jnp.concatenate([x[:, LANES:2 * LANES], v_], axis=0),
                       jnp.concatenate([b_, k_], axis=0)) * block_mask
              for x, v_, b_, k_ in zip(tx, vv, b_end, k_end)]
        for idx, (n, q) in enumerate(grp):
            rs = slice(n * C, (n + 1) * C)
            ls = slice(q * LANES, (q + 1) * LANES)
            rhat_ref[rs, ls] = (rt[idx].astype(f32) + arb[idx][:, 0:LANES]).astype(bf16)
            o_ref[rs, ls] = arb[idx][:, LANES:2 * LANES] + ark[idx]
            mp_ref[n * npair + q] = mp[idx].astype(bf16)
            nn_ref[n * npair + q] = nn[idx]

    for n in range(T // C):
        rs = slice(n * C, (n + 1) * C)
        sp = [s_ref[q] for q in range(npair)]
        sb = [x.astype(bf16) for x in sp]
        o_blk = [_bdot_nt(rhat_ref[rs, q * LANES:(q + 1) * LANES], sb[q]) for q in range(npair)]
        s_new = [jnp.dot(sb[q], mp_ref[n * npair + q], preferred_element_type=f32) for q in range(npair)]
        for q in range(npair):
            ls = slice(q * LANES, (q + 1) * LANES)
            o_ref[rs, ls] = o_ref[rs, ls] + o_blk[q]
            s_ref[q] = sp[q] * ecc_ref[n * SUBLANES:n * SUBLANES + 1, ls] + s_new[q] + nn_ref[n * npair + q]

    ob_ref[1] = (_rwkv_post(o_ref[...], bonus, vec, seg_ref) * _silu(pb[:, RWKV_SHIFT_W:])).astype(bf16)

    def merge_term(nb):
        return _sigmoid(proj(OFF_MERGE + nb * D_MODEL, D_MODEL)) * jnp.dot(
            ob_ref[nb], wbr_ref[nb], preferred_element_type=f32)

    m = merge_term(0) + merge_term(1) + merge_term(2)

    base = CONF_HALO - (CONF_K - 1)
    for rb in range(T // CONV_ROWS):
        for lb in range(BR_W // LANES):
            ls = slice(lb * LANES, (lb + 1) * LANES)
            acc = jnp.broadcast_to(v512_ref[V_CDB:V_CDB + 1, ls], (CONV_ROWS, LANES))
            for j in range(CONF_K):
                acc = acc + v512_ref[V_CDW + j:V_CDW + j + 1, ls] * cext_ref[pl.ds(rb * CONV_ROWS + base + j, CONV_ROWS), ls]
            ycv_ref[rb * CONV_ROWS:(rb + 1) * CONV_ROWS, ls] = acc
    cext_ref[0:CONF_HALO, :] = cext_ref[T:T + CONF_HALO, :]
    ob_ref[3] = (_silu(_layernorm(ycv_ref[...], vec(V_CLG), vec(V_CLB))) * conf_gate).astype(bf16)

    m = m + merge_term(3)
    yv = jnp.dot(m.astype(bf16), wout_ref[...], preferred_element_type=f32) + v1024_ref[M_BOUT:M_BOUT + 1, :]
    xn = alpha * x_ref[0] + gate * yv
    y_ref[0] = _layernorm(xn, v1024_ref[M_LNG:M_LNG + 1, :], v1024_ref[M_LNB:M_LNB + 1, :])

    lconv_ref[0] = lext_ref[LRU_HALO - (LRU_CONV - 1):LRU_HALO, :]
    lh_ref[0] = lhc_ref[0:1, :]
    rshift_ref[0] = rprev_ref[0:1, :]
    cconv_ref[0] = cext_ref[CONF_HALO - (CONF_K - 1):CONF_HALO, :]
    for q in range(npair):
        sp = s_ref[q]
        rs_ref[0, 2 * q] = sp[0:C, 0:C]
        rs_ref[0, 2 * q + 1] = sp[C:2 * C, C:2 * C]


def _resident(shape):
    nd = len(shape)
    return pl.BlockSpec(shape, lambda b, i: (0,) * nd, pipeline_mode=pl.Buffered(1))


def _resident_layer(shape, l):
    nd = len(shape) - 1
    return pl.BlockSpec((None,) + tuple(shape[1:]), lambda b, i: (l,) + (0,) * nd,
                        pipeline_mode=pl.Buffered(1))


def _prompt_call(x, mod3, lw, big, l, alpha):
    nb, seq, _ = x.shape
    T = TIME_TILE
    assert seq % T == 0 and T % GMLP_CHUNK == 0 and T % RWKV_CHUNK == 0
    nt = seq // T
    in_specs = [
        pl.BlockSpec((1, T, D_MODEL), lambda b, i: (b, i, 0)),
        pl.BlockSpec((1, 3, D_MODEL), lambda b, i: (b, 0, 0)),
        _resident_layer(big["win"].shape, l), _resident(lw["wlru"].shape), _resident(lw["wlora"].shape),
        _resident(lw["wg"].shape), _resident(lw["bsx"].shape), _resident(lw["seg"].shape),
        _resident(lw["ltri"].shape), _resident_layer(big["wbr"].shape, l),
        _resident_layer(big["wout"].shape, l),
        _resident(lw["v512"].shape), _resident(lw["mu"].shape), _resident(lw["v1024"].shape),
    ]
    out_shape = (
        jax.ShapeDtypeStruct((nb, seq, D_MODEL), f32),
        jax.ShapeDtypeStruct((nb, LRU_CONV - 1, BR_W), f32),
        jax.ShapeDtypeStruct((nb, 1, BR_W), f32),
        jax.ShapeDtypeStruct((nb, 1, RWKV_SHIFT_W), f32),
        jax.ShapeDtypeStruct((nb, RWKV_HEADS, RWKV_HD, RWKV_HD), f32),
        jax.ShapeDtypeStruct((nb, CONF_K - 1, BR_W), f32),
    )
    out_specs = (
        pl.BlockSpec((1, T, D_MODEL), lambda b, i: (b, i, 0)),
        pl.BlockSpec((1, LRU_CONV - 1, BR_W), lambda b, i: (b, 0, 0)),
        pl.BlockSpec((1, 1, BR_W), lambda b, i: (b, 0, 0)),
        pl.BlockSpec((1, 1, RWKV_SHIFT_W), lambda b, i: (b, 0, 0)),
        pl.BlockSpec((1, RWKV_HEADS, RWKV_HD, RWKV_HD), lambda b, i: (b, 0, 0, 0)),
        pl.BlockSpec((1, CONF_K - 1, BR_W), lambda b, i: (b, 0, 0)),
    )
    scratch = [
        pltpu.VMEM((T, D_MODEL), bf16),
        pltpu.VMEM((T + LRU_HALO, BR_W), f32),
        pltpu.VMEM((SUBLANES, BR_W), f32),
        pltpu.VMEM((SUBLANES, RWKV_SHIFT_W), f32),
        pltpu.VMEM((RWKV_HEADS // 2, 2 * RWKV_HD, 2 * RWKV_HD), f32),
        pltpu.VMEM((T + CONF_HALO, BR_W), f32),
        pltpu.VMEM((T, BR_W), f32),
        pltpu.VMEM((N_BRANCH, T, BR_W), bf16),
        pltpu.VMEM((7, T, BR_W), bf16),
        pltpu.VMEM((T // RWKV_CHUNK * SUBLANES, BR_W), f32),
        pltpu.VMEM((T, BR_W), f32),
        pltpu.VMEM((T, BR_W), bf16),
        pltpu.VMEM((T // RWKV_CHUNK * (RWKV_HEADS // 2), 2 * RWKV_HD, 2 * RWKV_HD), bf16),
        pltpu.VMEM((T // RWKV_CHUNK * (RWKV_HEADS // 2), 2 * RWKV_HD, 2 * RWKV_HD), f32),
    ]
    return pl.pallas_call(
        functools.partial(_prompt_kernel, alpha=alpha),
        grid=(nb, nt),
        in_specs=in_specs,
        out_specs=out_specs,
        out_shape=out_shape,
        scratch_shapes=scratch,
        compiler_params=pltpu.CompilerParams(dimension_semantics=("arbitrary", "arbitrary"),
                                             vmem_limit_bytes=VMEM_LIMIT_BYTES),
        name="prompt_layer",
    )(x, mod3, big["win"], lw["wlru"], lw["wlora"], lw["wg"], lw["bsx"], lw["seg"], lw["ltri"],
      big["wbr"], big["wout"], lw["v512"], lw["mu"], lw["v1024"])


def _sample_proj_kernel(x_ref, mod_ref, w_ref, o_ref):
    shift = mod_ref[:, 0:D_MODEL]
    scale = mod_ref[:, D_MODEL:2 * D_MODEL]
    h = (x_ref[...] * (1.0 + scale) + shift).astype(bf16)
    o_ref[...] = jnp.dot(h, w_ref[...], preferred_element_type=f32)


SAMPLE_PROJ_COLS = 1152


def _sample_proj_call(x, mod, win, l):
    n = x.shape[0]
    assert N_IN % SAMPLE_PROJ_COLS == 0
    return pl.pallas_call(
        _sample_proj_kernel,
        grid=(N_IN // SAMPLE_PROJ_COLS,),
        in_specs=[pl.BlockSpec((n, D_MODEL), lambda j: (0, 0)),
                  pl.BlockSpec((n, 3 * D_MODEL), lambda j: (0, 0)),
                  pl.BlockSpec((None, D_MODEL, SAMPLE_PROJ_COLS), lambda j: (l, 0, j))],
        out_specs=pl.BlockSpec((n, SAMPLE_PROJ_COLS), lambda j: (0, j)),
        out_shape=jax.ShapeDtypeStruct((n, N_IN), f32),
        compiler_params=pltpu.CompilerParams(dimension_semantics=("arbitrary",),
                                             vmem_limit_bytes=VMEM_LIMIT_BYTES),
        name="sample_proj",
    )(x, mod, win)


def _sample_state_kernel(p_ref, lconv_ref, lh_ref, rshift_ref, rs_ref, cconv_ref,
                         wlru_ref, wlora_ref, seg_ref, v512_ref, mu_ref,
                         ob_ref, lconv_o, lh_o, rshift_o, rs_o, cconv_o, gv_o):
    nb = SAMPLE_BLOCK

    def vec(row):
        return v512_ref[row:row + 1, :]

    xb = p_ref[:, OFF_LRU:OFF_LRU + BR_W]
    xc = vec(V_LCB) + vec(V_LCW + LRU_CONV - 1) * xb
    for j in range(LRU_CONV - 1):
        xc = xc + vec(V_LCW + j) * lconv_ref[:, j, :]
    lconv_o[:, 0:LRU_CONV - 2, :] = lconv_ref[:, 1:LRU_CONV - 1, :]
    lconv_o[:, LRU_CONV - 2, :] = xb
    a, u = _lru_gates(xc, jnp.dot(xc.astype(bf16), wlru_ref[...], preferred_element_type=f32), vec)
    hn = a * lh_ref[...] + u
    lh_o[...] = hn
    ob_ref[:, 0:BR_W] = hn * _silu(p_ref[:, OFF_LRU + BR_W:OFF_LRU + 2 * BR_W])

    p = p_ref[:, OFF_RWKV:OFF_RWKV + RWKV_SHIFT_W]
    xs = p + (rshift_ref[...] - p) * mu_ref[...]
    rshift_o[...] = p
    r, k2, v, logw, ag, kkn, bonus = _rwkv_prep(xs, vec, wlora_ref, seg_ref)
    w = jnp.exp(logw)
    beta = kkn * ag
    n = RWKV_HD
    diag = (lax.broadcasted_iota(jnp.int32, (n, n), 0) == lax.broadcasted_iota(jnp.int32, (n, n), 1))
    def head_row(x, b, hd):
        return x[b:b + 1, hd * n:(hd + 1) * n]

    o_rows = []
    for b0 in range(0, nb, SAMPLE_GROUP):
        items = [(b, hd) for b in range(b0, b0 + SAMPLE_GROUP) for hd in range(RWKV_HEADS)]
        s0 = [rs_ref[b, hd] for b, hd in items]
        sa = [-jnp.sum(s * head_row(kkn, *it), axis=-1, keepdims=True) for s, it in zip(s0, items)]
        v_col = [jnp.sum(jnp.where(diag, head_row(v, *it), 0.0), axis=-1, keepdims=True) for it in items]
        sn = [s * head_row(w, *it) + a_ * head_row(beta, *it) + c_ * head_row(k2, *it)
              for s, a_, c_, it in zip(s0, sa, v_col, items)]
        for s, (b, hd) in zip(sn, items):
            rs_o[b, hd] = s
        o_col = [jnp.sum(s * head_row(r, *it), axis=-1, keepdims=True) for s, it in zip(sn, items)]
        o_row = [jnp.sum(jnp.where(diag, c_, 0.0), axis=0, keepdims=True) for c_ in o_col]
        for j in range(SAMPLE_GROUP):
            o_rows.append(jnp.concatenate(o_row[j * RWKV_HEADS:(j + 1) * RWKV_HEADS], axis=1))
    o = jnp.concatenate(o_rows, axis=0)
    gb = p_ref[:, OFF_RWKV + RWKV_SHIFT_W:OFF_RWKV + RWKV_SHIFT_W + BR_W]
    ob_ref[:, BR_W:2 * BR_W] = _rwkv_post(o, bonus, vec, seg_ref) * _silu(gb)

    vn = _layernorm(p_ref[:, OFF_GMLP + BR_W:OFF_GMLP + 2 * BR_W], vec(V_GLG), vec(V_GLB))
    gv_o[...] = vn
    z = vec(V_GWS0) * vn + vec(V_GBS0)
    ob_ref[:, 2 * BR_W:3 * BR_W] = (p_ref[:, OFF_GMLP:OFF_GMLP + BR_W] * z
                                    * _silu(p_ref[:, OFF_GMLP + 2 * BR_W:OFF_GMLP + 3 * BR_W]))

    glu = p_ref[:, OFF_CONF:OFF_CONF + BR_W] * _sigmoid(p_ref[:, OFF_CONF + BR_W:OFF_CONF + 2 * BR_W])
    taps = v512_ref[V_CDW:V_CDW + CONF_K - 1, :]
    y = (vec(V_CDB) + vec(V_CDW + CONF_K - 1) * glu
         + jnp.sum(cconv_ref[...] * taps[None, :, :], axis=1))
    cconv_o[:, 0:CONF_K - 2, :] = cconv_ref[:, 1:CONF_K - 1, :]
    cconv_o[:, CONF_K - 2, :] = glu
    ob_ref[:, 3 * BR_W:4 * BR_W] = (_silu(_layernorm(y, vec(V_CLG), vec(V_CLB)))
                                    * _silu(p_ref[:, OFF_CONF + 2 * BR_W:OFF_CONF + 3 * BR_W]))


def _sample_state_call(proj, lconv, lh, rshift, rs, cconv, lw, l):
    n = proj.shape[0]
    nb = SAMPLE_BLOCK
    assert n % nb == 0
    state_shapes = [(LRU_CONV - 1, BR_W), (BR_W,), (RWKV_SHIFT_W,), (RWKV_HEADS, RWKV_HD, RWKV_HD),
                    (CONF_K - 1, BR_W)]

    def blk(shape):
        nd = len(shape)
        return pl.BlockSpec((nb,) + shape, lambda i: (i,) + (0,) * nd)

    def blk_layer(shape):
        nd = len(shape)
        return pl.BlockSpec((None, nb) + shape, lambda i: (l, i) + (0,) * nd)

    def res(shape):
        nd = len(shape)
        return pl.BlockSpec(shape, lambda i: (0,) * nd, pipeline_mode=pl.Buffered(1))

    in_specs = ([pl.BlockSpec((nb, OFF_MERGE), lambda i: (i, 0))] + [blk_layer(s) for s in state_shapes]
                + [res(lw["wlru"].shape), res(lw["wlora"].shape), res(lw["seg"].shape),
                   res(lw["v512"].shape), res(lw["mu"].shape)])
    out_specs = [blk((N_BRANCH * BR_W,))] + [blk(s) for s in state_shapes] + [blk((BR_W,))]
    out_shape = ([jax.ShapeDtypeStruct((n, N_BRANCH * BR_W), f32)]
                 + [jax.ShapeDtypeStruct((n,) + s, f32) for s in state_shapes]
                 + [jax.ShapeDtypeStruct((n, BR_W), f32)])
    return pl.pallas_call(
        _sample_state_kernel,
        grid=(n // nb,),
        in_specs=in_specs,
        out_specs=out_specs,
        out_shape=out_shape,
        compiler_params=pltpu.CompilerParams(dimension_semantics=("arbitrary",),
                                             vmem_limit_bytes=VMEM_LIMIT_BYTES),
        name="sample_state",
    )(proj, lconv, lh, rshift, rs, cconv, lw["wlru"], lw["wlora"], lw["seg"], lw["v512"], lw["mu"])


def _sample_merge_kernel(x_ref, mod_ref, p_ref, ob_ref, wbr_ref, wout_ref, v1024_ref, y_ref, *, alpha):
    m = None
    for nb in range(N_BRANCH):
        off = OFF_MERGE + nb * D_MODEL
        term = _sigmoid(p_ref[:, off:off + D_MODEL]) * _bdot(
            ob_ref[:, nb * BR_W:(nb + 1) * BR_W], wbr_ref[nb])
        m = term if m is None else m + term
    yv = _bdot(m, wout_ref[...]) + v1024_ref[M_BOUT:M_BOUT + 1, :]
    xn = alpha * x_ref[...] + mod_ref[:, 2 * D_MODEL:3 * D_MODEL] * yv
    y_ref[...] = _layernorm(xn, v1024_ref[M_LNG:M_LNG + 1, :], v1024_ref[M_LNB:M_LNB + 1, :])


def _sample_merge_call(x, mod, proj, ob, lw, big, l, alpha):
    n = x.shape[0]
    return pl.pallas_call(
        functools.partial(_sample_merge_kernel, alpha=alpha),
        grid=(1,),
        in_specs=[pl.BlockSpec((n, D_MODEL), lambda i: (0, 0)),
                  pl.BlockSpec((n, 3 * D_MODEL), lambda i: (0, 0)),
                  pl.BlockSpec((n, N_IN), lambda i: (0, 0)),
                  pl.BlockSpec((n, N_BRANCH * BR_W), lambda i: (0, 0)),
                  pl.BlockSpec((None,) + big["wbr"].shape[1:], lambda i: (l, 0, 0, 0)),
                  pl.BlockSpec((None,) + big["wout"].shape[1:], lambda i: (l, 0, 0)),
                  pl.BlockSpec(lw["v1024"].shape, lambda i: (0, 0))],
        out_specs=pl.BlockSpec((n, D_MODEL), lambda i: (0, 0)),
        out_shape=jax.ShapeDtypeStruct((n, D_MODEL), f32),
        compiler_params=pltpu.CompilerParams(dimension_semantics=("arbitrary",),
                                             vmem_limit_bytes=VMEM_LIMIT_BYTES),
        name="sample_merge",
    )(x, mod, proj, ob, big["wbr"], big["wout"], lw["v1024"])


def _pack_layer(l, w_in, lru_conv_w, lru_conv_b, lru_wr, lru_br, lru_wi, lru_bi, lru_lambda, rwkv_mu,
                rwkv_w0, rwkv_ww, rwkv_a0, rwkv_wa, rwkv_kk, rwkv_ka, rwkv_rk, rwkv_lnx_g, rwkv_lnx_b,
                gmlp_ln_g, gmlp_ln_b, gmlp_ws, gmlp_bs, conf_dw_w, conf_dw_b, conf_ln_g, conf_ln_b,
                w_branch, w_out, b_out, ln_g, ln_b):
    rep = BR_W // GMLP_GROUPS
    wr = block_diag(*[lru_wr[l, h] for h in range(LRU_HEADS)])
    wi = block_diag(*[lru_wi[l, h] for h in range(LRU_HEADS)])
    zero_lora = jnp.zeros((RWKV_RANK, BR_W), f32)
    wlora = jnp.concatenate([jnp.concatenate([rwkv_ww[l], zero_lora], axis=1),
                             jnp.concatenate([zero_lora, rwkv_wa[l]], axis=1)], axis=0)
    tril = jnp.tril(jnp.ones((GMLP_CHUNK, GMLP_CHUNK), dtype=bool))
    wm = jnp.where(tril[None], gmlp_ws[l], 0.0)
    wg = jnp.stack([jnp.concatenate([wm[2 * q], wm[2 * q + 1]], axis=1) for q in range(GMLP_GROUPS // 2)])
    rows = [lru_conv_w[l], lru_conv_b[l][None], lru_br[l][None], lru_bi[l][None], lru_lambda[l][None],
            rwkv_w0[l][None], rwkv_a0[l][None], rwkv_kk[l][None], rwkv_ka[l][None],
            rwkv_rk[l].reshape(1, BR_W), rwkv_lnx_g[l][None], rwkv_lnx_b[l][None],
            gmlp_ln_g[l][None], gmlp_ln_b[l][None], conf_dw_b[l][None], conf_ln_g[l][None],
            conf_ln_b[l][None], jnp.repeat(gmlp_ws[l, :, 0, 0], rep)[None],
            jnp.repeat(gmlp_bs[l, :, 0], rep)[None], jnp.zeros((V_CDW - V_GBS0 - 1, BR_W), f32),
            conf_dw_w[l], jnp.zeros((V_ROWS - V_CDW - CONF_K, BR_W), f32)]
    v1024 = jnp.concatenate([b_out[l][None], ln_g[l][None], ln_b[l][None],
                             jnp.zeros((SUBLANES - 3, D_MODEL), f32)], axis=0)
    nchunk = TIME_TILE // RWKV_CHUNK
    return dict(
        wlru=jnp.concatenate([wr, wi], axis=1).astype(bf16),
        wlora=wlora.astype(bf16),
        wg=wg.astype(bf16),
        bsx=jnp.repeat(gmlp_bs[l].T, rep, axis=1),
        seg=jnp.kron(jnp.eye(RWKV_HEADS, dtype=f32), jnp.ones((RWKV_HD, RWKV_HD), f32)).astype(bf16),
        ltri=jnp.kron(jnp.eye(nchunk, dtype=f32), jnp.tril(jnp.ones((RWKV_CHUNK, RWKV_CHUNK), f32))).astype(bf16),
        v512=jnp.concatenate(rows, axis=0),
        mu=rwkv_mu[l][None],
        v1024=v1024,
    )


def kernel(x_prompt, x_sample, state_lru_conv, state_lru_h, state_rwkv_shift, state_rwkv_S, state_conf_conv, c_prompt, c_sample, w_cond, b_cond, w_in, lru_conv_w, lru_conv_b, lru_wr, lru_br, lru_wi, lru_bi, lru_lambda, rwkv_mu, rwkv_w0, rwkv_ww, rwkv_a0, rwkv_wa, rwkv_kk, rwkv_ka, rwkv_rk, rwkv_lnx_g, rwkv_lnx_b, gmlp_ln_g, gmlp_ln_b, gmlp_ws, gmlp_bs, conf_dw_w, conf_dw_b, conf_ln_g, conf_ln_b, w_branch, w_out, b_out, ln_g, ln_b):
    depth = w_in.shape[0]
    alpha = (2.0 * depth) ** 0.25
    nb = x_prompt.shape[0]
    ns = x_sample.shape[0]

    mod = _cond_call(jnp.concatenate([c_prompt, c_sample], axis=0), w_cond.astype(bf16), b_cond)
    big = dict(win=w_in.astype(bf16), wbr=w_branch.astype(bf16), wout=w_out.astype(bf16))
    xp = x_prompt
    xs = x_sample.reshape(ns, D_MODEL)
    outs_p, outs_s = [], []
    for l in range(depth):
        lw = _pack_layer(l, w_in, lru_conv_w, lru_conv_b, lru_wr, lru_br, lru_wi, lru_bi, lru_lambda, rwkv_mu,
                         rwkv_w0, rwkv_ww, rwkv_a0, rwkv_wa, rwkv_kk, rwkv_ka, rwkv_rk, rwkv_lnx_g,
                         rwkv_lnx_b, gmlp_ln_g, gmlp_ln_b, gmlp_ws, gmlp_bs, conf_dw_w, conf_dw_b,
                         conf_ln_g, conf_ln_b, w_branch, w_out, b_out, ln_g, ln_b)
        mod_p = mod[l, :nb].reshape(nb, 3, D_MODEL)
        mod_s = mod[l, nb:]
        xp, lconv_p, lh_p, rshift_p, rs_p, cconv_p = _prompt_call(xp, mod_p, lw, big, l, alpha)
        outs_p.append((lconv_p, lh_p.reshape(nb, BR_W), rshift_p.reshape(nb, RWKV_SHIFT_W), rs_p, cconv_p))

        proj_s = _sample_proj_call(xs, mod_s, big["win"], l)
        ob, lconv_s, lh_s, rshift_s, rs_s, cconv_s, gv_s = _sample_state_call(
            proj_s, state_lru_conv, state_lru_h, state_rwkv_shift, state_rwkv_S, state_conf_conv, lw, l)
        xs = _sample_merge_call(xs, mod_s, proj_s, ob, lw, big, l, alpha)
        outs_s.append((lconv_s, lh_s, rshift_s, rs_s, cconv_s, gv_s.reshape(ns, 1, BR_W)))

    def stk(outs, j):
        return jnp.stack([o[j] for o in outs])

    return (xp, xs.reshape(ns, 1, D_MODEL),
            stk(outs_p, 0), stk(outs_s, 0),
            stk(outs_p, 1), stk(outs_s, 1),
            stk(outs_p, 2), stk(outs_s, 2),
            stk(outs_p, 3), stk(outs_s, 3),
            stk(outs_p, 4), stk(outs_s, 4),
            stk(outs_s, 5))
```

```python
import functools

import jax
import jax.numpy as jnp
from jax import lax
from jax.experimental import pallas as pl
from jax.experimental.pallas import tpu as pltpu
from jax.scipy.linalg import block_diag

f32 = jnp.float32
bf16 = jnp.bfloat16

D_MODEL = 1024
N_BRANCH = 4
BR_W = D_MODEL // 2
LRU_HEADS = 8
LRU_CONV = 4
LRU_C = 8.0
RWKV_HD = 64
RWKV_HEADS = BR_W // RWKV_HD
RWKV_RANK = D_MODEL // 16
RWKV_SHIFT_W = 3 * BR_W + 2 * RWKV_RANK
RWKV_DECAY_SCALE = 0.606531
RWKV_LNX_EPS = 64e-5
GMLP_CHUNK = 128
GMLP_GROUPS = 8
CONF_K = 31
LN_EPS = 1e-5

OFF_LRU = 0
OFF_RWKV = OFF_LRU + 2 * BR_W
OFF_GMLP = OFF_RWKV + RWKV_SHIFT_W + BR_W
OFF_CONF = OFF_GMLP + 3 * BR_W
OFF_MERGE = OFF_CONF + 3 * BR_W
N_IN = OFF_MERGE + N_BRANCH * D_MODEL

LANES = 128
SUBLANES = 8
VMEM_LIMIT_BYTES = 60 * 1024 * 1024

TIME_TILE = 256
RWKV_CHUNK = 64
CHUNK_SHIFT = RWKV_CHUNK.bit_length() - 1
RWKV_GROUP = 16
SAMPLE_BLOCK = 8
SAMPLE_GROUP = 2
CONV_ROWS = 64
LRU_HALO = SUBLANES
CONF_HALO = 32

V_LCW, V_LCB, V_LBR, V_LBI, V_LAM = 0, 4, 5, 6, 7
V_W0, V_A0, V_KK, V_KA, V_RK, V_LNXG, V_LNXB = 8, 9, 10, 11, 12, 13, 14
V_GLG, V_GLB, V_CDB, V_CLG, V_CLB, V_GWS0, V_GBS0 = 15, 16, 17, 18, 19, 20, 21
V_CDW = 24
V_ROWS = 56
M_BOUT, M_LNG, M_LNB = 0, 1, 2


def _sigmoid(x):
    return 0.5 * jnp.tanh(0.5 * x) + 0.5


def _silu(x):
    return x * _sigmoid(x)


def _softplus(z):
    return jnp.maximum(z, 0.0) + jnp.log1p(jnp.exp(-jnp.abs(z)))


def _layernorm(x, g, b, eps=LN_EPS):
    mu = jnp.mean(x, axis=-1, keepdims=True)
    xc = x - mu
    var = jnp.mean(xc * xc, axis=-1, keepdims=True)
    return xc * lax.rsqrt(var + eps) * g + b


def _bdot(a, b):
    return jnp.dot(a.astype(bf16), b.astype(bf16), preferred_element_type=f32)


def _bdot_nt(a, b):
    return lax.dot_general(a.astype(bf16), b.astype(bf16), (((1,), (1,)), ((), ())),
                           preferred_element_type=f32)


def _bdot_tn(a, b):
    return lax.dot_general(a.astype(bf16), b.astype(bf16), (((0,), (0,)), ((), ())),
                           preferred_element_type=f32)


def _segsum(x, seg_ref):
    hi = x.astype(bf16)
    lo = (x - hi.astype(f32)).astype(bf16)
    seg = seg_ref[...]
    return (jnp.dot(hi, seg, preferred_element_type=f32)
            + jnp.dot(lo, seg, preferred_element_type=f32))


def _lru_gates(xc, rg, vec):
    r = _sigmoid(rg[:, :BR_W] + vec(V_LBR))
    ig = _sigmoid(rg[:, BR_W:] + vec(V_LBI))
    log_a = -LRU_C * r * _softplus(-vec(V_LAM))
    a = jnp.exp(log_a)
    mult = jnp.sqrt(-jnp.tanh(log_a) * (a * a + 1.0))
    return a, mult * (ig * xc)


def _rwkv_prep(xs, vec, wlora_ref, seg_ref):
    r = xs[:, 0:BR_W]
    k = xs[:, BR_W:2 * BR_W]
    v = xs[:, 2 * BR_W:3 * BR_W]
    dwa = xs[:, 3 * BR_W:3 * BR_W + 2 * RWKV_RANK]
    lane = lax.broadcasted_iota(jnp.int32, dwa.shape, 1)
    lora_in = jnp.where(lane < RWKV_RANK, jnp.tanh(dwa), dwa)
    wa = jnp.dot(lora_in.astype(bf16), wlora_ref[...], preferred_element_type=f32)
    logw = -RWKV_DECAY_SCALE * _sigmoid(vec(V_W0) + wa[:, :BR_W])
    a = _sigmoid(vec(V_A0) + wa[:, BR_W:])
    kk = k * vec(V_KK)
    kkn = kk / jnp.maximum(jnp.sqrt(_segsum(kk * kk, seg_ref)), 1e-12)
    k2 = k * (1.0 + (a - 1.0) * vec(V_KA))
    bonus = _segsum(r * k2 * vec(V_RK), seg_ref) * v
    return r, k2, v, logw, a, kkn, bonus


def _rwkv_post(o, bonus, vec, seg_ref):
    inv_n = 1.0 / RWKV_HD
    mean = _segsum(o, seg_ref) * inv_n
    oc = o - mean
    var = _segsum(oc * oc, seg_ref) * inv_n
    return oc * lax.rsqrt(var + RWKV_LNX_EPS) * vec(V_LNXG) + vec(V_LNXB) + bonus


def _scan_rows(a, u):
    n = a.shape[0]
    rows = lax.broadcasted_iota(jnp.int32, a.shape, 0)
    d = 1
    while d < n:
        keep = rows >= d
        a_s = jnp.where(keep, pltpu.roll(a, d, axis=0), 1.0)
        u_s = jnp.where(keep, pltpu.roll(u, d, axis=0), 0.0)
        u = u + a * u_s
        a = a * a_s
        d *= 2
    return a, u


def _cond_kernel(c_ref, w_ref, b_ref, o_ref):
    o_ref[0] = _bdot(_silu(c_ref[...]), w_ref[0]) + b_ref[0]


def _cond_call(c_all, w_cond, b_cond):
    depth = w_cond.shape[0]
    n = c_all.shape[0]
    return pl.pallas_call(
        _cond_kernel,
        grid=(depth,),
        in_specs=[pl.BlockSpec((n, D_MODEL), lambda l: (0, 0)),
                  pl.BlockSpec((1, D_MODEL, 3 * D_MODEL), lambda l: (l, 0, 0)),
                  pl.BlockSpec((1, 1, 3 * D_MODEL), lambda l: (l, 0, 0))],
        out_specs=pl.BlockSpec((1, n, 3 * D_MODEL), lambda l: (l, 0, 0)),
        out_shape=jax.ShapeDtypeStruct((depth, n, 3 * D_MODEL), f32),
        compiler_params=pltpu.CompilerParams(dimension_semantics=("arbitrary",),
                                             vmem_limit_bytes=VMEM_LIMIT_BYTES),
        name="cond",
    )(c_all, w_cond, b_cond.reshape(depth, 1, 3 * D_MODEL))


def _prompt_kernel(x_ref, mod_ref, win_ref, wlru_ref, wlora_ref, wg_ref, bsx_ref, seg_ref, ltri_ref,
                   wbr_ref, wout_ref, v512_ref, mu_ref, v1024_ref,
                   y_ref, lconv_ref, lh_ref, rshift_ref, rs_ref, cconv_ref,
                   h_ref, lext_ref, lhc_ref, rprev_ref, s_ref, cext_ref, ycv_ref, ob_ref, rw_ref,
                   ecc_ref, o_ref, rhat_ref, mp_ref, nn_ref, *, alpha):
    T = TIME_TILE
    C = RWKV_CHUNK
    npair = RWKV_HEADS // 2
    first = pl.program_id(1) == 0

    def vec(row):
        return v512_ref[row:row + 1, :]

    def carried(x):
        return jnp.where(first, jnp.zeros_like(x), x)

    lext_ref[0:LRU_HALO, :] = carried(lext_ref[0:LRU_HALO, :])
    cext_ref[0:CONF_HALO, :] = carried(cext_ref[0:CONF_HALO, :])
    for q in range(npair):
        s_ref[q] = carried(s_ref[q])

    shift = mod_ref[0, 0:1, :]
    scale = mod_ref[0, 1:2, :]
    gate = mod_ref[0, 2:3, :]
    h_ref[...] = (x_ref[0] * (1.0 + scale) + shift).astype(bf16)

    def proj(off, n):
        return jnp.dot(h_ref[...], win_ref[:, off:off + n], preferred_element_type=f32)

    pa = proj(OFF_LRU, 2 * BR_W)
    lext_ref[LRU_HALO:LRU_HALO + T, :] = pa[:, :BR_W]
    xc = vec(V_LCB)
    for j in range(LRU_CONV):
        xc = xc + vec(V_LCW + j) * lext_ref[pl.ds(LRU_HALO - (LRU_CONV - 1) + j, T), :]
    lext_ref[0:LRU_HALO, :] = lext_ref[T:T + LRU_HALO, :]
    rg = jnp.dot(xc.astype(bf16), wlru_ref[...], preferred_element_type=f32)
    pb = proj(OFF_RWKV, RWKV_SHIFT_W + BR_W)
    pd = proj(OFF_CONF, 3 * BR_W)
    pc = proj(OFF_GMLP, 3 * BR_W)

    a, u = _lru_gates(xc, rg, vec)
    a_cum, hs = _scan_rows(a, u)
    hfull = hs + a_cum * carried(lhc_ref[0:1, :])
    lhc_ref[0:1, :] = hfull[T - 1:T, :]
    ob_ref[0] = (hfull * _silu(pa[:, BR_W:])).astype(bf16)

    p = pb[:, :RWKV_SHIFT_W]
    rows = lax.broadcasted_iota(jnp.int32, p.shape, 0)
    prev = jnp.where(rows == 0, carried(rprev_ref[0:1, :]), pltpu.roll(p, 1, axis=0))
    rprev_ref[0:1, :] = p[T - 1:T, :]
    xs = p + (prev - p) * mu_ref[...]
    r, k2, v, logw, ag, kkn, bonus = _rwkv_prep(xs, vec, wlora_ref, seg_ref)
    lw1 = logw.astype(bf16)
    rem = logw - lw1.astype(f32)
    lw2 = rem.astype(bf16)
    lw3 = (rem - lw2.astype(f32)).astype(bf16)
    ltri = ltri_ref[...]
    c = (jnp.dot(ltri, lw1, preferred_element_type=f32) + jnp.dot(ltri, lw2, preferred_element_type=f32)
         + jnp.dot(ltri, lw3, preferred_element_type=f32))
    cend = jnp.concatenate(
        [jnp.broadcast_to(c[(n + 1) * C - 1:(n + 1) * C, :], (C, BR_W)) for n in range(T // C)], axis=0)
    for n in range(T // C):
        ecc_ref[n * SUBLANES:(n + 1) * SUBLANES, :] = jnp.broadcast_to(
            jnp.exp(c[(n + 1) * C - 1:(n + 1) * C, :]), (SUBLANES, BR_W))
    e_neg = jnp.exp(-c)
    e_end = jnp.exp(cend - c)
    beta = kkn * ag
    rw_ref[0] = (-kkn * jnp.exp(c - logw)).astype(bf16)
    rw_ref[1] = (r * jnp.exp(c)).astype(bf16)
    rw_ref[2] = (beta * e_neg).astype(bf16)
    rw_ref[3] = (k2 * e_neg).astype(bf16)
    rw_ref[4] = v.astype(bf16)
    rw_ref[5] = (beta * e_end).astype(bf16)
    rw_ref[6] = (k2 * e_end).astype(bf16)

    R = lax.broadcasted_iota(jnp.int32, (2 * C, LANES), 0)
    Cc = lax.broadcasted_iota(jnp.int32, (2 * C, LANES), 1)
    t_idx = R & (C - 1)
    s_idx = Cc & (C - 1)
    mask_a = jnp.where(R < C, (s_idx < t_idx).astype(f32), (s_idx <= t_idx).astype(f32))
    block_mask = ((R >> CHUNK_SHIFT) == (Cc >> CHUNK_SHIFT)).astype(f32)
    eye = (R == Cc).astype(f32)
    lvl_masks = [((R >> 1) == (Cc >> 1)).astype(f32)]
    sh = 1
    while (1 << sh) < C:
        lvl_masks.append((((R >> (sh + 1)) == (Cc >> (sh + 1))) & ((R >> sh) != (Cc >> sh))).astype(f32))
        sh += 1
    left64 = lax.broadcasted_iota(jnp.int32, (C, LANES), 1) < C
    left128 = Cc < C

    def sel_l(x, m):
        return jnp.where(m, x, jnp.zeros_like(x))

    def sel_r(x, m):
        return jnp.where(m, jnp.zeros_like(x), x)

    left256 = (lax.broadcasted_iota(jnp.int32, (C, 2 * LANES), 1) & (LANES - 1)) < C

    def split_lr(x, m):
        return jnp.concatenate([sel_l(x, m), sel_r(x, m)], axis=0)

    cext_ref[CONF_HALO:CONF_HALO + T, :] = pd[:, :BR_W] * _sigmoid(pd[:, BR_W:2 * BR_W])
    conf_gate = _silu(pd[:, 2 * BR_W:])

    vn = _layernorm(pc[:, BR_W:2 * BR_W], vec(V_GLG), vec(V_GLB))
    left_g = lax.broadcasted_iota(jnp.int32, (GMLP_CHUNK, LANES), 1) < (LANES // 2)
    z_rows = []
    for n in range(T // GMLP_CHUNK):
        z_cols = []
        for q in range(GMLP_GROUPS // 2):
            vq = vn[n * GMLP_CHUNK:(n + 1) * GMLP_CHUNK, q * LANES:(q + 1) * LANES].astype(bf16)
            rhs = jnp.concatenate([sel_l(vq, left_g), sel_r(vq, left_g)], axis=0)
            z_cols.append(jnp.dot(wg_ref[q], rhs, preferred_element_type=f32))
        z_rows.append(jnp.concatenate(z_cols, axis=1) + bsx_ref[...])
    z = jnp.concatenate(z_rows, axis=0)
    ob_ref[2] = (pc[:, :BR_W] * z * _silu(pc[:, 2 * BR_W:])).astype(bf16)

    items = [(n, q) for n in range(T // C) for q in range(npair)]
    for g0 in range(0, len(items), RWKV_GROUP):
        grp = items[g0:g0 + RWKV_GROUP]

        def ld(kind):
            return [rw_ref[kind, n * C:(n + 1) * C, q * LANES:(q + 1) * LANES] for n, q in grp]

        at, rt, bt, kt, vv, b_end, k_end = [ld(kind) for kind in range(7)]
        ar = [jnp.concatenate([a_, r_], axis=0) for a_, r_ in zip(at, rt)]
        out_e = [_bdot_nt(sel_l(x, left128), jnp.concatenate([b_, k_], axis=0)) * mask_a
                 for x, b_, k_ in zip(ar, bt, kt)]
        out_o = [_bdot_nt(sel_r(x, left128), jnp.concatenate([k_, b_], axis=0)) * mask_a
                 for x, b_, k_ in zip(ar, bt, kt)]
        ae = [x[0:C] for x in out_e]
        ao = [x[0:C] for x in out_o]
        re = [x[C:2 * C] for x in out_e]
        ro = [x[C:2 * C] for x in out_o]
        v_rl = [jnp.concatenate([sel_r(x, left64), sel_l(x, left64)], axis=0) for x in vv]
        a_pair = [jnp.concatenate([sel_l(e_, left64), sel_r(o_, left64)], axis=0) for e_, o_ in zip(ae, ao)]
        tm = [eye + x * lvl_masks[0] for x in a_pair]
        for lm in lvl_masks[1:]:
            w = [_bdot(x * lm, t_) for x, t_ in zip(a_pair, tm)]
            tm = [t_ + _bdot(t_, w_) for t_, w_ in zip(tm, w)]
        t_row = [(t_[0:C] + t_[C:2 * C]).astype(bf16) for t_ in tm]
        av = [_bdot(jnp.where(left64, o_, e_), x) for o_, e_, x in zip(ao, ae, v_rl)]
        tx = [jnp.dot(t_, split_lr(jnp.concatenate([a_, x.astype(bf16)], axis=1), left256),
                      preferred_element_type=f32).astype(bf16)
              for t_, a_, x in zip(t_row, at, av)]
        arb = [_bdot(jnp.where(left64, e_, o_), split_lr(x, left256)) for e_, o_, x in zip(re, ro, tx)]
        ark = [_bdot(jnp.where(left64, o_, e_), x) for o_, e_, x in zip(ro, re, v_rl)]
        mp = [_bdot_tn(x[:, 0:LANES], b_) * block_mask for x, b_ in zip(tx, b_end)]
        nn = [_bdot_tn(jnp.concatenate([x[:, LANES:2 * LANES], v_], axis=0),
                       jnp.concatenate([b_, k_], axis=0)) * block_mask
              for x, v_, b_, k_ in zip(tx, vv, b_end, k_end)]
        for idx, (n, q) in enumerate(grp):
            rs = slice(n * C, (n + 1) * C)
            ls = slice(q * LANES, (q + 1) * LANES)
            rhat_ref[rs, ls] = (rt[idx].astype(f32) + arb[idx][:, 0:LANES]).astype(bf16)
            o_ref[rs, ls] = arb[idx][:, LANES:2 * LANES] + ark[idx]
            mp_ref[n * npair + q] = mp[idx].astype(bf16)
            nn_ref[n * npair + q] = nn[idx]

    for n in range(T // C):
        rs = slice(n * C, (n + 1) * C)
        sp = [s_ref[q] for q in range(npair)]
        sb = [x.astype(bf16) for x in sp]
        o_blk = [_bdot_nt(rhat_ref[rs, q * LANES:(q + 1) * LANES], sb[q]) for q in range(npair)]
        s_new = [jnp.dot(sb[q], mp_ref[n * npair + q], preferred_element_type=f32) for q in range(npair)]
        for q in range(npair):
            ls = slice(q * LANES, (q + 1) * LANES)
            o_ref[rs, ls] = o_ref[rs, ls] + o_blk[q]
            s_ref[q] = sp[q] * ecc_ref[n * SUBLANES:n * SUBLANES + 1, ls] + s_new[q] + nn_ref[n * npair + q]

    ob_ref[1] = (_rwkv_post(o_ref[...], bonus, vec, seg_ref) * _silu(pb[:, RWKV_SHIFT_W:])).astype(bf16)

    def merge_term(nb):
        return _sigmoid(proj(OFF_MERGE + nb * D_MODEL, D_MODEL)) * jnp.dot(
            ob_ref[nb], wbr_ref[nb], preferred_element_type=f32)

    m = merge_term(0) + merge_term(1) + merge_term(2)

    base = CONF_HALO - (CONF_K - 1)
    for rb in range(T // CONV_ROWS):
        for lb in range(BR_W // LANES):
            ls = slice(lb * LANES, (lb + 1) * LANES)
            acc = jnp.broadcast_to(v512_ref[V_CDB:V_CDB + 1, ls], (CONV_ROWS, LANES))
            for j in range(CONF_K):
                acc = acc + v512_ref[V_CDW + j:V_CDW + j + 1, ls] * cext_ref[pl.ds(rb * CONV_ROWS + base + j, CONV_ROWS), ls]
            ycv_ref[rb * CONV_ROWS:(rb + 1) * CONV_ROWS, ls] = acc
    cext_ref[0:CONF_HALO, :] = cext_ref[T:T + CONF_HALO, :]
    ob_ref[3] = (_silu(_layernorm(ycv_ref[...], vec(V_CLG), vec(V_CLB))) * conf_gate).astype(bf16)

    m = m + merge_term(3)
    yv = jnp.dot(m.astype(bf16), wout_ref[...], preferred_element_type=f32) + v1024_ref[M_BOUT:M_BOUT + 1, :]
    xn = alpha * x_ref[0] + gate * yv
    y_ref[0] = _layernorm(xn, v1024_ref[M_LNG:M_LNG + 1, :], v1024_ref[M_LNB:M_LNB + 1, :])

    lconv_ref[0] = lext_ref[LRU_HALO - (LRU_CONV - 1):LRU_HALO, :]
    lh_ref[0] = lhc_ref[0:1, :]
    rshift_ref[0] = rprev_ref[0:1, :]
    cconv_ref[0] = cext_ref[CONF_HALO - (CONF_K - 1):CONF_HALO, :]
    for q in range(npair):
        sp = s_ref[q]
        rs_ref[0, 2 * q] = sp[0:C, 0:C]
        rs_ref[0, 2 * q + 1] = sp[C:2 * C, C:2 * C]


def _resident(shape):
    nd = len(shape)
    return pl.BlockSpec(shape, lambda b, i: (0,) * nd, pipeline_mode=pl.Buffered(1))


def _resident_layer(shape, l):
    nd = len(shape) - 1
    return pl.BlockSpec((None,) + tuple(shape[1:]), lambda b, i: (l,) + (0,) * nd,
                        pipeline_mode=pl.Buffered(1))


def _prompt_call(x, mod3, lw, big, l, alpha):
    nb, seq, _ = x.shape
    T = TIME_TILE
    assert seq % T == 0 and T % GMLP_CHUNK == 0 and T % RWKV_CHUNK == 0
    nt = seq // T
    in_specs = [
        pl.BlockSpec((1, T, D_MODEL), lambda b, i: (b, i, 0)),
        pl.BlockSpec((1, 3, D_MODEL), lambda b, i: (b, 0, 0)),
        _resident_layer(big["win"].shape, l), _resident(lw["wlru"].shape), _resident(lw["wlora"].shape),
        _resident(lw["wg"].shape), _resident(lw["bsx"].shape), _resident(lw["seg"].shape),
        _resident(lw["ltri"].shape), _resident_layer(big["wbr"].shape, l),
        _resident_layer(big["wout"].shape, l),
        _resident(lw["v512"].shape), _resident(lw["mu"].shape), _resident(lw["v1024"].shape),
    ]
    out_shape = (
        jax.ShapeDtypeStruct((nb, seq, D_MODEL), f32),
        jax.ShapeDtypeStruct((nb, LRU_CONV - 1, BR_W), f32),
        jax.ShapeDtypeStruct((nb, 1, BR_W), f32),
        jax.ShapeDtypeStruct((nb, 1, RWKV_SHIFT_W), f32),
        jax.ShapeDtypeStruct((nb, RWKV_HEADS, RWKV_HD, RWKV_HD), f32),
        jax.ShapeDtypeStruct((nb, CONF_K - 1, BR_W), f32),
    )
    out_specs = (
        pl.BlockSpec((1, T, D_MODEL), lambda b, i: (b, i, 0)),
        pl.BlockSpec((1, LRU_CONV - 1, BR_W), lambda b, i: (b, 0, 0)),
        pl.BlockSpec((1, 1, BR_W), lambda b, i: (b, 0, 0)),
        pl.BlockSpec((1, 1, RWKV_SHIFT_W), lambda b, i: (b, 0, 0)),
        pl.BlockSpec((1, RWKV_HEADS, RWKV_HD, RWKV_HD), lambda b, i: (b, 0, 0, 0)),
        pl.BlockSpec((1, CONF_K - 1, BR_W), lambda b, i: (b, 0, 0)),
    )
    scratch = [
        pltpu.VMEM((T, D_MODEL), bf16),
        pltpu.VMEM((T + LRU_HALO, BR_W), f32),
        pltpu.VMEM((SUBLANES, BR_W), f32),
        pltpu.VMEM((SUBLANES, RWKV_SHIFT_W), f32),
        pltpu.VMEM((RWKV_HEADS // 2, 2 * RWKV_HD, 2 * RWKV_HD), f32),
        pltpu.VMEM((T + CONF_HALO, BR_W), f32),
        pltpu.VMEM((T, BR_W), f32),
        pltpu.VMEM((N_BRANCH, T, BR_W), bf16),
        pltpu.VMEM((7, T, BR_W), bf16),
        pltpu.VMEM((T // RWKV_CHUNK * SUBLANES, BR_W), f32),
        pltpu.VMEM((T, BR_W), f32),
        pltpu.VMEM((T, BR_W), bf16),
        pltpu.VMEM((T // RWKV_CHUNK * (RWKV_HEADS // 2), 2 * RWKV_HD, 2 * RWKV_HD), bf16),
        pltpu.VMEM((T // RWKV_CHUNK * (RWKV_HEADS // 2), 2 * RWKV_HD, 2 * RWKV_HD), f32),
    ]
    return pl.pallas_call(
        functools.partial(_prompt_kernel, alpha=alpha),
        grid=(nb, nt),
        in_specs=in_specs,
        out_specs=out_specs,
        out_shape=out_shape,
        scratch_shapes=scratch,
        compiler_params=pltpu.CompilerParams(dimension_semantics=("arbitrary", "arbitrary"),
                                             vmem_limit_bytes=VMEM_LIMIT_BYTES),
        name="prompt_layer",
    )(x, mod3, big["win"], lw["wlru"], lw["wlora"], lw["wg"], lw["bsx"], lw["seg"], lw["ltri"],
      big["wbr"], big["wout"], lw["v512"], lw["mu"], lw["v1024"])


def _sample_proj_kernel(x_ref, mod_ref, w_ref, o_ref):
    shift = mod_ref[:, 0:D_MODEL]
    scale = mod_ref[:, D_MODEL:2 * D_MODEL]
    h = (x_ref[...] * (1.0 + scale) + shift).astype(bf16)
    o_ref[...] = jnp.dot(h, w_ref[...], preferred_element_type=f32)


SAMPLE_PROJ_COLS = 1152


def _sample_proj_call(x, mod, win, l):
    n = x.shape[0]
    assert N_IN % SAMPLE_PROJ_COLS == 0
    return pl.pallas_call(
        _sample_proj_kernel,
        grid=(N_IN // SAMPLE_PROJ_COLS,),
        in_specs=[pl.BlockSpec((n, D_MODEL), lambda j: (0, 0)),
                  pl.BlockSpec((n, 3 * D_MODEL), lambda j: (0, 0)),
                  pl.BlockSpec((None, D_MODEL, SAMPLE_PROJ_COLS), lambda j: (l, 0, j))],
        out_specs=pl.BlockSpec((n, SAMPLE_PROJ_COLS), lambda j: (0, j)),
        out_shape=jax.ShapeDtypeStruct((n, N_IN), f32),
        compiler_params=pltpu.CompilerParams(dimension_semantics=("arbitrary",),
                                             vmem_limit_bytes=VMEM_LIMIT_BYTES),
        name="sample_proj",
    )(x, mod, win)


def _sample_pre_kernel(p_ref, lconv_ref, lh_ref, rshift_ref, cconv_ref,
                       wlru_ref, wlora_ref, seg_ref, v512_ref, mu_ref,
                       ob_ref, aux_ref, vecs_ref, lconv_o, lh_o, rshift_o, cconv_o, gv_o):
    def vec(row):
        return v512_ref[row:row + 1, :]

    xb = p_ref[:, OFF_LRU:OFF_LRU + BR_W]
    xc = vec(V_LCB) + vec(V_LCW + LRU_CONV - 1) * xb
    for j in range(LRU_CONV - 1):
        xc = xc + vec(V_LCW + j) * lconv_ref[j]
    for j in range(LRU_CONV - 2):
        lconv_o[j] = lconv_ref[j + 1]
    lconv_o[LRU_CONV - 2] = xb
    a, u = _lru_gates(xc, jnp.dot(xc.astype(bf16), wlru_ref[...], preferred_element_type=f32), vec)
    hn = a * lh_ref[...] + u
    lh_o[...] = hn
    ob_ref[:, 0:BR_W] = hn * _silu(p_ref[:, OFF_LRU + BR_W:OFF_LRU + 2 * BR_W])

    p = p_ref[:, OFF_RWKV:OFF_RWKV + RWKV_SHIFT_W]
    xs = p + (rshift_ref[...] - p) * mu_ref[...]
    rshift_o[...] = p
    r, k2, v, logw, ag, kkn, bonus = _rwkv_prep(xs, vec, wlora_ref, seg_ref)
    n = RWKV_HD
    for idx, x in enumerate((-kkn, kkn * ag, k2, r, jnp.exp(logw), v)):
        for hd in range(RWKV_HEADS):
            vecs_ref[idx, hd] = x[:, hd * n:(hd + 1) * n].T
    aux_ref[:, 0:BR_W] = bonus
    aux_ref[:, BR_W:2 * BR_W] = _silu(p_ref[:, OFF_RWKV + RWKV_SHIFT_W:OFF_RWKV + RWKV_SHIFT_W + BR_W])

    vn = _layernorm(p_ref[:, OFF_GMLP + BR_W:OFF_GMLP + 2 * BR_W], vec(V_GLG), vec(V_GLB))
    gv_o[...] = vn
    z = vec(V_GWS0) * vn + vec(V_GBS0)
    ob_ref[:, BR_W:2 * BR_W] = (p_ref[:, OFF_GMLP:OFF_GMLP + BR_W] * z
                                * _silu(p_ref[:, OFF_GMLP + 2 * BR_W:OFF_GMLP + 3 * BR_W]))

    glu = p_ref[:, OFF_CONF:OFF_CONF + BR_W] * _sigmoid(p_ref[:, OFF_CONF + BR_W:OFF_CONF + 2 * BR_W])
    y = vec(V_CDB) + vec(V_CDW + CONF_K - 1) * glu
    for j in range(CONF_K - 1):
        y = y + vec(V_CDW + j) * cconv_ref[j]
    for j in range(CONF_K - 2):
        cconv_o[j] = cconv_ref[j + 1]
    cconv_o[CONF_K - 2] = glu
    ob_ref[:, 2 * BR_W:3 * BR_W] = (_silu(_layernorm(y, vec(V_CLG), vec(V_CLB)))
                                    * _silu(p_ref[:, OFF_CONF + 2 * BR_W:OFF_CONF + 3 * BR_W]))


def _sample_rwkv_kernel(s_ref, vecs_ref, s_o, ot_o):
    neg_kk = vecs_ref[0]
    beta = vecs_ref[1]
    k2 = vecs_ref[2]
    r = vecs_ref[3]
    w = vecs_ref[4]
    o_rows = []
    for vi in range(RWKV_HD):
        s0 = s_ref[vi]
        sa = jnp.sum(s0 * neg_kk, axis=0, keepdims=True)
        sn = s0 * w + sa * beta + vecs_ref[5, vi:vi + 1, :] * k2
        s_o[vi] = sn
        o_rows.append(jnp.sum(sn * r, axis=0, keepdims=True))
    ot_o[...] = jnp.concatenate(o_rows, axis=0)


def _sample_pre_call(proj, lconv_t, lh, rshift, cconv_t, lw, l):
    n = proj.shape[0]

    def whole(shape):
        nd = len(shape)
        return pl.BlockSpec(shape, lambda i: (0,) * nd, pipeline_mode=pl.Buffered(1))

    def layer(shape):
        nd = len(shape)
        return pl.BlockSpec((None,) + shape, lambda i: (l,) + (0,) * nd, pipeline_mode=pl.Buffered(1))

    state_shapes = [(LRU_CONV - 1, n, BR_W), (n, BR_W), (n, RWKV_SHIFT_W), (CONF_K - 1, n, BR_W)]
    out_shapes = ([(n, (N_BRANCH - 1) * BR_W), (n, 2 * BR_W), (6, RWKV_HEADS, RWKV_HD, n)]
                  + state_shapes + [(n, BR_W)])
    return pl.pallas_call(
        _sample_pre_kernel,
        grid=(1,),
        in_specs=([pl.BlockSpec((n, OFF_MERGE), lambda i: (0, 0), pipeline_mode=pl.Buffered(1))]
                  + [layer(s) for s in state_shapes]
                  + [whole(lw["wlru"].shape), whole(lw["wlora"].shape), whole(lw["seg"].shape),
                     whole(lw["v512"].shape), whole(lw["mu"].shape)]),
        out_specs=[whole(s) for s in out_shapes],
        out_shape=[jax.ShapeDtypeStruct(s, f32) for s in out_shapes],
        compiler_params=pltpu.CompilerParams(dimension_semantics=("arbitrary",),
                                             vmem_limit_bytes=VMEM_LIMIT_BYTES),
        name="sample_pre",
    )(proj, lconv_t, lh, rshift, cconv_t, lw["wlru"], lw["wlora"], lw["seg"], lw["v512"], lw["mu"])


def _sample_rwkv_call(s_t, vecs, l):
    _, nh, nv, nk, n = s_t.shape
    return pl.pallas_call(
        _sample_rwkv_kernel,
        grid=(nh,),
        in_specs=[pl.BlockSpec((None, None, nv, nk, n), lambda h: (l, h, 0, 0, 0)),
                  pl.BlockSpec((6, None, nk, n), lambda h: (0, h, 0, 0))],
        out_specs=[pl.BlockSpec((None, nv, nk, n), lambda h: (h, 0, 0, 0)),
                   pl.BlockSpec((None, nv, n), lambda h: (h, 0, 0))],
        out_shape=[jax.ShapeDtypeStruct((nh, nv, nk, n), f32), jax.ShapeDtypeStruct((nh, nv, n), f32)],
        compiler_params=pltpu.CompilerParams(dimension_semantics=("arbitrary",),
                                             vmem_limit_bytes=VMEM_LIMIT_BYTES),
        name="sample_rwkv",
    )(s_t, vecs)


def _sample_merge_kernel(x_ref, mod_ref, p_ref, ob_ref, aux_ref, ot_ref, seg_ref, v512_ref, wbr_ref, wout_ref,
                         v1024_ref, y_ref, *, alpha):
    def vec(row):
        return v512_ref[row:row + 1, :]

    o = jnp.concatenate([ot_ref[hd].T for hd in range(RWKV_HEADS)], axis=1)
    ob_rwkv = _rwkv_post(o, aux_ref[:, 0:BR_W], vec, seg_ref) * aux_ref[:, BR_W:2 * BR_W]
    branches = [ob_ref[:, 0:BR_W], ob_rwkv, ob_ref[:, BR_W:2 * BR_W], ob_ref[:, 2 * BR_W:3 * BR_W]]
    m = None
    for nb in range(N_BRANCH):
        off = OFF_MERGE + nb * D_MODEL
        term = _sigmoid(p_ref[:, off:off + D_MODEL]) * _bdot(branches[nb], wbr_ref[nb])
        m = term if m is None else m + term
    yv = _bdot(m, wout_ref[...]) + v1024_ref[M_BOUT:M_BOUT + 1, :]
    xn = alpha * x_ref[...] + mod_ref[:, 2 * D_MODEL:3 * D_MODEL] * yv
    y_ref[...] = _layernorm(xn, v1024_ref[M_LNG:M_LNG + 1, :], v1024_ref[M_LNB:M_LNB + 1, :])


def _sample_merge_call(x, mod, proj, ob, aux, ot, lw, big, l, alpha):
    n = x.shape[0]
    return pl.pallas_call(
        functools.partial(_sample_merge_kernel, alpha=alpha),
        grid=(1,),
        in_specs=[pl.BlockSpec((n, D_MODEL), lambda i: (0, 0)),
                  pl.BlockSpec((n, 3 * D_MODEL), lambda i: (0, 0)),
                  pl.BlockSpec((n, N_IN), lambda i: (0, 0)),
                  pl.BlockSpec(ob.shape, lambda i: (0, 0)),
                  pl.BlockSpec(aux.shape, lambda i: (0, 0)),
                  pl.BlockSpec(ot.shape, lambda i: (0, 0, 0)),
                  pl.BlockSpec(lw["seg"].shape, lambda i: (0, 0)),
                  pl.BlockSpec(lw["v512"].shape, lambda i: (0, 0)),
                  pl.BlockSpec((None,) + big["wbr"].shape[1:], lambda i: (l, 0, 0, 0)),
                  pl.BlockSpec((None,) + big["wout"].shape[1:], lambda i: (l, 0, 0)),
                  pl.BlockSpec(lw["v1024"].shape, lambda i: (0, 0))],
        out_specs=pl.BlockSpec((n, D_MODEL), lambda i: (0, 0)),
        out_shape=jax.ShapeDtypeStruct((n, D_MODEL), f32),
        compiler_params=pltpu.CompilerParams(dimension_semantics=("arbitrary",),
                                             vmem_limit_bytes=VMEM_LIMIT_BYTES),
        name="sample_merge",
    )(x, mod, proj, ob, aux, ot, lw["seg"], lw["v512"], big["wbr"], big["wout"], lw["v1024"])


def _pack_layer(l, w_in, lru_conv_w, lru_conv_b, lru_wr, lru_br, lru_wi, lru_bi, lru_lambda, rwkv_mu,
                rwkv_w0, rwkv_ww, rwkv_a0, rwkv_wa, rwkv_kk, rwkv_ka, rwkv_rk, rwkv_lnx_g, rwkv_lnx_b,
                gmlp_ln_g, gmlp_ln_b, gmlp_ws, gmlp_bs, conf_dw_w, conf_dw_b, conf_ln_g, conf_ln_b,
                w_branch, w_out, b_out, ln_g, ln_b):
    rep = BR_W // GMLP_GROUPS
    wr = block_diag(*[lru_wr[l, h] for h in range(LRU_HEADS)])
    wi = block_diag(*[lru_wi[l, h] for h in range(LRU_HEADS)])
    zero_lora = jnp.zeros((RWKV_RANK, BR_W), f32)
    wlora = jnp.concatenate([jnp.concatenate([rwkv_ww[l], zero_lora], axis=1),
                             jnp.concatenate([zero_lora, rwkv_wa[l]], axis=1)], axis=0)
    tril = jnp.tril(jnp.ones((GMLP_CHUNK, GMLP_CHUNK), dtype=bool))
    wm = jnp.where(tril[None], gmlp_ws[l], 0.0)
    wg = jnp.stack([jnp.concatenate([wm[2 * q], wm[2 * q + 1]], axis=1) for q in range(GMLP_GROUPS // 2)])
    rows = [lru_conv_w[l], lru_conv_b[l][None], lru_br[l][None], lru_bi[l][None], lru_lambda[l][None],
            rwkv_w0[l][None], rwkv_a0[l][None], rwkv_kk[l][None], rwkv_ka[l][None],
            rwkv_rk[l].reshape(1, BR_W), rwkv_lnx_g[l][None], rwkv_lnx_b[l][None],
            gmlp_ln_g[l][None], gmlp_ln_b[l][None], conf_dw_b[l][None], conf_ln_g[l][None],
            conf_ln_b[l][None], jnp.repeat(gmlp_ws[l, :, 0, 0], rep)[None],
            jnp.repeat(gmlp_bs[l, :, 0], rep)[None], jnp.zeros((V_CDW - V_GBS0 - 1, BR_W), f32),
            conf_dw_w[l], jnp.zeros((V_ROWS - V_CDW - CONF_K, BR_W), f32)]
    v1024 = jnp.concatenate([b_out[l][None], ln_g[l][None], ln_b[l][None],
                             jnp.zeros((SUBLANES - 3, D_MODEL), f32)], axis=0)
    nchunk = TIME_TILE // RWKV_CHUNK
    return dict(
        wlru=jnp.concatenate([wr, wi], axis=1).astype(bf16),
        wlora=wlora.astype(bf16),
        wg=wg.astype(bf16),
        bsx=jnp.repeat(gmlp_bs[l].T, rep, axis=1),
        seg=jnp.kron(jnp.eye(RWKV_HEADS, dtype=f32), jnp.ones((RWKV_HD, RWKV_HD), f32)).astype(bf16),
        ltri=jnp.kron(jnp.eye(nchunk, dtype=f32), jnp.tril(jnp.ones((RWKV_CHUNK, RWKV_CHUNK), f32))).astype(bf16),
        v512=jnp.concatenate(rows, axis=0),
        mu=rwkv_mu[l][None],
        v1024=v1024,
    )


def kernel(x_prompt, x_sample, state_lru_conv, state_lru_h, state_rwkv_shift, state_rwkv_S, state_conf_conv, c_prompt, c_sample, w_cond, b_cond, w_in, lru_conv_w, lru_conv_b, lru_wr, lru_br, lru_wi, lru_bi, lru_lambda, rwkv_mu, rwkv_w0, rwkv_ww, rwkv_a0, rwkv_wa, rwkv_kk, rwkv_ka, rwkv_rk, rwkv_lnx_g, rwkv_lnx_b, gmlp_ln_g, gmlp_ln_b, gmlp_ws, gmlp_bs, conf_dw_w, conf_dw_b, conf_ln_g, conf_ln_b, w_branch, w_out, b_out, ln_g, ln_b):
    depth = w_in.shape[0]
    alpha = (2.0 * depth) ** 0.25
    nb = x_prompt.shape[0]
    ns = x_sample.shape[0]

    mod = _cond_call(jnp.concatenate([c_prompt, c_sample], axis=0), w_cond.astype(bf16), b_cond)
    big = dict(win=w_in.astype(bf16), wbr=w_branch.astype(bf16), wout=w_out.astype(bf16))
    lconv_t = jnp.transpose(state_lru_conv, (0, 2, 1, 3))
    cconv_t = jnp.transpose(state_conf_conv, (0, 2, 1, 3))
    s_t = jnp.transpose(state_rwkv_S, (0, 2, 3, 4, 1))
    xp = x_prompt
    xs = x_sample.reshape(ns, D_MODEL)
    outs_p, outs_s = [], []
    for l in range(depth):
        lw = _pack_layer(l, w_in, lru_conv_w, lru_conv_b, lru_wr, lru_br, lru_wi, lru_bi, lru_lambda, rwkv_mu,
                         rwkv_w0, rwkv_ww, rwkv_a0, rwkv_wa, rwkv_kk, rwkv_ka, rwkv_rk, rwkv_lnx_g,
                         rwkv_lnx_b, gmlp_ln_g, gmlp_ln_b, gmlp_ws, gmlp_bs, conf_dw_w, conf_dw_b,
                         conf_ln_g, conf_ln_b, w_branch, w_out, b_out, ln_g, ln_b)
        mod_p = mod[l, :nb].reshape(nb, 3, D_MODEL)
        mod_s = mod[l, nb:]
        xp, lconv_p, lh_p, rshift_p, rs_p, cconv_p = _prompt_call(xp, mod_p, lw, big, l, alpha)
        outs_p.append((lconv_p, lh_p.reshape(nb, BR_W), rshift_p.reshape(nb, RWKV_SHIFT_W), rs_p, cconv_p))

        proj_s = _sample_proj_call(xs, mod_s, big["win"], l)
        ob, aux, vecs, lconv_s, lh_s, rshift_s, cconv_s, gv_s = _sample_pre_call(
            proj_s, lconv_t, state_lru_h, state_rwkv_shift, cconv_t, lw, l)
        rs_s, ot = _sample_rwkv_call(s_t, vecs, l)
        xs = _sample_merge_call(xs, mod_s, proj_s, ob, aux, ot, lw, big, l, alpha)
        outs_s.append((lconv_s, lh_s, rshift_s, rs_s, cconv_s, gv_s.reshape(ns, 1, BR_W)))

    def stk(outs, j):
        return jnp.stack([o[j] for o in outs])

    return (xp, xs.reshape(ns, 1, D_MODEL),
            stk(outs_p, 0), jnp.transpose(stk(outs_s, 0), (0, 2, 1, 3)),
            stk(outs_p, 1), stk(outs_s, 1),
            stk(outs_p, 2), stk(outs_s, 2),
            stk(outs_p, 3), jnp.transpose(stk(outs_s, 3), (0, 4, 1, 2, 3)),
            stk(outs_p, 4), jnp.transpose(stk(outs_s, 4), (0, 2, 1, 3)),
            stk(outs_s, 5))
```

```python
import functools

import jax
import jax.numpy as jnp
from jax import lax
from jax.experimental import pallas as pl
from jax.experimental.pallas import tpu as pltpu
from jax.scipy.linalg import block_diag

f32 = jnp.float32
bf16 = jnp.bfloat16

D_MODEL = 1024
N_BRANCH = 4
BR_W = D_MODEL // 2
LRU_HEADS = 8
LRU_CONV = 4
LRU_C = 8.0
RWKV_HD = 64
RWKV_HEADS = BR_W // RWKV_HD
RWKV_RANK = D_MODEL // 16
RWKV_SHIFT_W = 3 * BR_W + 2 * RWKV_RANK
RWKV_DECAY_SCALE = 0.606531
RWKV_LNX_EPS = 64e-5
GMLP_CHUNK = 128
GMLP_GROUPS = 8
CONF_K = 31
LN_EPS = 1e-5

OFF_LRU = 0
OFF_RWKV = OFF_LRU + 2 * BR_W
OFF_GMLP = OFF_RWKV + RWKV_SHIFT_W + BR_W
OFF_CONF = OFF_GMLP + 3 * BR_W
OFF_MERGE = OFF_CONF + 3 * BR_W
N_IN = OFF_MERGE + N_BRANCH * D_MODEL

LANES = 128
SUBLANES = 8
VMEM_LIMIT_BYTES = 60 * 1024 * 1024

TIME_TILE = 256
RWKV_CHUNK = 64
CHUNK_SHIFT = RWKV_CHUNK.bit_length() - 1
RWKV_GROUP = 16
SAMPLE_BLOCK = 8
SAMPLE_GROUP = 2
CONV_ROWS = 64
LRU_HALO = SUBLANES
CONF_HALO = 32

V_LCW, V_LCB, V_LBR, V_LBI, V_LAM = 0, 4, 5, 6, 7
V_W0, V_A0, V_KK, V_KA, V_RK, V_LNXG, V_LNXB = 8, 9, 10, 11, 12, 13, 14
V_GLG, V_GLB, V_CDB, V_CLG, V_CLB, V_GWS0, V_GBS0 = 15, 16, 17, 18, 19, 20, 21
V_CDW = 24
V_ROWS = 56
M_BOUT, M_LNG, M_LNB = 0, 1, 2


def _sigmoid(x):
    return 0.5 * jnp.tanh(0.5 * x) + 0.5


def _silu(x):
    return x * _sigmoid(x)


def _softplus(z):
    return jnp.maximum(z, 0.0) + jnp.log1p(jnp.exp(-jnp.abs(z)))


def _layernorm(x, g, b, eps=LN_EPS):
    mu = jnp.mean(x, axis=-1, keepdims=True)
    xc = x - mu
    var = jnp.mean(xc * xc, axis=-1, keepdims=True)
    return xc * lax.rsqrt(var + eps) * g + b


def _bdot(a, b):
    return jnp.dot(a.astype(bf16), b.astype(bf16), preferred_element_type=f32)


def _bdot_nt(a, b):
    return lax.dot_general(a.astype(bf16), b.astype(bf16), (((1,), (1,)), ((), ())),
                           preferred_element_type=f32)


def _bdot_tn(a, b):
    return lax.dot_general(a.astype(bf16), b.astype(bf16), (((0,), (0,)), ((), ())),
                           preferred_element_type=f32)


def _segsum(x, seg_ref):
    hi = x.astype(bf16)
    lo = (x - hi.astype(f32)).astype(bf16)
    seg = seg_ref[...]
    return (jnp.dot(hi, seg, preferred_element_type=f32)
            + jnp.dot(lo, seg, preferred_element_type=f32))


def _lru_gates(xc, rg, vec):
    r = _sigmoid(rg[:, :BR_W] + vec(V_LBR))
    ig = _sigmoid(rg[:, BR_W:] + vec(V_LBI))
    log_a = -LRU_C * r * _softplus(-vec(V_LAM))
    a = jnp.exp(log_a)
    mult = jnp.sqrt(-jnp.tanh(log_a) * (a * a + 1.0))
    return a, mult * (ig * xc)


def _rwkv_prep(xs, vec, wlora_ref, seg_ref):
    r = xs[:, 0:BR_W]
    k = xs[:, BR_W:2 * BR_W]
    v = xs[:, 2 * BR_W:3 * BR_W]
    dwa = xs[:, 3 * BR_W:3 * BR_W + 2 * RWKV_RANK]
    lane = lax.broadcasted_iota(jnp.int32, dwa.shape, 1)
    lora_in = jnp.where(lane < RWKV_RANK, jnp.tanh(dwa), dwa)
    wa = jnp.dot(lora_in.astype(bf16), wlora_ref[...], preferred_element_type=f32)
    logw = -RWKV_DECAY_SCALE * _sigmoid(vec(V_W0) + wa[:, :BR_W])
    a = _sigmoid(vec(V_A0) + wa[:, BR_W:])
    kk = k * vec(V_KK)
    kkn = kk / jnp.maximum(jnp.sqrt(_segsum(kk * kk, seg_ref)), 1e-12)
    k2 = k * (1.0 + (a - 1.0) * vec(V_KA))
    bonus = _segsum(r * k2 * vec(V_RK), seg_ref) * v
    return r, k2, v, logw, a, kkn, bonus


def _rwkv_post(o, bonus, vec, seg_ref):
    inv_n = 1.0 / RWKV_HD
    mean = _segsum(o, seg_ref) * inv_n
    oc = o - mean
    var = _segsum(oc * oc, seg_ref) * inv_n
    return oc * lax.rsqrt(var + RWKV_LNX_EPS) * vec(V_LNXG) + vec(V_LNXB) + bonus


def _scan_rows(a, u, h0):
    n = a.shape[0]
    in_group = lax.broadcasted_iota(jnp.int32, a.shape, 0) & (SUBLANES - 1)
    d = 1
    while d < SUBLANES:
        keep = in_group >= d
        a_s = jnp.where(keep, pltpu.roll(a, d, axis=0), 1.0)
        u_s = jnp.where(keep, pltpu.roll(u, d, axis=0), 0.0)
        u = u + a * u_s
        a = a * a_s
        d *= 2
    carry = h0
    groups = []
    for g in range(n // SUBLANES):
        rs = slice(g * SUBLANES, (g + 1) * SUBLANES)
        hg = u[rs] + a[rs] * carry
        groups.append(hg)
        carry = hg[SUBLANES - 1:SUBLANES]
    return jnp.concatenate(groups, axis=0)


def _cond_kernel(c_ref, w_ref, b_ref, o_ref):
    o_ref[0] = _bdot(_silu(c_ref[...]), w_ref[0]) + b_ref[0]


def _cond_call(c_all, w_cond, b_cond):
    depth = w_cond.shape[0]
    n = c_all.shape[0]
    return pl.pallas_call(
        _cond_kernel,
        grid=(depth,),
        in_specs=[pl.BlockSpec((n, D_MODEL), lambda l: (0, 0)),
                  pl.BlockSpec((1, D_MODEL, 3 * D_MODEL), lambda l: (l, 0, 0)),
                  pl.BlockSpec((1, 1, 3 * D_MODEL), lambda l: (l, 0, 0))],
        out_specs=pl.BlockSpec((1, n, 3 * D_MODEL), lambda l: (l, 0, 0)),
        out_shape=jax.ShapeDtypeStruct((depth, n, 3 * D_MODEL), f32),
        compiler_params=pltpu.CompilerParams(dimension_semantics=("arbitrary",),
                                             vmem_limit_bytes=VMEM_LIMIT_BYTES),
        name="cond",
    )(c_all, w_cond, b_cond.reshape(depth, 1, 3 * D_MODEL))


def _prompt_kernel(x_ref, mod_ref, win_ref, wlru_ref, wlora_ref, wg_ref, bsx_ref, seg_ref, ltri_ref,
                   wbr_ref, wout_ref, v512_ref, mu_ref, v1024_ref,
                   y_ref, lconv_ref, lh_ref, rshift_ref, rs_ref, cconv_ref,
                   h_ref, lext_ref, lhc_ref, rprev_ref, s_ref, cext_ref, ycv_ref, ob_ref, rw_ref,
                   ecc_ref, o_ref, rhat_ref, mp_ref, nn_ref, shc_ref, *, alpha):
    T = TIME_TILE
    C = RWKV_CHUNK
    npair = RWKV_HEADS // 2
    first = pl.program_id(1) == 0

    def vec(row):
        return v512_ref[row:row + 1, :]

    def carried(x):
        return jnp.where(first, jnp.zeros_like(x), x)

    lext_ref[0:LRU_HALO, :] = carried(lext_ref[0:LRU_HALO, :])
    cext_ref[0:CONF_HALO, :] = carried(cext_ref[0:CONF_HALO, :])
    for q in range(npair):
        s_ref[q] = carried(s_ref[q])

    shift = mod_ref[0, 0:1, :]
    scale = mod_ref[0, 1:2, :]
    gate = mod_ref[0, 2:3, :]
    h_ref[...] = (x_ref[0] * (1.0 + scale) + shift).astype(bf16)

    def proj(off, n):
        return jnp.dot(h_ref[...], win_ref[:, off:off + n], preferred_element_type=f32)

    pa = proj(OFF_LRU, 2 * BR_W)
    lext_ref[LRU_HALO:LRU_HALO + T, :] = pa[:, :BR_W]
    xc = vec(V_LCB)
    for j in range(LRU_CONV):
        xc = xc + vec(V_LCW + j) * lext_ref[pl.ds(LRU_HALO - (LRU_CONV - 1) + j, T), :]
    lext_ref[0:LRU_HALO, :] = lext_ref[T:T + LRU_HALO, :]
    rg = jnp.dot(xc.astype(bf16), wlru_ref[...], preferred_element_type=f32)
    pb = proj(OFF_RWKV, RWKV_SHIFT_W + BR_W)
    pd = proj(OFF_CONF, 3 * BR_W)
    pc = proj(OFF_GMLP, 3 * BR_W)

    a, u = _lru_gates(xc, rg, vec)
    hfull = _scan_rows(a, u, carried(lhc_ref[0:1, :]))
    lhc_ref[0:1, :] = hfull[T - 1:T, :]
    ob_ref[0] = (hfull * _silu(pa[:, BR_W:])).astype(bf16)

    p = pb[:, :RWKV_SHIFT_W]
    rows = lax.broadcasted_iota(jnp.int32, p.shape, 0)
    prev = jnp.where(rows == 0, carried(rprev_ref[0:1, :]), pltpu.roll(p, 1, axis=0))
    rprev_ref[0:1, :] = p[T - 1:T, :]
    xs = p + (prev - p) * mu_ref[...]
    r, k2, v, logw, ag, kkn, bonus = _rwkv_prep(xs, vec, wlora_ref, seg_ref)
    lw1 = logw.astype(bf16)
    rem = logw - lw1.astype(f32)
    lw2 = rem.astype(bf16)
    lw3 = (rem - lw2.astype(f32)).astype(bf16)
    ltri = ltri_ref[...]
    c = (jnp.dot(ltri, lw1, preferred_element_type=f32) + jnp.dot(ltri, lw2, preferred_element_type=f32)
         + jnp.dot(ltri, lw3, preferred_element_type=f32))
    cend = jnp.concatenate(
        [jnp.broadcast_to(c[(n + 1) * C - 1:(n + 1) * C, :], (C, BR_W)) for n in range(T // C)], axis=0)
    for n in range(T // C):
        ecc_ref[n * SUBLANES:(n + 1) * SUBLANES, :] = jnp.broadcast_to(
            jnp.exp(c[(n + 1) * C - 1:(n + 1) * C, :]), (SUBLANES, BR_W))
    e_neg = jnp.exp(-c)
    e_end = jnp.exp(cend - c)
    beta = kkn * ag
    rw_ref[0] = (-kkn * jnp.exp(c - logw)).astype(bf16)
    rw_ref[1] = (r * jnp.exp(c)).astype(bf16)
    rw_ref[2] = (beta * e_neg).astype(bf16)
    rw_ref[3] = (k2 * e_neg).astype(bf16)
    rw_ref[4] = v.astype(bf16)
    rw_ref[5] = (beta * e_end).astype(bf16)
    rw_ref[6] = (k2 * e_end).astype(bf16)

    R = lax.broadcasted_iota(jnp.int32, (2 * C, LANES), 0)
    Cc = lax.broadcasted_iota(jnp.int32, (2 * C, LANES), 1)
    t_idx = R & (C - 1)
    s_idx = Cc & (C - 1)
    lower = s_idx < t_idx
    mask_a = jnp.where(R < C, lower.astype(f32), (s_idx <= t_idx).astype(f32)).astype(bf16)
    block_mask = ((R >> CHUNK_SHIFT) == (Cc >> CHUNK_SHIFT)).astype(f32)
    eye = (R == Cc).astype(f32)
    lvl0_mask = (((R >> 1) == (Cc >> 1)) & lower).astype(f32)
    lvl_masks = []
    sh = 1
    while (1 << sh) < C:
        lvl_masks.append((((R >> (sh + 1)) == (Cc >> (sh + 1))) & ((R >> sh) != (Cc >> sh)) & lower)
                         .astype(f32).astype(bf16))
        sh += 1
    left64 = lax.broadcasted_iota(jnp.int32, (C, LANES), 1) < C
    left128 = Cc < C

    def sel_l(x, m):
        return jnp.where(m, x, jnp.zeros_like(x))

    def sel_r(x, m):
        return jnp.where(m, jnp.zeros_like(x), x)

    left256 = (lax.broadcasted_iota(jnp.int32, (C, 2 * LANES), 1) & (LANES - 1)) < C

    def split_lr(x, m):
        return jnp.concatenate([sel_l(x, m), sel_r(x, m)], axis=0)

    cext_ref[CONF_HALO:CONF_HALO + T, :] = pd[:, :BR_W] * _sigmoid(pd[:, BR_W:2 * BR_W])
    conf_gate = _silu(pd[:, 2 * BR_W:])

    vn = _layernorm(pc[:, BR_W:2 * BR_W], vec(V_GLG), vec(V_GLB))
    left_g = lax.broadcasted_iota(jnp.int32, (GMLP_CHUNK, LANES), 1) < (LANES // 2)
    z_rows = []
    for n in range(T // GMLP_CHUNK):
        z_cols = []
        for q in range(GMLP_GROUPS // 2):
            vq = vn[n * GMLP_CHUNK:(n + 1) * GMLP_CHUNK, q * LANES:(q + 1) * LANES].astype(bf16)
            rhs = jnp.concatenate([sel_l(vq, left_g), sel_r(vq, left_g)], axis=0)
            z_cols.append(jnp.dot(wg_ref[q], rhs, preferred_element_type=f32))
        z_rows.append(jnp.concatenate(z_cols, axis=1) + bsx_ref[...])
    z = jnp.concatenate(z_rows, axis=0)
    ob_ref[2] = (pc[:, :BR_W] * z * _silu(pc[:, 2 * BR_W:])).astype(bf16)

    items = [(n, q) for n in range(T // C) for q in range(npair)]
    for g0 in range(0, len(items), RWKV_GROUP):
        grp = items[g0:g0 + RWKV_GROUP]

        def ld(kind):
            return [rw_ref[kind, n * C:(n + 1) * C, q * LANES:(q + 1) * LANES] for n, q in grp]

        at, rt, bt, kt, vv, b_end, k_end = [ld(kind) for kind in range(7)]
        ar = [jnp.concatenate([a_, r_], axis=0) for a_, r_ in zip(at, rt)]
        out_e = [_bdot_nt(sel_l(x, left128), jnp.concatenate([b_, k_], axis=0)) for x, b_, k_ in zip(ar, bt, kt)]
        out_o = [_bdot_nt(sel_r(x, left128), jnp.concatenate([k_, b_], axis=0)) for x, b_, k_ in zip(ar, bt, kt)]
        tm = [eye + jnp.concatenate([sel_l(e_[0:C], left64), sel_r(o_[0:C], left64)], axis=0) * lvl0_mask
              for e_, o_ in zip(out_e, out_o)]
        out_e = [x.astype(bf16) * mask_a for x in out_e]
        out_o = [x.astype(bf16) * mask_a for x in out_o]
        a_pair = [jnp.concatenate([sel_l(e_[0:C], left64), sel_r(o_[0:C], left64)], axis=0)
                  for e_, o_ in zip(out_e, out_o)]
        lhs_ak = [jnp.where(left64, o_[0:C], e_[0:C]) for e_, o_ in zip(out_e, out_o)]
        lhs_rb = [jnp.where(left64, e_[C:2 * C], o_[C:2 * C]) for e_, o_ in zip(out_e, out_o)]
        lhs_rk = [jnp.where(left64, o_[C:2 * C], e_[C:2 * C]) for e_, o_ in zip(out_e, out_o)]
        v_rl = [jnp.concatenate([sel_r(x, left64), sel_l(x, left64)], axis=0) for x in vv]
        av = [jnp.dot(l_, x, preferred_element_type=f32).astype(bf16) for l_, x in zip(lhs_ak, v_rl)]
        ark = [jnp.dot(l_, x, preferred_element_type=f32) for l_, x in zip(lhs_rk, v_rl)]
        for lm in lvl_masks:
            w = [_bdot(x * lm, t_) for x, t_ in zip(a_pair, tm)]
            tm = [t_ + _bdot(t_, w_) for t_, w_ in zip(tm, w)]
        t_row = [(t_[0:C] + t_[C:2 * C]).astype(bf16) for t_ in tm]
        tx = [jnp.dot(t_, split_lr(jnp.concatenate([a_, x], axis=1), left256),
                      preferred_element_type=f32).astype(bf16)
              for t_, a_, x in zip(t_row, at, av)]
        arb = [jnp.dot(l_, split_lr(x, left256), preferred_element_type=f32) for l_, x in zip(lhs_rb, tx)]
        mp = [_bdot_tn(x[:, 0:LANES], b_) * block_mask for x, b_ in zip(tx, b_end)]
        nn = [_bdot_tn(jnp.concatenate([x[:, LANES:2 * LANES], v_], axis=0),
                       jnp.concatenate([b_, k_], axis=0)) * block_mask
              for x, v_, b_, k_ in zip(tx, vv, b_end, k_end)]
        for idx, (n, q) in enumerate(grp):
            rs = slice(n * C, (n + 1) * C)
            ls = slice(q * LANES, (q + 1) * LANES)
            rhat_ref[rs, ls] = (rt[idx].astype(f32) + arb[idx][:, 0:LANES]).astype(bf16)
            o_ref[rs, ls] = arb[idx][:, LANES:2 * LANES] + ark[idx]
            mp_ref[n * npair + q] = mp[idx].astype(bf16)
            nn_ref[n * npair + q] = nn[idx]

    for n in range(T // C):
        rs = slice(n * C, (n + 1) * C)
        sp = [s_ref[q] for q in range(npair)]
        sb = [x.astype(bf16) for x in sp]
        o_blk = [_bdot_nt(rhat_ref[rs, q * LANES:(q + 1) * LANES], sb[q]) for q in range(npair)]
        s_new = [jnp.dot(sb[q], mp_ref[n * npair + q], preferred_element_type=f32) for q in range(npair)]
        for q in range(npair):
            ls = slice(q * LANES, (q + 1) * LANES)
            o_ref[rs, ls] = o_ref[rs, ls] + o_blk[q]
            s_ref[q] = sp[q] * ecc_ref[n * SUBLANES:n * SUBLANES + 1, ls] + s_new[q] + nn_ref[n * npair + q]

    ob_ref[1] = (_rwkv_post(o_ref[...], bonus, vec, seg_ref) * _silu(pb[:, RWKV_SHIFT_W:])).astype(bf16)

    def merge_term(nb):
        return _sigmoid(proj(OFF_MERGE + nb * D_MODEL, D_MODEL)) * jnp.dot(
            ob_ref[nb], wbr_ref[nb], preferred_element_type=f32)

    m = merge_term(0) + merge_term(1) + merge_term(2)

    base = CONF_HALO - (CONF_K - 1)
    shift_rows = T + CONF_HALO - SUBLANES
    for lb in range(BR_W // LANES):
        ls = slice(lb * LANES, (lb + 1) * LANES)
        for s in range(1, SUBLANES):
            shc_ref[s, 0:shift_rows, :] = cext_ref[pl.ds(s, shift_rows), ls]
        for rb in range(T // CONV_ROWS):
            acc = jnp.broadcast_to(v512_ref[V_CDB:V_CDB + 1, ls], (CONV_ROWS, LANES))
            for j in range(CONF_K):
                q, s = divmod(base + j, SUBLANES)
                row0 = rb * CONV_ROWS + q * SUBLANES
                if s == 0:
                    tap_in = cext_ref[row0:row0 + CONV_ROWS, ls]
                else:
                    tap_in = shc_ref[s, row0:row0 + CONV_ROWS, :]
                acc = acc + v512_ref[V_CDW + j:V_CDW + j + 1, ls] * tap_in
            ycv_ref[rb * CONV_ROWS:(rb + 1) * CONV_ROWS, ls] = acc
    cext_ref[0:CONF_HALO, :] = cext_ref[T:T + CONF_HALO, :]
    ob_ref[3] = (_silu(_layernorm(ycv_ref[...], vec(V_CLG), vec(V_CLB))) * conf_gate).astype(bf16)

    m = m + merge_term(3)
    yv = jnp.dot(m.astype(bf16), wout_ref[...], preferred_element_type=f32) + v1024_ref[M_BOUT:M_BOUT + 1, :]
    xn = alpha * x_ref[0] + gate * yv
    y_ref[0] = _layernorm(xn, v1024_ref[M_LNG:M_LNG + 1, :], v1024_ref[M_LNB:M_LNB + 1, :])

    lconv_ref[0] = lext_ref[LRU_HALO - (LRU_CONV - 1):LRU_HALO, :]
    lh_ref[0] = lhc_ref[0:1, :]
    rshift_ref[0] = rprev_ref[0:1, :]
    cconv_ref[0] = cext_ref[CONF_HALO - (CONF_K - 1):CONF_HALO, :]
    for q in range(npair):
        sp = s_ref[q]
        rs_ref[0, 2 * q] = sp[0:C, 0:C]
        rs_ref[0, 2 * q + 1] = sp[C:2 * C, C:2 * C]


def _resident(shape):
    nd = len(shape)
    return pl.BlockSpec(shape, lambda b, i: (0,) * nd, pipeline_mode=pl.Buffered(1))


def _resident_layer(shape, l):
    nd = len(shape) - 1
    return pl.BlockSpec((None,) + tuple(shape[1:]), lambda b, i: (l,) + (0,) * nd,
                        pipeline_mode=pl.Buffered(1))


def _prompt_call(x, mod3, lw, big, l, alpha):
    nb, seq, _ = x.shape
    T = TIME_TILE
    assert seq % T == 0 and T % GMLP_CHUNK == 0 and T % RWKV_CHUNK == 0
    nt = seq // T
    in_specs = [
        pl.BlockSpec((1, T, D_MODEL), lambda b, i: (b, i, 0)),
        pl.BlockSpec((1, 3, D_MODEL), lambda b, i: (b, 0, 0)),
        _resident_layer(big["win"].shape, l), _resident(lw["wlru"].shape), _resident(lw["wlora"].shape),
        _resident(lw["wg"].shape), _resident(lw["bsx"].shape), _resident(lw["seg"].shape),
        _resident(lw["ltri"].shape), _resident_layer(big["wbr"].shape, l),
        _resident_layer(big["wout"].shape, l),
        _resident(lw["v512"].shape), _resident(lw["mu"].shape), _resident(lw["v1024"].shape),
    ]
    out_shape = (
        jax.ShapeDtypeStruct((nb, seq, D_MODEL), f32),
        jax.ShapeDtypeStruct((nb, LRU_CONV - 1, BR_W), f32),
        jax.ShapeDtypeStruct((nb, 1, BR_W), f32),
        jax.ShapeDtypeStruct((nb, 1, RWKV_SHIFT_W), f32),
        jax.ShapeDtypeStruct((nb, RWKV_HEADS, RWKV_HD, RWKV_HD), f32),
        jax.ShapeDtypeStruct((nb, CONF_K - 1, BR_W), f32),
    )
    out_specs = (
        pl.BlockSpec((1, T, D_MODEL), lambda b, i: (b, i, 0)),
        pl.BlockSpec((1, LRU_CONV - 1, BR_W), lambda b, i: (b, 0, 0)),
        pl.BlockSpec((1, 1, BR_W), lambda b, i: (b, 0, 0)),
        pl.BlockSpec((1, 1, RWKV_SHIFT_W), lambda b, i: (b, 0, 0)),
        pl.BlockSpec((1, RWKV_HEADS, RWKV_HD, RWKV_HD), lambda b, i: (b, 0, 0, 0)),
        pl.BlockSpec((1, CONF_K - 1, BR_W), lambda b, i: (b, 0, 0)),
    )
    scratch = [
        pltpu.VMEM((T, D_MODEL), bf16),
        pltpu.VMEM((T + LRU_HALO, BR_W), f32),
        pltpu.VMEM((SUBLANES, BR_W), f32),
        pltpu.VMEM((SUBLANES, RWKV_SHIFT_W), f32),
        pltpu.VMEM((RWKV_HEADS // 2, 2 * RWKV_HD, 2 * RWKV_HD), f32),
        pltpu.VMEM((T + CONF_HALO, BR_W), f32),
        pltpu.VMEM((T, BR_W), f32),
        pltpu.VMEM((N_BRANCH, T, BR_W), bf16),
        pltpu.VMEM((7, T, BR_W), bf16),
        pltpu.VMEM((T // RWKV_CHUNK * SUBLANES, BR_W), f32),
        pltpu.VMEM((T, BR_W), f32),
        pltpu.VMEM((T, BR_W), bf16),
        pltpu.VMEM((T // RWKV_CHUNK * (RWKV_HEADS // 2), 2 * RWKV_HD, 2 * RWKV_HD), bf16),
        pltpu.VMEM((T // RWKV_CHUNK * (RWKV_HEADS // 2), 2 * RWKV_HD, 2 * RWKV_HD), f32),
        pltpu.VMEM((SUBLANES, T + CONF_HALO, LANES), f32),
    ]
    return pl.pallas_call(
        functools.partial(_prompt_kernel, alpha=alpha),
        grid=(nb, nt),
        in_specs=in_specs,
        out_specs=out_specs,
        out_shape=out_shape,
        scratch_shapes=scratch,
        compiler_params=pltpu.CompilerParams(dimension_semantics=("arbitrary", "arbitrary"),
                                             vmem_limit_bytes=VMEM_LIMIT_BYTES),
        name="prompt_layer",
    )(x, mod3, big["win"], lw["wlru"], lw["wlora"], lw["wg"], lw["bsx"], lw["seg"], lw["ltri"],
      big["wbr"], big["wout"], lw["v512"], lw["mu"], lw["v1024"])


def _sample_proj_kernel(x_ref, mod_ref, w_ref, o_ref):
    shift = mod_ref[:, 0:D_MODEL]
    scale = mod_ref[:, D_MODEL:2 * D_MODEL]
    h = (x_ref[...] * (1.0 + scale) + shift).astype(bf16)
    o_ref[...] = jnp.dot(h, w_ref[...], preferred_element_type=f32)


SAMPLE_PROJ_COLS = 1152


def _sample_proj_call(x, mod, win, l):
    n = x.shape[0]
    assert N_IN % SAMPLE_PROJ_COLS == 0
    return pl.pallas_call(
        _sample_proj_kernel,
        grid=(N_IN // SAMPLE_PROJ_COLS,),
        in_specs=[pl.BlockSpec((n, D_MODEL), lambda j: (0, 0)),
                  pl.BlockSpec((n, 3 * D_MODEL), lambda j: (0, 0)),
                  pl.BlockSpec((None, D_MODEL, SAMPLE_PROJ_COLS), lambda j: (l, 0, j))],
        out_specs=pl.BlockSpec((n, SAMPLE_PROJ_COLS), lambda j: (0, j)),
        out_shape=jax.ShapeDtypeStruct((n, N_IN), f32),
        compiler_params=pltpu.CompilerParams(dimension_semantics=("arbitrary",),
                                             vmem_limit_bytes=VMEM_LIMIT_BYTES),
        name="sample_proj",
    )(x, mod, win)


def _sample_pre_kernel(p_ref, lconv_ref, lh_ref, rshift_ref, cconv_ref,
                       wlru_ref, wlora_ref, seg_ref, v512_ref, mu_ref,
                       ob_ref, aux_ref, vecs_ref, lconv_o, lh_o, rshift_o, cconv_o, gv_o):
    def vec(row):
        return v512_ref[row:row + 1, :]

    xb = p_ref[:, OFF_LRU:OFF_LRU + BR_W]
    xc = vec(V_LCB) + vec(V_LCW + LRU_CONV - 1) * xb
    for j in range(LRU_CONV - 1):
        xc = xc + vec(V_LCW + j) * lconv_ref[j]
    for j in range(LRU_CONV - 2):
        lconv_o[j] = lconv_ref[j + 1]
    lconv_o[LRU_CONV - 2] = xb
    a, u = _lru_gates(xc, jnp.dot(xc.astype(bf16), wlru_ref[...], preferred_element_type=f32), vec)
    hn = a * lh_ref[...] + u
    lh_o[...] = hn
    ob_ref[:, 0:BR_W] = hn * _silu(p_ref[:, OFF_LRU + BR_W:OFF_LRU + 2 * BR_W])

    p = p_ref[:, OFF_RWKV:OFF_RWKV + RWKV_SHIFT_W]
    xs = p + (rshift_ref[...] - p) * mu_ref[...]
    rshift_o[...] = p
    r, k2, v, logw, ag, kkn, bonus = _rwkv_prep(xs, vec, wlora_ref, seg_ref)
    n = RWKV_HD
    for idx, x in enumerate((-kkn, kkn * ag, k2, r, jnp.exp(logw), v)):
        for hd in range(RWKV_HEADS):
            vecs_ref[idx, hd] = x[:, hd * n:(hd + 1) * n].T
    aux_ref[:, 0:BR_W] = bonus
    aux_ref[:, BR_W:2 * BR_W] = _silu(p_ref[:, OFF_RWKV + RWKV_SHIFT_W:OFF_RWKV + RWKV_SHIFT_W + BR_W])

    vn = _layernorm(p_ref[:, OFF_GMLP + BR_W:OFF_GMLP + 2 * BR_W], vec(V_GLG), vec(V_GLB))
    gv_o[...] = vn
    z = vec(V_GWS0) * vn + vec(V_GBS0)
    ob_ref[:, BR_W:2 * BR_W] = (p_ref[:, OFF_GMLP:OFF_GMLP + BR_W] * z
                                * _silu(p_ref[:, OFF_GMLP + 2 * BR_W:OFF_GMLP + 3 * BR_W]))

    glu = p_ref[:, OFF_CONF:OFF_CONF + BR_W] * _sigmoid(p_ref[:, OFF_CONF + BR_W:OFF_CONF + 2 * BR_W])
    y = vec(V_CDB) + vec(V_CDW + CONF_K - 1) * glu
    for j in range(CONF_K - 1):
        y = y + vec(V_CDW + j) * cconv_ref[j]
    for j in range(CONF_K - 2):
        cconv_o[j] = cconv_ref[j + 1]
    cconv_o[CONF_K - 2] = glu
    ob_ref[:, 2 * BR_W:3 * BR_W] = (_silu(_layernorm(y, vec(V_CLG), vec(V_CLB)))
                                    * _silu(p_ref[:, OFF_CONF + 2 * BR_W:OFF_CONF + 3 * BR_W]))


def _sample_rwkv_kernel(s_ref, vecs_ref, s_o, ot_o):
    neg_kk = vecs_ref[0]
    beta = vecs_ref[1]
    k2 = vecs_ref[2]
    r = vecs_ref[3]
    w = vecs_ref[4]
    o_rows = []
    for vi in range(RWKV_HD):
        s0 = s_ref[vi]
        sa = jnp.sum(s0 * neg_kk, axis=0, keepdims=True)
        sn = s0 * w + sa * beta + vecs_ref[5, vi:vi + 1, :] * k2
        s_o[vi] = sn
        o_rows.append(jnp.sum(sn * r, axis=0, keepdims=True))
    ot_o[...] = jnp.concatenate(o_rows, axis=0)


def _sample_pre_call(proj, lconv_t, lh, rshift, cconv_t, lw, l):
    n = proj.shape[0]

    def whole(shape):
        nd = len(shape)
        return pl.BlockSpec(shape, lambda i: (0,) * nd, pipeline_mode=pl.Buffered(1))

    def layer(shape):
        nd = len(shape)
        return pl.BlockSpec((None,) + shape, lambda i: (l,) + (0,) * nd, pipeline_mode=pl.Buffered(1))

    state_shapes = [(LRU_CONV - 1, n, BR_W), (n, BR_W), (n, RWKV_SHIFT_W), (CONF_K - 1, n, BR_W)]
    out_shapes = ([(n, (N_BRANCH - 1) * BR_W), (n, 2 * BR_W), (6, RWKV_HEADS, RWKV_HD, n)]
                  + state_shapes + [(n, BR_W)])
    return pl.pallas_call(
        _sample_pre_kernel,
        grid=(1,),
        in_specs=([pl.BlockSpec((n, OFF_MERGE), lambda i: (0, 0), pipeline_mode=pl.Buffered(1))]
                  + [layer(s) for s in state_shapes]
                  + [whole(lw["wlru"].shape), whole(lw["wlora"].shape), whole(lw["seg"].shape),
                     whole(lw["v512"].shape), whole(lw["mu"].shape)]),
        out_specs=[whole(s) for s in out_shapes],
        out_shape=[jax.ShapeDtypeStruct(s, f32) for s in out_shapes],
        compiler_params=pltpu.CompilerParams(dimension_semantics=("arbitrary",),
                                             vmem_limit_bytes=VMEM_LIMIT_BYTES),
        name="sample_pre",
    )(proj, lconv_t, lh, rshift, cconv_t, lw["wlru"], lw["wlora"], lw["seg"], lw["v512"], lw["mu"])


def _sample_rwkv_call(s_t, vecs, l):
    _, nh, nv, nk, n = s_t.shape
    return pl.pallas_call(
        _sample_rwkv_kernel,
        grid=(nh,),
        in_specs=[pl.BlockSpec((None, None, nv, nk, n), lambda h: (l, h, 0, 0, 0)),
                  pl.BlockSpec((6, None, nk, n), lambda h: (0, h, 0, 0))],
        out_specs=[pl.BlockSpec((None, nv, nk, n), lambda h: (h, 0, 0, 0)),
                   pl.BlockSpec((None, nv, n), lambda h: (h, 0, 0))],
        out_shape=[jax.ShapeDtypeStruct((nh, nv, nk, n), f32), jax.ShapeDtypeStruct((nh, nv, n), f32)],
        compiler_params=pltpu.CompilerParams(dimension_semantics=("arbitrary",),
                                             vmem_limit_bytes=VMEM_LIMIT_BYTES),
        name="sample_rwkv",
    )(s_t, vecs)


def _sample_merge_kernel(x_ref, mod_ref, p_ref, ob_ref, aux_ref, ot_ref, seg_ref, v512_ref, wbr_ref, wout_ref,
                         v1024_ref, y_ref, *, alpha):
    def vec(row):
        return v512_ref[row:row + 1, :]

    o = jnp.concatenate([ot_ref[hd].T for hd in range(RWKV_HEADS)], axis=1)
    ob_rwkv = _rwkv_post(o, aux_ref[:, 0:BR_W], vec, seg_ref) * aux_ref[:, BR_W:2 * BR_W]
    branches = [ob_ref[:, 0:BR_W], ob_rwkv, ob_ref[:, BR_W:2 * BR_W], ob_ref[:, 2 * BR_W:3 * BR_W]]
    m = None
    for nb in range(N_BRANCH):
        off = OFF_MERGE + nb * D_MODEL
        term = _sigmoid(p_ref[:, off:off + D_MODEL]) * _bdot(branches[nb], wbr_ref[nb])
        m = term if m is None else m + term
    yv = _bdot(m, wout_ref[...]) + v1024_ref[M_BOUT:M_BOUT + 1, :]
    xn = alpha * x_ref[...] + mod_ref[:, 2 * D_MODEL:3 * D_MODEL] * yv
    y_ref[...] = _layernorm(xn, v1024_ref[M_LNG:M_LNG + 1, :], v1024_ref[M_LNB:M_LNB + 1, :])


def _sample_merge_call(x, mod, proj, ob, aux, ot, lw, big, l, alpha):
    n = x.shape[0]
    return pl.pallas_call(
        functools.partial(_sample_merge_kernel, alpha=alpha),
        grid=(1,),
        in_specs=[pl.BlockSpec((n, D_MODEL), lambda i: (0, 0)),
                  pl.BlockSpec((n, 3 * D_MODEL), lambda i: (0, 0)),
                  pl.BlockSpec((n, N_IN), lambda i: (0, 0)),
                  pl.BlockSpec(ob.shape, lambda i: (0, 0)),
                  pl.BlockSpec(aux.shape, lambda i: (0, 0)),
                  pl.BlockSpec(ot.shape, lambda i: (0, 0, 0)),
                  pl.BlockSpec(lw["seg"].shape, lambda i: (0, 0)),
                  pl.BlockSpec(lw["v512"].shape, lambda i: (0, 0)),
                  pl.BlockSpec((None,) + big["wbr"].shape[1:], lambda i: (l, 0, 0, 0)),
                  pl.BlockSpec((None,) + big["wout"].shape[1:], lambda i: (l, 0, 0)),
                  pl.BlockSpec(lw["v1024"].shape, lambda i: (0, 0))],
        out_specs=pl.BlockSpec((n, D_MODEL), lambda i: (0, 0)),
        out_shape=jax.ShapeDtypeStruct((n, D_MODEL), f32),
        compiler_params=pltpu.CompilerParams(dimension_semantics=("arbitrary",),
                                             vmem_limit_bytes=VMEM_LIMIT_BYTES),
        name="sample_merge",
    )(x, mod, proj, ob, aux, ot, lw["seg"], lw["v512"], big["wbr"], big["wout"], lw["v1024"])


def _pack_layer(l, w_in, lru_conv_w, lru_conv_b, lru_wr, lru_br, lru_wi, lru_bi, lru_lambda, rwkv_mu,
                rwkv_w0, rwkv_ww, rwkv_a0, rwkv_wa, rwkv_kk, rwkv_ka, rwkv_rk, rwkv_lnx_g, rwkv_lnx_b,
                gmlp_ln_g, gmlp_ln_b, gmlp_ws, gmlp_bs, conf_dw_w, conf_dw_b, conf_ln_g, conf_ln_b,
                w_branch, w_out, b_out, ln_g, ln_b):
    rep = BR_W // GMLP_GROUPS
    wr = block_diag(*[lru_wr[l, h] for h in range(LRU_HEADS)])
    wi = block_diag(*[lru_wi[l, h] for h in range(LRU_HEADS)])
    zero_lora = jnp.zeros((RWKV_RANK, BR_W), f32)
    wlora = jnp.concatenate([jnp.concatenate([rwkv_ww[l], zero_lora], axis=1),
                             jnp.concatenate([zero_lora, rwkv_wa[l]], axis=1)], axis=0)
    tril = jnp.tril(jnp.ones((GMLP_CHUNK, GMLP_CHUNK), dtype=bool))
    wm = jnp.where(tril[None], gmlp_ws[l], 0.0)
    wg = jnp.stack([jnp.concatenate([wm[2 * q], wm[2 * q + 1]], axis=1) for q in range(GMLP_GROUPS // 2)])
    rows = [lru_conv_w[l], lru_conv_b[l][None], lru_br[l][None], lru_bi[l][None], lru_lambda[l][None],
            rwkv_w0[l][None], rwkv_a0[l][None], rwkv_kk[l][None], rwkv_ka[l][None],
            rwkv_rk[l].reshape(1, BR_W), rwkv_lnx_g[l][None], rwkv_lnx_b[l][None],
            gmlp_ln_g[l][None], gmlp_ln_b[l][None], conf_dw_b[l][None], conf_ln_g[l][None],
            conf_ln_b[l][None], jnp.repeat(gmlp_ws[l, :, 0, 0], rep)[None],
            jnp.repeat(gmlp_bs[l, :, 0], rep)[None], jnp.zeros((V_CDW - V_GBS0 - 1, BR_W), f32),
            conf_dw_w[l], jnp.zeros((V_ROWS - V_CDW - CONF_K, BR_W), f32)]
    v1024 = jnp.concatenate([b_out[l][None], ln_g[l][None], ln_b[l][None],
                             jnp.zeros((SUBLANES - 3, D_MODEL), f32)], axis=0)
    nchunk = TIME_TILE // RWKV_CHUNK
    return dict(
        wlru=jnp.concatenate([wr, wi], axis=1).astype(bf16),
        wlora=wlora.astype(bf16),
        wg=wg.astype(bf16),
        bsx=jnp.repeat(gmlp_bs[l].T, rep, axis=1),
        seg=jnp.kron(jnp.eye(RWKV_HEADS, dtype=f32), jnp.ones((RWKV_HD, RWKV_HD), f32)).astype(bf16),
        ltri=jnp.kron(jnp.eye(nchunk, dtype=f32), jnp.tril(jnp.ones((RWKV_CHUNK, RWKV_CHUNK), f32))).astype(bf16),
        v512=jnp.concatenate(rows, axis=0),
        mu=rwkv_mu[l][None],
        v1024=v1024,
    )


def kernel(x_prompt, x_sample, state_lru_conv, state_lru_h, state_rwkv_shift, state_rwkv_S, state_conf_conv, c_prompt, c_sample, w_cond, b_cond, w_in, lru_conv_w, lru_conv_b, lru_wr, lru_br, lru_wi, lru_bi, lru_lambda, rwkv_mu, rwkv_w0, rwkv_ww, rwkv_a0, rwkv_wa, rwkv_kk, rwkv_ka, rwkv_rk, rwkv_lnx_g, rwkv_lnx_b, gmlp_ln_g, gmlp_ln_b, gmlp_ws, gmlp_bs, conf_dw_w, conf_dw_b, conf_ln_g, conf_ln_b, w_branch, w_out, b_out, ln_g, ln_b):
    depth = w_in.shape[0]
    alpha = (2.0 * depth) ** 0.25
    nb = x_prompt.shape[0]
    ns = x_sample.shape[0]

    mod = _cond_call(jnp.concatenate([c_prompt, c_sample], axis=0), w_cond.astype(bf16), b_cond)
    big = dict(win=w_in.astype(bf16), wbr=w_branch.astype(bf16), wout=w_out.astype(bf16))
    lconv_t = jnp.transpose(state_lru_conv, (0, 2, 1, 3))
    cconv_t = jnp.transpose(state_conf_conv, (0, 2, 1, 3))
    s_t = jnp.transpose(state_rwkv_S, (0, 2, 3, 4, 1))
    xp = x_prompt
    xs = x_sample.reshape(ns, D_MODEL)
    outs_p, outs_s = [], []
    for l in range(depth):
        lw = _pack_layer(l, w_in, lru_conv_w, lru_conv_b, lru_wr, lru_br, lru_wi, lru_bi, lru_lambda, rwkv_mu,
                         rwkv_w0, rwkv_ww, rwkv_a0, rwkv_wa, rwkv_kk, rwkv_ka, rwkv_rk, rwkv_lnx_g,
                         rwkv_lnx_b, gmlp_ln_g, gmlp_ln_b, gmlp_ws, gmlp_bs, conf_dw_w, conf_dw_b,
                         conf_ln_g, conf_ln_b, w_branch, w_out, b_out, ln_g, ln_b)
        mod_p = mod[l, :nb].reshape(nb, 3, D_MODEL)
        mod_s = mod[l, nb:]
        xp, lconv_p, lh_p, rshift_p, rs_p, cconv_p = _prompt_call(xp, mod_p, lw, big, l, alpha)
        outs_p.append((lconv_p, lh_p.reshape(nb, BR_W), rshift_p.reshape(nb, RWKV_SHIFT_W), rs_p, cconv_p))

        proj_s = _sample_proj_call(xs, mod_s, big["win"], l)
        ob, aux, vecs, lconv_s, lh_s, rshift_s, cconv_s, gv_s = _sample_pre_call(
            proj_s, lconv_t, state_lru_h, state_rwkv_shift, cconv_t, lw, l)
        rs_s, ot = _sample_rwkv_call(s_t, vecs, l)
        xs = _sample_merge_call(xs, mod_s, proj_s, ob, aux, ot, lw, big, l, alpha)
        outs_s.append((lconv_s, lh_s, rshift_s, rs_s, cconv_s, gv_s.reshape(ns, 1, BR_W)))

    def stk(outs, j):
        return jnp.stack([o[j] for o in outs])

    return (xp, xs.reshape(ns, 1, D_MODEL),
            stk(outs_p, 0), jnp.transpose(stk(outs_s, 0), (0, 2, 1, 3)),
            stk(outs_p, 1), stk(outs_s, 1),
            stk(outs_p, 2), stk(outs_s, 2),
            stk(outs_p, 3), jnp.transpose(stk(outs_s, 3), (0, 4, 1, 2, 3)),
            stk(outs_p, 4), jnp.transpose(stk(outs_s, 4), (0, 2, 1, 3)),
            stk(outs_s, 5))
```

```python
import functools

import jax
import jax.numpy as jnp
from jax import lax
from jax.experimental import pallas as pl
from jax.experimental.pallas import tpu as pltpu
from jax.scipy.linalg import block_diag

f32 = jnp.float32
bf16 = jnp.bfloat16

D_MODEL = 1024
N_BRANCH = 4
BR_W = D_MODEL // 2
LRU_HEADS = 8
LRU_CONV = 4
LRU_C = 8.0
RWKV_HD = 64
RWKV_HEADS = BR_W // RWKV_HD
RWKV_RANK = D_MODEL // 16
RWKV_SHIFT_W = 3 * BR_W + 2 * RWKV_RANK
RWKV_DECAY_SCALE = 0.606531
RWKV_LNX_EPS = 64e-5
GMLP_CHUNK = 128
GMLP_GROUPS = 8
CONF_K = 31
LN_EPS = 1e-5

OFF_LRU = 0
OFF_RWKV = OFF_LRU + 2 * BR_W
OFF_GMLP = OFF_RWKV + RWKV_SHIFT_W + BR_W
OFF_CONF = OFF_GMLP + 3 * BR_W
OFF_MERGE = OFF_CONF + 3 * BR_W
N_IN = OFF_MERGE + N_BRANCH * D_MODEL

LANES = 128
SUBLANES = 8
VMEM_LIMIT_BYTES = 60 * 1024 * 1024

TIME_TILE = 256
RWKV_CHUNK = 64
CHUNK_SHIFT = RWKV_CHUNK.bit_length() - 1
RWKV_GROUP = 16
SAMPLE_BLOCK = 8
SAMPLE_GROUP = 2
CONV_ROWS = 64
PROJ_CHUNK = 256
PROMPT_FLAGS = None
LRU_HALO = SUBLANES
CONF_HALO = 32

V_LCW, V_LCB, V_LBR, V_LBI, V_LAM = 0, 4, 5, 6, 7
V_W0, V_A0, V_KK, V_KA, V_RK, V_LNXG, V_LNXB = 8, 9, 10, 11, 12, 13, 14
V_GLG, V_GLB, V_CDB, V_CLG, V_CLB, V_GWS0, V_GBS0 = 15, 16, 17, 18, 19, 20, 21
V_CDW = 24
V_ROWS = 56
M_BOUT, M_LNG, M_LNB = 0, 1, 2


def _sigmoid(x):
    return 0.5 * jnp.tanh(0.5 * x) + 0.5


def _silu(x):
    hx = 0.5 * x
    return hx * jnp.tanh(hx) + hx


def _merge_gate_times(p_half, y_half):
    return (jnp.tanh(p_half) + 1.0) * y_half


def _softplus(z):
    return jnp.maximum(z, 0.0) + jnp.log1p(jnp.exp(-jnp.abs(z)))


def _layernorm(x, g, b, eps=LN_EPS):
    mu = jnp.mean(x, axis=-1, keepdims=True)
    xc = x - mu
    var = jnp.mean(xc * xc, axis=-1, keepdims=True)
    return xc * lax.rsqrt(var + eps) * g + b


def _bdot(a, b):
    return jnp.dot(a.astype(bf16), b.astype(bf16), preferred_element_type=f32)


def _bdot_nt(a, b):
    return lax.dot_general(a.astype(bf16), b.astype(bf16), (((1,), (1,)), ((), ())),
                           preferred_element_type=f32)


def _bdot_tn(a, b):
    return lax.dot_general(a.astype(bf16), b.astype(bf16), (((0,), (0,)), ((), ())),
                           preferred_element_type=f32)


def _segsum(x, seg_ref):
    hi = x.astype(bf16)
    lo = (x - hi.astype(f32)).astype(bf16)
    seg = seg_ref[...]
    return (jnp.dot(hi, seg, preferred_element_type=f32)
            + jnp.dot(lo, seg, preferred_element_type=f32))


def _lru_gates(xc, rg, vec):
    r = _sigmoid(rg[:, :BR_W] + vec(V_LBR))
    ig = _sigmoid(rg[:, BR_W:] + vec(V_LBI))
    log_a = -LRU_C * r * _softplus(-vec(V_LAM))
    a = jnp.exp(log_a)
    mult = jnp.sqrt(-jnp.tanh(log_a) * (a * a + 1.0))
    return a, mult * (ig * xc)


def _rwkv_prep(xs, vec, wlora_ref, seg_ref):
    r = xs[:, 0:BR_W]
    k = xs[:, BR_W:2 * BR_W]
    v = xs[:, 2 * BR_W:3 * BR_W]
    dwa = xs[:, 3 * BR_W:3 * BR_W + 2 * RWKV_RANK]
    lane = lax.broadcasted_iota(jnp.int32, dwa.shape, 1)
    lora_in = jnp.where(lane < RWKV_RANK, jnp.tanh(dwa), dwa)
    wa = jnp.dot(lora_in.astype(bf16), wlora_ref[...], preferred_element_type=f32)
    logw = -RWKV_DECAY_SCALE * _sigmoid(vec(V_W0) + wa[:, :BR_W])
    a = _sigmoid(vec(V_A0) + wa[:, BR_W:])
    kk = k * vec(V_KK)
    kkn = kk / jnp.maximum(jnp.sqrt(_segsum(kk * kk, seg_ref)), 1e-12)
    k2 = k * (1.0 + (a - 1.0) * vec(V_KA))
    bonus = _segsum(r * k2 * vec(V_RK), seg_ref) * v
    return r, k2, v, logw, a, kkn, bonus


def _rwkv_post(o, bonus, vec, seg_ref):
    inv_n = 1.0 / RWKV_HD
    mean = _segsum(o, seg_ref) * inv_n
    oc = o - mean
    var = _segsum(oc * oc, seg_ref) * inv_n
    return oc * lax.rsqrt(var + RWKV_LNX_EPS) * vec(V_LNXG) + vec(V_LNXB) + bonus


def _scan_rows(a, u, h0):
    n = a.shape[0]
    in_group = lax.broadcasted_iota(jnp.int32, a.shape, 0) & (SUBLANES - 1)
    d = 1
    while d < SUBLANES:
        keep = in_group >= d
        a_s = jnp.where(keep, pltpu.roll(a, d, axis=0), 1.0)
        u_s = jnp.where(keep, pltpu.roll(u, d, axis=0), 0.0)
        u = u + a * u_s
        a = a * a_s
        d *= 2
    carry = h0
    groups = []
    for g in range(n // SUBLANES):
        rs = slice(g * SUBLANES, (g + 1) * SUBLANES)
        hg = u[rs] + a[rs] * carry
        groups.append(hg)
        carry = hg[SUBLANES - 1:SUBLANES]
    return jnp.concatenate(groups, axis=0)


def _cond_kernel(c_ref, w_ref, b_ref, o_ref):
    o_ref[0] = _bdot(_silu(c_ref[...]), w_ref[0]) + b_ref[0]


def _cond_call(c_all, w_cond, b_cond):
    depth = w_cond.shape[0]
    n = c_all.shape[0]
    return pl.pallas_call(
        _cond_kernel,
        grid=(depth,),
        in_specs=[pl.BlockSpec((n, D_MODEL), lambda l: (0, 0)),
                  pl.BlockSpec((1, D_MODEL, 3 * D_MODEL), lambda l: (l, 0, 0)),
                  pl.BlockSpec((1, 1, 3 * D_MODEL), lambda l: (l, 0, 0))],
        out_specs=pl.BlockSpec((1, n, 3 * D_MODEL), lambda l: (l, 0, 0)),
        out_shape=jax.ShapeDtypeStruct((depth, n, 3 * D_MODEL), f32),
        compiler_params=pltpu.CompilerParams(dimension_semantics=("arbitrary",),
                                             vmem_limit_bytes=VMEM_LIMIT_BYTES),
        name="cond",
    )(c_all, w_cond, b_cond.reshape(depth, 1, 3 * D_MODEL))


def _prompt_kernel(x_ref, mod_ref, win_ref, wlru_ref, wlora_ref, wg_ref, bsx_ref, seg_ref, ltri_ref,
                   wbr_ref, wout_ref, v512_ref, mu_ref, v1024_ref,
                   y_ref, lconv_ref, lh_ref, rshift_ref, rs_ref, cconv_ref,
                   h_ref, lext_ref, lhc_ref, rprev_ref, s_ref, cext_ref, ycv_ref, ob_ref, rw_ref,
                   ecc_ref, o_ref, rhat_ref, mp_ref, nn_ref, shc_ref, *, alpha):
    T = TIME_TILE
    C = RWKV_CHUNK
    npair = RWKV_HEADS // 2
    first = pl.program_id(1) == 0

    def vec(row):
        return v512_ref[row:row + 1, :]

    def carried(x):
        return jnp.where(first, jnp.zeros_like(x), x)

    lext_ref[0:LRU_HALO, :] = carried(lext_ref[0:LRU_HALO, :])
    cext_ref[0:CONF_HALO, :] = carried(cext_ref[0:CONF_HALO, :])
    for q in range(npair):
        s_ref[q] = carried(s_ref[q])

    shift = mod_ref[0, 0:1, :]
    scale = mod_ref[0, 1:2, :]
    gate = mod_ref[0, 2:3, :]
    h_ref[...] = (x_ref[0] * (1.0 + scale) + shift).astype(bf16)

    def proj(off, n):
        return jnp.dot(h_ref[...], win_ref[:, off:off + n], preferred_element_type=f32)

    pa = proj(OFF_LRU, 2 * BR_W)
    lext_ref[LRU_HALO:LRU_HALO + T, :] = pa[:, :BR_W]
    xc = vec(V_LCB)
    for j in range(LRU_CONV):
        xc = xc + vec(V_LCW + j) * lext_ref[pl.ds(LRU_HALO - (LRU_CONV - 1) + j, T), :]
    lext_ref[0:LRU_HALO, :] = lext_ref[T:T + LRU_HALO, :]
    rg = jnp.dot(xc.astype(bf16), wlru_ref[...], preferred_element_type=f32)

    pending = [(key, off + o, min(PROJ_CHUNK, n - o))
               for key, off, n in (("b", OFF_RWKV, RWKV_SHIFT_W + BR_W), ("d", OFF_CONF, 3 * BR_W),
                                   ("c", OFF_GMLP, 3 * BR_W))
               for o in range(0, n, PROJ_CHUNK)]
    chunks = {"b": [], "d": [], "c": []}

    def pump(count):
        for _ in range(count):
            if pending:
                key, off, width = pending.pop(0)
                chunks[key].append(proj(off, width))

    pump(-(-(RWKV_SHIFT_W + BR_W) // PROJ_CHUNK))
    pb = jnp.concatenate(chunks["b"], axis=1)
    per_stage = -(-len(pending) // 4)

    a, u = _lru_gates(xc, rg, vec)
    hfull = _scan_rows(a, u, carried(lhc_ref[0:1, :]))
    lhc_ref[0:1, :] = hfull[T - 1:T, :]
    ob_ref[0] = (hfull * _silu(pa[:, BR_W:])).astype(bf16)

    p = pb[:, :RWKV_SHIFT_W]
    rows = lax.broadcasted_iota(jnp.int32, p.shape, 0)
    prev = jnp.where(rows == 0, carried(rprev_ref[0:1, :]), pltpu.roll(p, 1, axis=0))
    rprev_ref[0:1, :] = p[T - 1:T, :]
    xs = p + (prev - p) * mu_ref[...]
    r = xs[:, 0:BR_W]
    k = xs[:, BR_W:2 * BR_W]
    v = xs[:, 2 * BR_W:3 * BR_W]
    dwa = xs[:, 3 * BR_W:3 * BR_W + 2 * RWKV_RANK]
    lora_in = jnp.where(lax.broadcasted_iota(jnp.int32, dwa.shape, 1) < RWKV_RANK, jnp.tanh(dwa), dwa)
    kk = k * vec(V_KK)
    pump(per_stage)
    wa = jnp.dot(lora_in.astype(bf16), wlora_ref[...], preferred_element_type=f32)
    logw = -RWKV_DECAY_SCALE * _sigmoid(vec(V_W0) + wa[:, :BR_W])
    ag = _sigmoid(vec(V_A0) + wa[:, BR_W:])
    k2 = k * (1.0 + (ag - 1.0) * vec(V_KA))
    pump(per_stage)
    kkn = kk / jnp.maximum(jnp.sqrt(_segsum(kk * kk, seg_ref)), 1e-12)
    bonus = _segsum(r * k2 * vec(V_RK), seg_ref) * v
    lw1 = logw.astype(bf16)
    rem = logw - lw1.astype(f32)
    lw2 = rem.astype(bf16)
    lw3 = (rem - lw2.astype(f32)).astype(bf16)
    pump(per_stage)
    ltri = ltri_ref[...]
    c = (jnp.dot(ltri, lw1, preferred_element_type=f32) + jnp.dot(ltri, lw2, preferred_element_type=f32)
         + jnp.dot(ltri, lw3, preferred_element_type=f32))
    cend = jnp.concatenate(
        [jnp.broadcast_to(c[(n + 1) * C - 1:(n + 1) * C, :], (C, BR_W)) for n in range(T // C)], axis=0)
    for n in range(T // C):
        ecc_ref[n * SUBLANES:(n + 1) * SUBLANES, :] = jnp.broadcast_to(
            jnp.exp(c[(n + 1) * C - 1:(n + 1) * C, :]), (SUBLANES, BR_W))
    e_neg = jnp.exp(-c)
    e_end = jnp.exp(cend - c)
    beta = kkn * ag
    rw_ref[0] = (-kkn * jnp.exp(c - logw)).astype(bf16)
    rw_ref[1] = (r * jnp.exp(c)).astype(bf16)
    rw_ref[2] = (beta * e_neg).astype(bf16)
    rw_ref[3] = (k2 * e_neg).astype(bf16)
    rw_ref[4] = v.astype(bf16)
    rw_ref[5] = (beta * e_end).astype(bf16)
    rw_ref[6] = (k2 * e_end).astype(bf16)
    pump(len(pending))
    pd = jnp.concatenate(chunks["d"], axis=1)
    pc = jnp.concatenate(chunks["c"], axis=1)

    R = lax.broadcasted_iota(jnp.int32, (2 * C, LANES), 0)
    Cc = lax.broadcasted_iota(jnp.int32, (2 * C, LANES), 1)
    t_idx = R & (C - 1)
    s_idx = Cc & (C - 1)
    lower = s_idx < t_idx
    mask_a = jnp.where(R < C, lower.astype(f32), (s_idx <= t_idx).astype(f32)).astype(bf16)
    block_mask = ((R >> CHUNK_SHIFT) == (Cc >> CHUNK_SHIFT)).astype(f32)
    eye = (R == Cc).astype(f32)
    lvl0_mask = (((R >> 1) == (Cc >> 1)) & lower).astype(f32)
    lvl_masks = []
    sh = 1
    while (1 << sh) < C:
        lvl_masks.append((((R >> (sh + 1)) == (Cc >> (sh + 1))) & ((R >> sh) != (Cc >> sh)) & lower)
                         .astype(f32).astype(bf16))
        sh += 1
    left64 = lax.broadcasted_iota(jnp.int32, (C, LANES), 1) < C
    left128 = Cc < C

    def sel_l(x, m):
        return jnp.where(m, x, jnp.zeros_like(x))

    def sel_r(x, m):
        return jnp.where(m, jnp.zeros_like(x), x)

    left256 = (lax.broadcasted_iota(jnp.int32, (C, 2 * LANES), 1) & (LANES - 1)) < C

    def split_lr(x, m):
        return jnp.concatenate([sel_l(x, m), sel_r(x, m)], axis=0)

    cext_ref[CONF_HALO:CONF_HALO + T, :] = pd[:, :BR_W] * _sigmoid(pd[:, BR_W:2 * BR_W])
    conf_gate = _silu(pd[:, 2 * BR_W:])

    vn = _layernorm(pc[:, BR_W:2 * BR_W], vec(V_GLG), vec(V_GLB))
    left_g = lax.broadcasted_iota(jnp.int32, (GMLP_CHUNK, LANES), 1) < (LANES // 2)
    z_rows = []
    for n in range(T // GMLP_CHUNK):
        z_cols = []
        for q in range(GMLP_GROUPS // 2):
            vq = vn[n * GMLP_CHUNK:(n + 1) * GMLP_CHUNK, q * LANES:(q + 1) * LANES].astype(bf16)
            rhs = jnp.concatenate([sel_l(vq, left_g), sel_r(vq, left_g)], axis=0)
            z_cols.append(jnp.dot(wg_ref[q], rhs, preferred_element_type=f32))
        z_rows.append(jnp.concatenate(z_cols, axis=1) + bsx_ref[...])
    z = jnp.concatenate(z_rows, axis=0)
    ob_ref[2] = (pc[:, :BR_W] * z * _silu(pc[:, 2 * BR_W:])).astype(bf16)

    items = [(n, q) for n in range(T // C) for q in range(npair)]
    for g0 in range(0, len(items), RWKV_GROUP):
        grp = items[g0:g0 + RWKV_GROUP]

        def ld(kind):
            return [rw_ref[kind, n * C:(n + 1) * C, q * LANES:(q + 1) * LANES] for n, q in grp]

        at, rt, bt, kt, vv, b_end, k_end = [ld(kind) for kind in range(7)]
        ar = [jnp.concatenate([a_, r_], axis=0) for a_, r_ in zip(at, rt)]
        out_e = [_bdot_nt(sel_l(x, left128), jnp.concatenate([b_, k_], axis=0)) for x, b_, k_ in zip(ar, bt, kt)]
        out_o = [_bdot_nt(sel_r(x, left128), jnp.concatenate([k_, b_], axis=0)) for x, b_, k_ in zip(ar, bt, kt)]
        tm = [eye + jnp.concatenate([sel_l(e_[0:C], left64), sel_r(o_[0:C], left64)], axis=0) * lvl0_mask
              for e_, o_ in zip(out_e, out_o)]
        out_e = [x.astype(bf16) * mask_a for x in out_e]
        out_o = [x.astype(bf16) * mask_a for x in out_o]
        a_pair = [jnp.concatenate([sel_l(e_[0:C], left64), sel_r(o_[0:C], left64)], axis=0)
                  for e_, o_ in zip(out_e, out_o)]
        lhs_ak = [jnp.where(left64, o_[0:C], e_[0:C]) for e_, o_ in zip(out_e, out_o)]
        lhs_rb = [jnp.where(left64, e_[C:2 * C], o_[C:2 * C]) for e_, o_ in zip(out_e, out_o)]
        lhs_rk = [jnp.where(left64, o_[C:2 * C], e_[C:2 * C]) for e_, o_ in zip(out_e, out_o)]
        v_rl = [jnp.concatenate([sel_r(x, left64), sel_l(x, left64)], axis=0) for x in vv]
        av = [jnp.dot(l_, x, preferred_element_type=f32).astype(bf16) for l_, x in zip(lhs_ak, v_rl)]
        ark = [jnp.dot(l_, x, preferred_element_type=f32) for l_, x in zip(lhs_rk, v_rl)]
        for lm in lvl_masks:
            w = [_bdot(x * lm, t_) for x, t_ in zip(a_pair, tm)]
            tm = [t_ + _bdot(t_, w_) for t_, w_ in zip(tm, w)]
        t_row = [(t_[0:C] + t_[C:2 * C]).astype(bf16) for t_ in tm]
        tx = [jnp.dot(t_, split_lr(jnp.concatenate([a_, x], axis=1), left256),
                      preferred_element_type=f32).astype(bf16)
              for t_, a_, x in zip(t_row, at, av)]
        arb = [jnp.dot(l_, split_lr(x, left256), preferred_element_type=f32) for l_, x in zip(lhs_rb, tx)]
        mp = [_bdot_tn(x[:, 0:LANES], b_) * block_mask for x, b_ in zip(tx, b_end)]
        nn = [_bdot_tn(jnp.concatenate([x[:, LANES:2 * LANES], v_], axis=0),
                       jnp.concatenate([b_, k_], axis=0)) * block_mask
              for x, v_, b_, k_ in zip(tx, vv, b_end, k_end)]
        for idx, (n, q) in enumerate(grp):
            rs = slice(n * C, (n + 1) * C)
            ls = slice(q * LANES, (q + 1) * LANES)
            rhat_ref[rs, ls] = (rt[idx].astype(f32) + arb[idx][:, 0:LANES]).astype(bf16)
            o_ref[rs, ls] = arb[idx][:, LANES:2 * LANES] + ark[idx]
            mp_ref[n * npair + q] = mp[idx].astype(bf16)
            nn_ref[n * npair + q] = nn[idx]

    for n in range(T // C):
        rs = slice(n * C, (n + 1) * C)
        sp = [s_ref[q] for q in range(npair)]
        sb = [x.astype(bf16) for x in sp]
        o_blk = [_bdot_nt(rhat_ref[rs, q * LANES:(q + 1) * LANES], sb[q]) for q in range(npair)]
        s_new = [jnp.dot(sb[q], mp_ref[n * npair + q], preferred_element_type=f32) for q in range(npair)]
        for q in range(npair):
            ls = slice(q * LANES, (q + 1) * LANES)
            o_ref[rs, ls] = o_ref[rs, ls] + o_blk[q]
            s_ref[q] = sp[q] * ecc_ref[n * SUBLANES:n * SUBLANES + 1, ls] + s_new[q] + nn_ref[n * npair + q]

    ob_ref[1] = (_rwkv_post(o_ref[...], bonus, vec, seg_ref) * _silu(pb[:, RWKV_SHIFT_W:])).astype(bf16)

    def merge_term(nb):
        return _merge_gate_times(proj(OFF_MERGE + nb * D_MODEL, D_MODEL),
                                 jnp.dot(ob_ref[nb], wbr_ref[nb], preferred_element_type=f32))

    m = merge_term(0) + merge_term(1) + merge_term(2)

    base = CONF_HALO - (CONF_K - 1)
    shift_rows = T + CONF_HALO - SUBLANES
    for lb in range(BR_W // LANES):
        ls = slice(lb * LANES, (lb + 1) * LANES)
        for s in range(1, SUBLANES):
            shc_ref[s, 0:shift_rows, :] = cext_ref[pl.ds(s, shift_rows), ls]
        for rb in range(T // CONV_ROWS):
            acc = jnp.broadcast_to(v512_ref[V_CDB:V_CDB + 1, ls], (CONV_ROWS, LANES))
            for j in range(CONF_K):
                q, s = divmod(base + j, SUBLANES)
                row0 = rb * CONV_ROWS + q * SUBLANES
                if s == 0:
                    tap_in = cext_ref[row0:row0 + CONV_ROWS, ls]
                else:
                    tap_in = shc_ref[s, row0:row0 + CONV_ROWS, :]
                acc = acc + v512_ref[V_CDW + j:V_CDW + j + 1, ls] * tap_in
            ycv_ref[rb * CONV_ROWS:(rb + 1) * CONV_ROWS, ls] = acc
    cext_ref[0:CONF_HALO, :] = cext_ref[T:T + CONF_HALO, :]
    ob_ref[3] = (_silu(_layernorm(ycv_ref[...], vec(V_CLG), vec(V_CLB))) * conf_gate).astype(bf16)

    m = m + merge_term(3)
    yv = jnp.dot(m.astype(bf16), wout_ref[...], preferred_element_type=f32) + v1024_ref[M_BOUT:M_BOUT + 1, :]
    xn = alpha * x_ref[0] + gate * yv
    y_ref[0] = _layernorm(xn, v1024_ref[M_LNG:M_LNG + 1, :], v1024_ref[M_LNB:M_LNB + 1, :])

    lconv_ref[0] = lext_ref[LRU_HALO - (LRU_CONV - 1):LRU_HALO, :]
    lh_ref[0] = lhc_ref[0:1, :]
    rshift_ref[0] = rprev_ref[0:1, :]
    cconv_ref[0] = cext_ref[CONF_HALO - (CONF_K - 1):CONF_HALO, :]
    for q in range(npair):
        sp = s_ref[q]
        rs_ref[0, 2 * q] = sp[0:C, 0:C]
        rs_ref[0, 2 * q + 1] = sp[C:2 * C, C:2 * C]


def _resident(shape):
    nd = len(shape)
    return pl.BlockSpec(shape, lambda b, i: (0,) * nd, pipeline_mode=pl.Buffered(1))


def _resident_layer(shape, l):
    nd = len(shape) - 1
    return pl.BlockSpec((None,) + tuple(shape[1:]), lambda b, i: (l,) + (0,) * nd,
                        pipeline_mode=pl.Buffered(1))


def _prompt_call(x, mod3, lw, big, l, alpha):
    nb, seq, _ = x.shape
    T = TIME_TILE
    assert seq % T == 0 and T % GMLP_CHUNK == 0 and T % RWKV_CHUNK == 0
    nt = seq // T
    in_specs = [
        pl.BlockSpec((1, T, D_MODEL), lambda b, i: (b, i, 0)),
        pl.BlockSpec((1, 3, D_MODEL), lambda b, i: (b, 0, 0)),
        _resident_layer(big["win"].shape, l), _resident(lw["wlru"].shape), _resident(lw["wlora"].shape),
        _resident(lw["wg"].shape), _resident(lw["bsx"].shape), _resident(lw["seg"].shape),
        _resident(lw["ltri"].shape), _resident_layer(big["wbr"].shape, l),
        _resident_layer(big["wout"].shape, l),
        _resident(lw["v512"].shape), _resident(lw["mu"].shape), _resident(lw["v1024"].shape),
    ]
    out_shape = (
        jax.ShapeDtypeStruct((nb, seq, D_MODEL), f32),
        jax.ShapeDtypeStruct((nb, LRU_CONV - 1, BR_W), f32),
        jax.ShapeDtypeStruct((nb, 1, BR_W), f32),
        jax.ShapeDtypeStruct((nb, 1, RWKV_SHIFT_W), f32),
        jax.ShapeDtypeStruct((nb, RWKV_HEADS, RWKV_HD, RWKV_HD), f32),
        jax.ShapeDtypeStruct((nb, CONF_K - 1, BR_W), f32),
    )
    out_specs = (
        pl.BlockSpec((1, T, D_MODEL), lambda b, i: (b, i, 0)),
        pl.BlockSpec((1, LRU_CONV - 1, BR_W), lambda b, i: (b, 0, 0)),
        pl.BlockSpec((1, 1, BR_W), lambda b, i: (b, 0, 0)),
        pl.BlockSpec((1, 1, RWKV_SHIFT_W), lambda b, i: (b, 0, 0)),
        pl.BlockSpec((1, RWKV_HEADS, RWKV_HD, RWKV_HD), lambda b, i: (b, 0, 0, 0)),
        pl.BlockSpec((1, CONF_K - 1, BR_W), lambda b, i: (b, 0, 0)),
    )
    scratch = [
        pltpu.VMEM((T, D_MODEL), bf16),
        pltpu.VMEM((T + LRU_HALO, BR_W), f32),
        pltpu.VMEM((SUBLANES, BR_W), f32),
        pltpu.VMEM((SUBLANES, RWKV_SHIFT_W), f32),
        pltpu.VMEM((RWKV_HEADS // 2, 2 * RWKV_HD, 2 * RWKV_HD), f32),
        pltpu.VMEM((T + CONF_HALO, BR_W), f32),
        pltpu.VMEM((T, BR_W), f32),
        pltpu.VMEM((N_BRANCH, T, BR_W), bf16),
        pltpu.VMEM((7, T, BR_W), bf16),
        pltpu.VMEM((T // RWKV_CHUNK * SUBLANES, BR_W), f32),
        pltpu.VMEM((T, BR_W), f32),
        pltpu.VMEM((T, BR_W), bf16),
        pltpu.VMEM((T // RWKV_CHUNK * (RWKV_HEADS // 2), 2 * RWKV_HD, 2 * RWKV_HD), bf16),
        pltpu.VMEM((T // RWKV_CHUNK * (RWKV_HEADS // 2), 2 * RWKV_HD, 2 * RWKV_HD), f32),
        pltpu.VMEM((SUBLANES, T + CONF_HALO, LANES), f32),
    ]
    return pl.pallas_call(
        functools.partial(_prompt_kernel, alpha=alpha),
        grid=(nb, nt),
        in_specs=in_specs,
        out_specs=out_specs,
        out_shape=out_shape,
        scratch_shapes=scratch,
        compiler_params=pltpu.CompilerParams(dimension_semantics=("arbitrary", "arbitrary"),
                                             vmem_limit_bytes=VMEM_LIMIT_BYTES, flags=PROMPT_FLAGS),
        name="prompt_layer",
    )(x, mod3, big["win"], lw["wlru"], lw["wlora"], lw["wg"], lw["bsx"], lw["seg"], lw["ltri"],
      big["wbr"], big["wout"], lw["v512"], lw["mu"], lw["v1024"])


def _sample_proj_kernel(x_ref, mod_ref, w_ref, o_ref):
    shift = mod_ref[:, 0:D_MODEL]
    scale = mod_ref[:, D_MODEL:2 * D_MODEL]
    h = (x_ref[...] * (1.0 + scale) + shift).astype(bf16)
    o_ref[...] = jnp.dot(h, w_ref[...], preferred_element_type=f32)


SAMPLE_PROJ_COLS = 1152


def _sample_proj_call(x, mod, win, l):
    n = x.shape[0]
    assert N_IN % SAMPLE_PROJ_COLS == 0
    return pl.pallas_call(
        _sample_proj_kernel,
        grid=(N_IN // SAMPLE_PROJ_COLS,),
        in_specs=[pl.BlockSpec((n, D_MODEL), lambda j: (0, 0)),
                  pl.BlockSpec((n, 3 * D_MODEL), lambda j: (0, 0)),
                  pl.BlockSpec((None, D_MODEL, SAMPLE_PROJ_COLS), lambda j: (l, 0, j))],
        out_specs=pl.BlockSpec((n, SAMPLE_PROJ_COLS), lambda j: (0, j)),
        out_shape=jax.ShapeDtypeStruct((n, N_IN), f32),
        compiler_params=pltpu.CompilerParams(dimension_semantics=("arbitrary",),
                                             vmem_limit_bytes=VMEM_LIMIT_BYTES),
        name="sample_proj",
    )(x, mod, win)


def _sample_pre_kernel(p_ref, lconv_ref, lh_ref, rshift_ref, cconv_ref,
                       wlru_ref, wlora_ref, seg_ref, v512_ref, mu_ref,
                       ob_ref, aux_ref, vecs_ref, lconv_o, lh_o, rshift_o, cconv_o, gv_o):
    def vec(row):
        return v512_ref[row:row + 1, :]

    xb = p_ref[:, OFF_LRU:OFF_LRU + BR_W]
    xc = vec(V_LCB) + vec(V_LCW + LRU_CONV - 1) * xb
    for j in range(LRU_CONV - 1):
        xc = xc + vec(V_LCW + j) * lconv_ref[j]
    for j in range(LRU_CONV - 2):
        lconv_o[j] = lconv_ref[j + 1]
    lconv_o[LRU_CONV - 2] = xb
    a, u = _lru_gates(xc, jnp.dot(xc.astype(bf16), wlru_ref[...], preferred_element_type=f32), vec)
    hn = a * lh_ref[...] + u
    lh_o[...] = hn
    ob_ref[:, 0:BR_W] = hn * _silu(p_ref[:, OFF_LRU + BR_W:OFF_LRU + 2 * BR_W])

    p = p_ref[:, OFF_RWKV:OFF_RWKV + RWKV_SHIFT_W]
    xs = p + (rshift_ref[...] - p) * mu_ref[...]
    rshift_o[...] = p
    r, k2, v, logw, ag, kkn, bonus = _rwkv_prep(xs, vec, wlora_ref, seg_ref)
    n = RWKV_HD
    for idx, x in enumerate((-kkn, kkn * ag, k2, r, jnp.exp(logw), v)):
        for hd in range(RWKV_HEADS):
            vecs_ref[idx, hd] = x[:, hd * n:(hd + 1) * n].T
    aux_ref[:, 0:BR_W] = bonus
    aux_ref[:, BR_W:2 * BR_W] = _silu(p_ref[:, OFF_RWKV + RWKV_SHIFT_W:OFF_RWKV + RWKV_SHIFT_W + BR_W])

    vn = _layernorm(p_ref[:, OFF_GMLP + BR_W:OFF_GMLP + 2 * BR_W], vec(V_GLG), vec(V_GLB))
    gv_o[...] = vn
    z = vec(V_GWS0) * vn + vec(V_GBS0)
    ob_ref[:, BR_W:2 * BR_W] = (p_ref[:, OFF_GMLP:OFF_GMLP + BR_W] * z
                                * _silu(p_ref[:, OFF_GMLP + 2 * BR_W:OFF_GMLP + 3 * BR_W]))

    glu = p_ref[:, OFF_CONF:OFF_CONF + BR_W] * _sigmoid(p_ref[:, OFF_CONF + BR_W:OFF_CONF + 2 * BR_W])
    y = vec(V_CDB) + vec(V_CDW + CONF_K - 1) * glu
    for j in range(CONF_K - 1):
        y = y + vec(V_CDW + j) * cconv_ref[j]
    for j in range(CONF_K - 2):
        cconv_o[j] = cconv_ref[j + 1]
    cconv_o[CONF_K - 2] = glu
    ob_ref[:, 2 * BR_W:3 * BR_W] = (_silu(_layernorm(y, vec(V_CLG), vec(V_CLB)))
                                    * _silu(p_ref[:, OFF_CONF + 2 * BR_W:OFF_CONF + 3 * BR_W]))


def _sample_rwkv_kernel(s_ref, vecs_ref, s_o, ot_o):
    neg_kk = vecs_ref[0]
    beta = vecs_ref[1]
    k2 = vecs_ref[2]
    r = vecs_ref[3]
    w = vecs_ref[4]
    o_rows = []
    for vi in range(RWKV_HD):
        s0 = s_ref[vi]
        sa = jnp.sum(s0 * neg_kk, axis=0, keepdims=True)
        sn = s0 * w + sa * beta + vecs_ref[5, vi:vi + 1, :] * k2
        s_o[vi] = sn
        o_rows.append(jnp.sum(sn * r, axis=0, keepdims=True))
    ot_o[...] = jnp.concatenate(o_rows, axis=0)


def _sample_pre_call(proj, lconv_t, lh, rshift, cconv_t, lw, l):
    n = proj.shape[0]

    def whole(shape):
        nd = len(shape)
        return pl.BlockSpec(shape, lambda i: (0,) * nd, pipeline_mode=pl.Buffered(1))

    def layer(shape):
        nd = len(shape)
        return pl.BlockSpec((None,) + shape, lambda i: (l,) + (0,) * nd, pipeline_mode=pl.Buffered(1))

    state_shapes = [(LRU_CONV - 1, n, BR_W), (n, BR_W), (n, RWKV_SHIFT_W), (CONF_K - 1, n, BR_W)]
    out_shapes = ([(n, (N_BRANCH - 1) * BR_W), (n, 2 * BR_W), (6, RWKV_HEADS, RWKV_HD, n)]
                  + state_shapes + [(n, BR_W)])
    return pl.pallas_call(
        _sample_pre_kernel,
        grid=(1,),
        in_specs=([pl.BlockSpec((n, OFF_MERGE), lambda i: (0, 0), pipeline_mode=pl.Buffered(1))]
                  + [layer(s) for s in state_shapes]
                  + [whole(lw["wlru"].shape), whole(lw["wlora"].shape), whole(lw["seg"].shape),
                     whole(lw["v512"].shape), whole(lw["mu"].shape)]),
        out_specs=[whole(s) for s in out_shapes],
        out_shape=[jax.ShapeDtypeStruct(s, f32) for s in out_shapes],
        compiler_params=pltpu.CompilerParams(dimension_semantics=("arbitrary",),
                                             vmem_limit_bytes=VMEM_LIMIT_BYTES),
        name="sample_pre",
    )(proj, lconv_t, lh, rshift, cconv_t, lw["wlru"], lw["wlora"], lw["seg"], lw["v512"], lw["mu"])


def _sample_rwkv_call(s_t, vecs, l):
    _, nh, nv, nk, n = s_t.shape
    return pl.pallas_call(
        _sample_rwkv_kernel,
        grid=(nh,),
        in_specs=[pl.BlockSpec((None, None, nv, nk, n), lambda h: (l, h, 0, 0, 0)),
                  pl.BlockSpec((6, None, nk, n), lambda h: (0, h, 0, 0))],
        out_specs=[pl.BlockSpec((None, nv, nk, n), lambda h: (h, 0, 0, 0)),
                   pl.BlockSpec((None, nv, n), lambda h: (h, 0, 0))],
        out_shape=[jax.ShapeDtypeStruct((nh, nv, nk, n), f32), jax.ShapeDtypeStruct((nh, nv, n), f32)],
        compiler_params=pltpu.CompilerParams(dimension_semantics=("arbitrary",),
                                             vmem_limit_bytes=VMEM_LIMIT_BYTES),
        name="sample_rwkv",
    )(s_t, vecs)


def _sample_merge_kernel(x_ref, mod_ref, p_ref, ob_ref, aux_ref, ot_ref, seg_ref, v512_ref, wbr_ref, wout_ref,
                         v1024_ref, y_ref, *, alpha):
    def vec(row):
        return v512_ref[row:row + 1, :]

    o = jnp.concatenate([ot_ref[hd].T for hd in range(RWKV_HEADS)], axis=1)
    ob_rwkv = _rwkv_post(o, aux_ref[:, 0:BR_W], vec, seg_ref) * aux_ref[:, BR_W:2 * BR_W]
    branches = [ob_ref[:, 0:BR_W], ob_rwkv, ob_ref[:, BR_W:2 * BR_W], ob_ref[:, 2 * BR_W:3 * BR_W]]
    m = None
    for nb in range(N_BRANCH):
        off = OFF_MERGE + nb * D_MODEL
        term = _merge_gate_times(p_ref[:, off:off + D_MODEL], _bdot(branches[nb], wbr_ref[nb]))
        m = term if m is None else m + term
    yv = _bdot(m, wout_ref[...]) + v1024_ref[M_BOUT:M_BOUT + 1, :]
    xn = alpha * x_ref[...] + mod_ref[:, 2 * D_MODEL:3 * D_MODEL] * yv
    y_ref[...] = _layernorm(xn, v1024_ref[M_LNG:M_LNG + 1, :], v1024_ref[M_LNB:M_LNB + 1, :])


def _sample_merge_call(x, mod, proj, ob, aux, ot, lw, big, l, alpha):
    n = x.shape[0]
    return pl.pallas_call(
        functools.partial(_sample_merge_kernel, alpha=alpha),
        grid=(1,),
        in_specs=[pl.BlockSpec((n, D_MODEL), lambda i: (0, 0)),
                  pl.BlockSpec((n, 3 * D_MODEL), lambda i: (0, 0)),
                  pl.BlockSpec((n, N_IN), lambda i: (0, 0)),
                  pl.BlockSpec(ob.shape, lambda i: (0, 0)),
                  pl.BlockSpec(aux.shape, lambda i: (0, 0)),
                  pl.BlockSpec(ot.shape, lambda i: (0, 0, 0)),
                  pl.BlockSpec(lw["seg"].shape, lambda i: (0, 0)),
                  pl.BlockSpec(lw["v512"].shape, lambda i: (0, 0)),
                  pl.BlockSpec((None,) + big["wbr"].shape[1:], lambda i: (l, 0, 0, 0)),
                  pl.BlockSpec((None,) + big["wout"].shape[1:], lambda i: (l, 0, 0)),
                  pl.BlockSpec(lw["v1024"].shape, lambda i: (0, 0))],
        out_specs=pl.BlockSpec((n, D_MODEL), lambda i: (0, 0)),
        out_shape=jax.ShapeDtypeStruct((n, D_MODEL), f32),
        compiler_params=pltpu.CompilerParams(dimension_semantics=("arbitrary",),
                                             vmem_limit_bytes=VMEM_LIMIT_BYTES),
        name="sample_merge",
    )(x, mod, proj, ob, aux, ot, lw["seg"], lw["v512"], big["wbr"], big["wout"], lw["v1024"])


def _pack_layer(l, w_in, lru_conv_w, lru_conv_b, lru_wr, lru_br, lru_wi, lru_bi, lru_lambda, rwkv_mu,
                rwkv_w0, rwkv_ww, rwkv_a0, rwkv_wa, rwkv_kk, rwkv_ka, rwkv_rk, rwkv_lnx_g, rwkv_lnx_b,
                gmlp_ln_g, gmlp_ln_b, gmlp_ws, gmlp_bs, conf_dw_w, conf_dw_b, conf_ln_g, conf_ln_b,
                w_branch, w_out, b_out, ln_g, ln_b):
    rep = BR_W // GMLP_GROUPS
    wr = block_diag(*[lru_wr[l, h] for h in range(LRU_HEADS)])
    wi = block_diag(*[lru_wi[l, h] for h in range(LRU_HEADS)])
    zero_lora = jnp.zeros((RWKV_RANK, BR_W), f32)
    wlora = jnp.concatenate([jnp.concatenate([rwkv_ww[l], zero_lora], axis=1),
                             jnp.concatenate([zero_lora, rwkv_wa[l]], axis=1)], axis=0)
    tril = jnp.tril(jnp.ones((GMLP_CHUNK, GMLP_CHUNK), dtype=bool))
    wm = jnp.where(tril[None], gmlp_ws[l], 0.0)
    wg = jnp.stack([jnp.concatenate([wm[2 * q], wm[2 * q + 1]], axis=1) for q in range(GMLP_GROUPS // 2)])
    rows = [lru_conv_w[l], lru_conv_b[l][None], lru_br[l][None], lru_bi[l][None], lru_lambda[l][None],
            rwkv_w0[l][None], rwkv_a0[l][None], rwkv_kk[l][None], rwkv_ka[l][None],
            rwkv_rk[l].reshape(1, BR_W), rwkv_lnx_g[l][None], rwkv_lnx_b[l][None],
            gmlp_ln_g[l][None], gmlp_ln_b[l][None], conf_dw_b[l][None], conf_ln_g[l][None],
            conf_ln_b[l][None], jnp.repeat(gmlp_ws[l, :, 0, 0], rep)[None],
            jnp.repeat(gmlp_bs[l, :, 0], rep)[None], jnp.zeros((V_CDW - V_GBS0 - 1, BR_W), f32),
            conf_dw_w[l], jnp.zeros((V_ROWS - V_CDW - CONF_K, BR_W), f32)]
    v1024 = jnp.concatenate([b_out[l][None], ln_g[l][None], ln_b[l][None],
                             jnp.zeros((SUBLANES - 3, D_MODEL), f32)], axis=0)
    nchunk = TIME_TILE // RWKV_CHUNK
    return dict(
        wlru=jnp.concatenate([wr, wi], axis=1).astype(bf16),
        wlora=wlora.astype(bf16),
        wg=wg.astype(bf16),
        bsx=jnp.repeat(gmlp_bs[l].T, rep, axis=1),
        seg=jnp.kron(jnp.eye(RWKV_HEADS, dtype=f32), jnp.ones((RWKV_HD, RWKV_HD), f32)).astype(bf16),
        ltri=jnp.kron(jnp.eye(nchunk, dtype=f32), jnp.tril(jnp.ones((RWKV_CHUNK, RWKV_CHUNK), f32))).astype(bf16),
        v512=jnp.concatenate(rows, axis=0),
        mu=rwkv_mu[l][None],
        v1024=v1024,
    )


def kernel(x_prompt, x_sample, state_lru_conv, state_lru_h, state_rwkv_shift, state_rwkv_S, state_conf_conv, c_prompt, c_sample, w_cond, b_cond, w_in, lru_conv_w, lru_conv_b, lru_wr, lru_br, lru_wi, lru_bi, lru_lambda, rwkv_mu, rwkv_w0, rwkv_ww, rwkv_a0, rwkv_wa, rwkv_kk, rwkv_ka, rwkv_rk, rwkv_lnx_g, rwkv_lnx_b, gmlp_ln_g, gmlp_ln_b, gmlp_ws, gmlp_bs, conf_dw_w, conf_dw_b, conf_ln_g, conf_ln_b, w_branch, w_out, b_out, ln_g, ln_b):
    depth = w_in.shape[0]
    alpha = (2.0 * depth) ** 0.25
    nb = x_prompt.shape[0]
    ns = x_sample.shape[0]

    mod = _cond_call(jnp.concatenate([c_prompt, c_sample], axis=0), w_cond, b_cond)
    col = lax.broadcasted_iota(jnp.int32, (1, 1, N_IN), 2)
    big = dict(win=(w_in * jnp.where(col >= OFF_MERGE, 0.5, 1.0)).astype(bf16),
               wbr=(0.5 * w_branch).astype(bf16), wout=w_out.astype(bf16))
    lconv_t = jnp.transpose(state_lru_conv, (0, 2, 1, 3))
    cconv_t = jnp.transpose(state_conf_conv, (0, 2, 1, 3))
    s_t = jnp.transpose(state_rwkv_S, (0, 2, 3, 4, 1))
    xp = x_prompt
    xs = x_sample.reshape(ns, D_MODEL)
    outs_p, outs_s = [], []
    for l in range(depth):
        lw = _pack_layer(l, w_in, lru_conv_w, lru_conv_b, lru_wr, lru_br, lru_wi, lru_bi, lru_lambda, rwkv_mu,
                         rwkv_w0, rwkv_ww, rwkv_a0, rwkv_wa, rwkv_kk, rwkv_ka, rwkv_rk, rwkv_lnx_g,
                         rwkv_lnx_b, gmlp_ln_g, gmlp_ln_b, gmlp_ws, gmlp_bs, conf_dw_w, conf_dw_b,
                         conf_ln_g, conf_ln_b, w_branch, w_out, b_out, ln_g, ln_b)
        mod_p = mod[l, :nb].reshape(nb, 3, D_MODEL)
        mod_s = mod[l, nb:]
        xp, lconv_p, lh_p, rshift_p, rs_p, cconv_p = _prompt_call(xp, mod_p, lw, big, l, alpha)
        outs_p.append((lconv_p, lh_p.reshape(nb, BR_W), rshift_p.reshape(nb, RWKV_SHIFT_W), rs_p, cconv_p))

        proj_s = _sample_proj_call(xs, mod_s, big["win"], l)
        ob, aux, vecs, lconv_s, lh_s, rshift_s, cconv_s, gv_s = _sample_pre_call(
            proj_s, lconv_t, state_lru_h, state_rwkv_shift, cconv_t, lw, l)
        rs_s, ot = _sample_rwkv_call(s_t, vecs, l)
        xs = _sample_merge_call(xs, mod_s, proj_s, ob, aux, ot, lw, big, l, alpha)
        outs_s.append((lconv_s, lh_s, rshift_s, rs_s, cconv_s, gv_s.reshape(ns, 1, BR_W)))

    def stk(outs, j):
        return jnp.stack([o[j] for o in outs])

    return (xp, xs.reshape(ns, 1, D_MODEL),
            stk(outs_p, 0), jnp.transpose(stk(outs_s, 0), (0, 2, 1, 3)),
            stk(outs_p, 1), stk(outs_s, 1),
            stk(outs_p, 2), stk(outs_s, 2),
            stk(outs_p, 3), jnp.transpose(stk(outs_s, 3), (0, 4, 1, 2, 3)),
            stk(outs_p, 4), jnp.transpose(stk(outs_s, 4), (0, 2, 1, 3)),
            stk(outs_s, 5))
```

```python
import functools

import jax
import jax.numpy as jnp
from jax import lax
from jax.experimental import pallas as pl
from jax.experimental.pallas import tpu as pltpu
from jax.scipy.linalg import block_diag

f32 = jnp.float32
bf16 = jnp.bfloat16

D_MODEL = 1024
N_BRANCH = 4
BR_W = D_MODEL // 2
LRU_HEADS = 8
LRU_CONV = 4
LRU_C = 8.0
RWKV_HD = 64
RWKV_HEADS = BR_W // RWKV_HD
RWKV_RANK = D_MODEL // 16
RWKV_SHIFT_W = 3 * BR_W + 2 * RWKV_RANK
RWKV_DECAY_SCALE = 0.606531
RWKV_LNX_EPS = 64e-5
GMLP_CHUNK = 128
GMLP_GROUPS = 8
CONF_K = 31
LN_EPS = 1e-5

OFF_LRU = 0
OFF_RWKV = OFF_LRU + 2 * BR_W
OFF_GMLP = OFF_RWKV + RWKV_SHIFT_W + BR_W
OFF_CONF = OFF_GMLP + 3 * BR_W
OFF_MERGE = OFF_CONF + 3 * BR_W
N_IN = OFF_MERGE + N_BRANCH * D_MODEL

LANES = 128
SUBLANES = 8
VMEM_LIMIT_BYTES = 60 * 1024 * 1024

TIME_TILE = 256
RWKV_CHUNK = 64
CHUNK_SHIFT = RWKV_CHUNK.bit_length() - 1
RWKV_GROUP = 16
CONV_ROWS = 64
PROJ_CHUNK = 256
LRU_HALO = SUBLANES
CONF_HALO = 32

V_LCW, V_LCB, V_LBR, V_LBI, V_LAM = 0, 4, 5, 6, 7
V_W0, V_A0, V_KK, V_KA, V_RK, V_LNXG, V_LNXB = 8, 9, 10, 11, 12, 13, 14
V_GLG, V_GLB, V_CDB, V_CLG, V_CLB, V_GWS0, V_GBS0 = 15, 16, 17, 18, 19, 20, 21
V_CDW = 24
V_ROWS = 56
M_BOUT, M_LNG, M_LNB = 0, 1, 2


def _sigmoid(x):
    return 0.5 * jnp.tanh(0.5 * x) + 0.5


def _silu(x):
    hx = 0.5 * x
    return hx * jnp.tanh(hx) + hx


def _merge_gate_times(p_half, y_half):
    return (jnp.tanh(p_half) + 1.0) * y_half


def _softplus(z):
    return jnp.maximum(z, 0.0) + jnp.log1p(jnp.exp(-jnp.abs(z)))


def _layernorm(x, g, b, eps=LN_EPS):
    mu = jnp.mean(x, axis=-1, keepdims=True)
    xc = x - mu
    var = jnp.mean(xc * xc, axis=-1, keepdims=True)
    return xc * lax.rsqrt(var + eps) * g + b


def _bdot(a, b):
    return jnp.dot(a.astype(bf16), b.astype(bf16), preferred_element_type=f32)


def _bdot_nt(a, b):
    return lax.dot_general(a.astype(bf16), b.astype(bf16), (((1,), (1,)), ((), ())),
                           preferred_element_type=f32)


def _bdot_tn(a, b):
    return lax.dot_general(a.astype(bf16), b.astype(bf16), (((0,), (0,)), ((), ())),
                           preferred_element_type=f32)


def _segsum(x, seg_ref):
    hi = x.astype(bf16)
    lo = (x - hi.astype(f32)).astype(bf16)
    seg = seg_ref[...]
    return (jnp.dot(hi, seg, preferred_element_type=f32)
            + jnp.dot(lo, seg, preferred_element_type=f32))


def _lru_gates(xc, rg, vec):
    r = _sigmoid(rg[:, :BR_W] + vec(V_LBR))
    ig = _sigmoid(rg[:, BR_W:] + vec(V_LBI))
    log_a = -LRU_C * r * _softplus(-vec(V_LAM))
    a = jnp.exp(log_a)
    mult = jnp.sqrt(-jnp.tanh(log_a) * (a * a + 1.0))
    return a, mult * (ig * xc)


def _rwkv_prep(xs, vec, wlora_ref, seg_ref):
    r = xs[:, 0:BR_W]
    k = xs[:, BR_W:2 * BR_W]
    v = xs[:, 2 * BR_W:3 * BR_W]
    dwa = xs[:, 3 * BR_W:3 * BR_W + 2 * RWKV_RANK]
    lane = lax.broadcasted_iota(jnp.int32, dwa.shape, 1)
    lora_in = jnp.where(lane < RWKV_RANK, jnp.tanh(dwa), dwa)
    wa = jnp.dot(lora_in.astype(bf16), wlora_ref[...], preferred_element_type=f32)
    logw = -RWKV_DECAY_SCALE * _sigmoid(vec(V_W0) + wa[:, :BR_W])
    a = _sigmoid(vec(V_A0) + wa[:, BR_W:])
    kk = k * vec(V_KK)
    kkn = kk / jnp.maximum(jnp.sqrt(_segsum(kk * kk, seg_ref)), 1e-12)
    k2 = k * (1.0 + (a - 1.0) * vec(V_KA))
    bonus = _segsum(r * k2 * vec(V_RK), seg_ref) * v
    return r, k2, v, logw, a, kkn, bonus


def _rwkv_post(o, bonus, vec, seg_ref):
    inv_n = 1.0 / RWKV_HD
    mean = _segsum(o, seg_ref) * inv_n
    oc = o - mean
    var = _segsum(oc * oc, seg_ref) * inv_n
    return oc * lax.rsqrt(var + RWKV_LNX_EPS) * vec(V_LNXG) + vec(V_LNXB) + bonus


def _scan_rows(a, u, h0):
    n = a.shape[0]
    in_group = lax.broadcasted_iota(jnp.int32, a.shape, 0) & (SUBLANES - 1)
    d = 1
    while d < SUBLANES:
        keep = in_group >= d
        a_s = jnp.where(keep, pltpu.roll(a, d, axis=0), 1.0)
        u_s = jnp.where(keep, pltpu.roll(u, d, axis=0), 0.0)
        u = u + a * u_s
        a = a * a_s
        d *= 2
    carry = h0
    groups = []
    for g in range(n // SUBLANES):
        rs = slice(g * SUBLANES, (g + 1) * SUBLANES)
        hg = u[rs] + a[rs] * carry
        groups.append(hg)
        carry = hg[SUBLANES - 1:SUBLANES]
    return jnp.concatenate(groups, axis=0)


def _cond_kernel(c_ref, w_ref, b_ref, o_ref):
    o_ref[0] = _bdot(_silu(c_ref[...]), w_ref[0]) + b_ref[0]


def _cond_call(c_all, w_cond, b_cond):
    depth = w_cond.shape[0]
    n = c_all.shape[0]
    return pl.pallas_call(
        _cond_kernel,
        grid=(depth,),
        in_specs=[pl.BlockSpec((n, D_MODEL), lambda l: (0, 0)),
                  pl.BlockSpec((1, D_MODEL, 3 * D_MODEL), lambda l: (l, 0, 0)),
                  pl.BlockSpec((1, 1, 3 * D_MODEL), lambda l: (l, 0, 0))],
        out_specs=pl.BlockSpec((1, n, 3 * D_MODEL), lambda l: (l, 0, 0)),
        out_shape=jax.ShapeDtypeStruct((depth, n, 3 * D_MODEL), f32),
        compiler_params=pltpu.CompilerParams(dimension_semantics=("arbitrary",),
                                             vmem_limit_bytes=VMEM_LIMIT_BYTES),
        name="cond",
    )(c_all, w_cond, b_cond.reshape(depth, 1, 3 * D_MODEL))


def _prompt_kernel(x_ref, mod_ref, win_ref, wlru_ref, wlora_ref, wg_ref, bsx_ref, seg_ref, ltri_ref,
                   wbr_ref, wout_ref, v512_ref, mu_ref, v1024_ref,
                   y_ref, lconv_ref, lh_ref, rshift_ref, rs_ref, cconv_ref,
                   h_ref, lext_ref, lhc_ref, rprev_ref, s_ref, cext_ref, ycv_ref, ob_ref, rw_ref,
                   ecc_ref, o_ref, rhat_ref, mp_ref, nn_ref, shc_ref, *, alpha):
    T = TIME_TILE
    C = RWKV_CHUNK
    npair = RWKV_HEADS // 2
    first = pl.program_id(1) == 0

    def vec(row):
        return v512_ref[row:row + 1, :]

    def carried(x):
        return jnp.where(first, jnp.zeros_like(x), x)

    lext_ref[0:LRU_HALO, :] = carried(lext_ref[0:LRU_HALO, :])
    cext_ref[0:CONF_HALO, :] = carried(cext_ref[0:CONF_HALO, :])
    for q in range(npair):
        s_ref[q] = carried(s_ref[q])

    shift = mod_ref[0, 0:1, :]
    scale = mod_ref[0, 1:2, :]
    gate = mod_ref[0, 2:3, :]
    h_ref[...] = (x_ref[0] * (1.0 + scale) + shift).astype(bf16)

    def proj(off, n):
        return jnp.dot(h_ref[...], win_ref[:, off:off + n], preferred_element_type=f32)

    pa = proj(OFF_LRU, 2 * BR_W)
    lext_ref[LRU_HALO:LRU_HALO + T, :] = pa[:, :BR_W]
    xc = vec(V_LCB)
    for j in range(LRU_CONV):
        xc = xc + vec(V_LCW + j) * lext_ref[pl.ds(LRU_HALO - (LRU_CONV - 1) + j, T), :]
    lext_ref[0:LRU_HALO, :] = lext_ref[T:T + LRU_HALO, :]
    rg = jnp.dot(xc.astype(bf16), wlru_ref[...], preferred_element_type=f32)

    pending = [(key, off + o, min(PROJ_CHUNK, n - o))
               for key, off, n in (("b", OFF_RWKV, RWKV_SHIFT_W + BR_W), ("d", OFF_CONF, 3 * BR_W),
                                   ("c", OFF_GMLP, 3 * BR_W))
               for o in range(0, n, PROJ_CHUNK)]
    chunks = {"b": [], "d": [], "c": []}

    def pump(count):
        for _ in range(count):
            if pending:
                key, off, width = pending.pop(0)
                chunks[key].append(proj(off, width))

    pump(-(-(RWKV_SHIFT_W + BR_W) // PROJ_CHUNK))
    pb = jnp.concatenate(chunks["b"], axis=1)
    per_stage = -(-len(pending) // 4)

    a, u = _lru_gates(xc, rg, vec)
    hfull = _scan_rows(a, u, carried(lhc_ref[0:1, :]))
    lhc_ref[0:1, :] = hfull[T - 1:T, :]
    ob_ref[0] = (hfull * _silu(pa[:, BR_W:])).astype(bf16)

    p = pb[:, :RWKV_SHIFT_W]
    rows = lax.broadcasted_iota(jnp.int32, p.shape, 0)
    prev = jnp.where(rows == 0, carried(rprev_ref[0:1, :]), pltpu.roll(p, 1, axis=0))
    rprev_ref[0:1, :] = p[T - 1:T, :]
    xs = p + (prev - p) * mu_ref[...]
    r = xs[:, 0:BR_W]
    k = xs[:, BR_W:2 * BR_W]
    v = xs[:, 2 * BR_W:3 * BR_W]
    dwa = xs[:, 3 * BR_W:3 * BR_W + 2 * RWKV_RANK]
    lora_in = jnp.where(lax.broadcasted_iota(jnp.int32, dwa.shape, 1) < RWKV_RANK, jnp.tanh(dwa), dwa)
    kk = k * vec(V_KK)
    pump(per_stage)
    wa = jnp.dot(lora_in.astype(bf16), wlora_ref[...], preferred_element_type=f32)
    logw = -RWKV_DECAY_SCALE * _sigmoid(vec(V_W0) + wa[:, :BR_W])
    ag = _sigmoid(vec(V_A0) + wa[:, BR_W:])
    k2 = k * (1.0 + (ag - 1.0) * vec(V_KA))
    pump(per_stage)
    kkn = kk / jnp.maximum(jnp.sqrt(_segsum(kk * kk, seg_ref)), 1e-12)
    bonus = _segsum(r * k2 * vec(V_RK), seg_ref) * v
    lw1 = logw.astype(bf16)
    rem = logw - lw1.astype(f32)
    lw2 = rem.astype(bf16)
    lw3 = (rem - lw2.astype(f32)).astype(bf16)
    pump(per_stage)
    ltri = ltri_ref[...]
    c = (jnp.dot(ltri, lw1, preferred_element_type=f32) + jnp.dot(ltri, lw2, preferred_element_type=f32)
         + jnp.dot(ltri, lw3, preferred_element_type=f32))
    cend = jnp.concatenate(
        [jnp.broadcast_to(c[(n + 1) * C - 1:(n + 1) * C, :], (C, BR_W)) for n in range(T // C)], axis=0)
    for n in range(T // C):
        ecc_ref[n * SUBLANES:(n + 1) * SUBLANES, :] = jnp.broadcast_to(
            jnp.exp(c[(n + 1) * C - 1:(n + 1) * C, :]), (SUBLANES, BR_W))
    e_neg = jnp.exp(-c)
    e_end = jnp.exp(cend - c)
    beta = kkn * ag
    rw_ref[0] = (-kkn * jnp.exp(c - logw)).astype(bf16)
    rw_ref[1] = (r * jnp.exp(c)).astype(bf16)
    rw_ref[2] = (beta * e_neg).astype(bf16)
    rw_ref[3] = (k2 * e_neg).astype(bf16)
    rw_ref[4] = v.astype(bf16)
    rw_ref[5] = (beta * e_end).astype(bf16)
    rw_ref[6] = (k2 * e_end).astype(bf16)
    pump(len(pending))
    pd = jnp.concatenate(chunks["d"], axis=1)
    pc = jnp.concatenate(chunks["c"], axis=1)

    R = lax.broadcasted_iota(jnp.int32, (2 * C, LANES), 0)
    Cc = lax.broadcasted_iota(jnp.int32, (2 * C, LANES), 1)
    t_idx = R & (C - 1)
    s_idx = Cc & (C - 1)
    lower = s_idx < t_idx
    mask_a = jnp.where(R < C, lower.astype(f32), (s_idx <= t_idx).astype(f32)).astype(bf16)
    block_mask = ((R >> CHUNK_SHIFT) == (Cc >> CHUNK_SHIFT)).astype(f32)
    eye = (R == Cc).astype(f32)
    lvl0_mask = (((R >> 1) == (Cc >> 1)) & lower).astype(f32)
    lvl_masks = []
    sh = 1
    while (1 << sh) < C:
        lvl_masks.append((((R >> (sh + 1)) == (Cc >> (sh + 1))) & ((R >> sh) != (Cc >> sh)) & lower)
                         .astype(f32).astype(bf16))
        sh += 1
    left64 = lax.broadcasted_iota(jnp.int32, (C, LANES), 1) < C
    left128 = Cc < C

    def sel_l(x, m):
        return jnp.where(m, x, jnp.zeros_like(x))

    def sel_r(x, m):
        return jnp.where(m, jnp.zeros_like(x), x)

    left256 = (lax.broadcasted_iota(jnp.int32, (C, 2 * LANES), 1) & (LANES - 1)) < C

    def split_lr(x, m):
        return jnp.concatenate([sel_l(x, m), sel_r(x, m)], axis=0)

    cext_ref[CONF_HALO:CONF_HALO + T, :] = pd[:, :BR_W] * _sigmoid(pd[:, BR_W:2 * BR_W])
    conf_gate = _silu(pd[:, 2 * BR_W:])

    vn = _layernorm(pc[:, BR_W:2 * BR_W], vec(V_GLG), vec(V_GLB))
    left_g = lax.broadcasted_iota(jnp.int32, (GMLP_CHUNK, LANES), 1) < (LANES // 2)
    z_rows = []
    for n in range(T // GMLP_CHUNK):
        z_cols = []
        for q in range(GMLP_GROUPS // 2):
            vq = vn[n * GMLP_CHUNK:(n + 1) * GMLP_CHUNK, q * LANES:(q + 1) * LANES].astype(bf16)
            rhs = jnp.concatenate([sel_l(vq, left_g), sel_r(vq, left_g)], axis=0)
            z_cols.append(jnp.dot(wg_ref[q], rhs, preferred_element_type=f32))
        z_rows.append(jnp.concatenate(z_cols, axis=1) + bsx_ref[...])
    z = jnp.concatenate(z_rows, axis=0)
    ob_ref[2] = (pc[:, :BR_W] * z * _silu(pc[:, 2 * BR_W:])).astype(bf16)

    items = [(n, q) for n in range(T // C) for q in range(npair)]
    for g0 in range(0, len(items), RWKV_GROUP):
        grp = items[g0:g0 + RWKV_GROUP]

        def ld(kind):
            return [rw_ref[kind, n * C:(n + 1) * C, q * LANES:(q + 1) * LANES] for n, q in grp]

        at, rt, bt, kt, vv, b_end, k_end = [ld(kind) for kind in range(7)]
        ar = [jnp.concatenate([a_, r_], axis=0) for a_, r_ in zip(at, rt)]
        out_e = [_bdot_nt(sel_l(x, left128), jnp.concatenate([b_, k_], axis=0)) for x, b_, k_ in zip(ar, bt, kt)]
        out_o = [_bdot_nt(sel_r(x, left128), jnp.concatenate([k_, b_], axis=0)) for x, b_, k_ in zip(ar, bt, kt)]
        tm = [eye + jnp.concatenate([sel_l(e_[0:C], left64), sel_r(o_[0:C], left64)], axis=0) * lvl0_mask
              for e_, o_ in zip(out_e, out_o)]
        out_e = [x.astype(bf16) * mask_a for x in out_e]
        out_o = [x.astype(bf16) * mask_a for x in out_o]
        a_pair = [jnp.concatenate([sel_l(e_[0:C], left64), sel_r(o_[0:C], left64)], axis=0)
                  for e_, o_ in zip(out_e, out_o)]
        lhs_ak = [jnp.where(left64, o_[0:C], e_[0:C]) for e_, o_ in zip(out_e, out_o)]
        lhs_rb = [jnp.where(left64, e_[C:2 * C], o_[C:2 * C]) for e_, o_ in zip(out_e, out_o)]
        lhs_rk = [jnp.where(left64, o_[C:2 * C], e_[C:2 * C]) for e_, o_ in zip(out_e, out_o)]
        v_rl = [jnp.concatenate([sel_r(x, left64), sel_l(x, left64)], axis=0) for x in vv]
        av = [jnp.dot(l_, x, preferred_element_type=f32).astype(bf16) for l_, x in zip(lhs_ak, v_rl)]
        ark = [jnp.dot(l_, x, preferred_element_type=f32) for l_, x in zip(lhs_rk, v_rl)]
        for lm in lvl_masks:
            w = [_bdot(x * lm, t_) for x, t_ in zip(a_pair, tm)]
            tm = [t_ + _bdot(t_, w_) for t_, w_ in zip(tm, w)]
        t_row = [(t_[0:C] + t_[C:2 * C]).astype(bf16) for t_ in tm]
        tx = [jnp.dot(t_, split_lr(jnp.concatenate([a_, x], axis=1), left256),
                      preferred_element_type=f32).astype(bf16)
              for t_, a_, x in zip(t_row, at, av)]
        arb = [jnp.dot(l_, split_lr(x, left256), preferred_element_type=f32) for l_, x in zip(lhs_rb, tx)]
        mp = [_bdot_tn(x[:, 0:LANES], b_) * block_mask for x, b_ in zip(tx, b_end)]
        nn = [_bdot_tn(jnp.concatenate([x[:, LANES:2 * LANES], v_], axis=0),
                       jnp.concatenate([b_, k_], axis=0)) * block_mask
              for x, v_, b_, k_ in zip(tx, vv, b_end, k_end)]
        for idx, (n, q) in enumerate(grp):
            rs = slice(n * C, (n + 1) * C)
            ls = slice(q * LANES, (q + 1) * LANES)
            rhat_ref[rs, ls] = (rt[idx].astype(f32) + arb[idx][:, 0:LANES]).astype(bf16)
            o_ref[rs, ls] = arb[idx][:, LANES:2 * LANES] + ark[idx]
            mp_ref[n * npair + q] = mp[idx].astype(bf16)
            nn_ref[n * npair + q] = nn[idx]

    for n in range(T // C):
        rs = slice(n * C, (n + 1) * C)
        sp = [s_ref[q] for q in range(npair)]
        sb = [x.astype(bf16) for x in sp]
        o_blk = [_bdot_nt(rhat_ref[rs, q * LANES:(q + 1) * LANES], sb[q]) for q in range(npair)]
        s_new = [jnp.dot(sb[q], mp_ref[n * npair + q], preferred_element_type=f32) for q in range(npair)]
        for q in range(npair):
            ls = slice(q * LANES, (q + 1) * LANES)
            o_ref[rs, ls] = o_ref[rs, ls] + o_blk[q]
            s_ref[q] = sp[q] * ecc_ref[n * SUBLANES:n * SUBLANES + 1, ls] + s_new[q] + nn_ref[n * npair + q]

    ob_ref[1] = (_rwkv_post(o_ref[...], bonus, vec, seg_ref) * _silu(pb[:, RWKV_SHIFT_W:])).astype(bf16)

    def merge_term(nb):
        return _merge_gate_times(proj(OFF_MERGE + nb * D_MODEL, D_MODEL),
                                 jnp.dot(ob_ref[nb], wbr_ref[nb], preferred_element_type=f32))

    m = merge_term(0) + merge_term(1) + merge_term(2)

    base = CONF_HALO - (CONF_K - 1)
    shift_rows = T + CONF_HALO - SUBLANES
    for lb in range(BR_W // LANES):
        ls = slice(lb * LANES, (lb + 1) * LANES)
        for s in range(1, SUBLANES):
            shc_ref[s, 0:shift_rows, :] = cext_ref[pl.ds(s, shift_rows), ls]
        for rb in range(T // CONV_ROWS):
            acc = jnp.broadcast_to(v512_ref[V_CDB:V_CDB + 1, ls], (CONV_ROWS, LANES))
            for j in range(CONF_K):
                q, s = divmod(base + j, SUBLANES)
                row0 = rb * CONV_ROWS + q * SUBLANES
                if s == 0:
                    tap_in = cext_ref[row0:row0 + CONV_ROWS, ls]
                else:
                    tap_in = shc_ref[s, row0:row0 + CONV_ROWS, :]
                acc = acc + v512_ref[V_CDW + j:V_CDW + j + 1, ls] * tap_in
            ycv_ref[rb * CONV_ROWS:(rb + 1) * CONV_ROWS, ls] = acc
    cext_ref[0:CONF_HALO, :] = cext_ref[T:T + CONF_HALO, :]
    ob_ref[3] = (_silu(_layernorm(ycv_ref[...], vec(V_CLG), vec(V_CLB))) * conf_gate).astype(bf16)

    m = m + merge_term(3)
    yv = jnp.dot(m.astype(bf16), wout_ref[...], preferred_element_type=f32) + v1024_ref[M_BOUT:M_BOUT + 1, :]
    xn = alpha * x_ref[0] + gate * yv
    y_ref[0] = _layernorm(xn, v1024_ref[M_LNG:M_LNG + 1, :], v1024_ref[M_LNB:M_LNB + 1, :])

    lconv_ref[0] = lext_ref[LRU_HALO - (LRU_CONV - 1):LRU_HALO, :]
    lh_ref[0] = lhc_ref[0:1, :]
    rshift_ref[0] = rprev_ref[0:1, :]
    cconv_ref[0] = cext_ref[CONF_HALO - (CONF_K - 1):CONF_HALO, :]
    for q in range(npair):
        sp = s_ref[q]
        rs_ref[0, 2 * q] = sp[0:C, 0:C]
        rs_ref[0, 2 * q + 1] = sp[C:2 * C, C:2 * C]


def _resident(shape):
    nd = len(shape)
    return pl.BlockSpec(shape, lambda b, i: (0,) * nd, pipeline_mode=pl.Buffered(1))


def _resident_layer(shape, l):
    nd = len(shape) - 1
    return pl.BlockSpec((None,) + tuple(shape[1:]), lambda b, i: (l,) + (0,) * nd,
                        pipeline_mode=pl.Buffered(1))


def _prompt_call(x, mod3, lw, big, l, alpha):
    nb, seq, _ = x.shape
    T = TIME_TILE
    assert seq % T == 0 and T % GMLP_CHUNK == 0 and T % RWKV_CHUNK == 0
    nt = seq // T
    in_specs = [
        pl.BlockSpec((1, T, D_MODEL), lambda b, i: (b, i, 0)),
        pl.BlockSpec((1, 3, D_MODEL), lambda b, i: (b, 0, 0)),
        _resident_layer(big["win"].shape, l), _resident(lw["wlru"].shape), _resident(lw["wlora"].shape),
        _resident(lw["wg"].shape), _resident(lw["bsx"].shape), _resident(lw["seg"].shape),
        _resident(lw["ltri"].shape), _resident_layer(big["wbr"].shape, l),
        _resident_layer(big["wout"].shape, l),
        _resident(lw["v512"].shape), _resident(lw["mu"].shape), _resident(lw["v1024"].shape),
    ]
    out_shape = (
        jax.ShapeDtypeStruct((nb, seq, D_MODEL), f32),
        jax.ShapeDtypeStruct((nb, LRU_CONV - 1, BR_W), f32),
        jax.ShapeDtypeStruct((nb, 1, BR_W), f32),
        jax.ShapeDtypeStruct((nb, 1, RWKV_SHIFT_W), f32),
        jax.ShapeDtypeStruct((nb, RWKV_HEADS, RWKV_HD, RWKV_HD), f32),
        jax.ShapeDtypeStruct((nb, CONF_K - 1, BR_W), f32),
    )
    out_specs = (
        pl.BlockSpec((1, T, D_MODEL), lambda b, i: (b, i, 0)),
        pl.BlockSpec((1, LRU_CONV - 1, BR_W), lambda b, i: (b, 0, 0)),
        pl.BlockSpec((1, 1, BR_W), lambda b, i: (b, 0, 0)),
        pl.BlockSpec((1, 1, RWKV_SHIFT_W), lambda b, i: (b, 0, 0)),
        pl.BlockSpec((1, RWKV_HEADS, RWKV_HD, RWKV_HD), lambda b, i: (b, 0, 0, 0)),
        pl.BlockSpec((1, CONF_K - 1, BR_W), lambda b, i: (b, 0, 0)),
    )
    scratch = [
        pltpu.VMEM((T, D_MODEL), bf16),
        pltpu.VMEM((T + LRU_HALO, BR_W), f32),
        pltpu.VMEM((SUBLANES, BR_W), f32),
        pltpu.VMEM((SUBLANES, RWKV_SHIFT_W), f32),
        pltpu.VMEM((RWKV_HEADS // 2, 2 * RWKV_HD, 2 * RWKV_HD), f32),
        pltpu.VMEM((T + CONF_HALO, BR_W), f32),
        pltpu.VMEM((T, BR_W), f32),
        pltpu.VMEM((N_BRANCH, T, BR_W), bf16),
        pltpu.VMEM((7, T, BR_W), bf16),
        pltpu.VMEM((T // RWKV_CHUNK * SUBLANES, BR_W), f32),
        pltpu.VMEM((T, BR_W), f32),
        pltpu.VMEM((T, BR_W), bf16),
        pltpu.VMEM((T // RWKV_CHUNK * (RWKV_HEADS // 2), 2 * RWKV_HD, 2 * RWKV_HD), bf16),
        pltpu.VMEM((T // RWKV_CHUNK * (RWKV_HEADS // 2), 2 * RWKV_HD, 2 * RWKV_HD), f32),
        pltpu.VMEM((SUBLANES, T + CONF_HALO, LANES), f32),
    ]
    return pl.pallas_call(
        functools.partial(_prompt_kernel, alpha=alpha),
        grid=(nb, nt),
        in_specs=in_specs,
        out_specs=out_specs,
        out_shape=out_shape,
        scratch_shapes=scratch,
        compiler_params=pltpu.CompilerParams(dimension_semantics=("arbitrary", "arbitrary"),
                                             vmem_limit_bytes=VMEM_LIMIT_BYTES),
        name="prompt_layer",
    )(x, mod3, big["win"], lw["wlru"], lw["wlora"], lw["wg"], lw["bsx"], lw["seg"], lw["ltri"],
      big["wbr"], big["wout"], lw["v512"], lw["mu"], lw["v1024"])


def _sample_proj_kernel(x_ref, mod_ref, w_ref, o_ref):
    shift = mod_ref[:, 0:D_MODEL]
    scale = mod_ref[:, D_MODEL:2 * D_MODEL]
    h = (x_ref[...] * (1.0 + scale) + shift).astype(bf16)
    o_ref[...] = jnp.dot(h, w_ref[...], preferred_element_type=f32)


SAMPLE_PROJ_COLS = 3456


def _sample_proj_call(x, mod, win, l):
    n = x.shape[0]
    assert N_IN % SAMPLE_PROJ_COLS == 0
    return pl.pallas_call(
        _sample_proj_kernel,
        grid=(N_IN // SAMPLE_PROJ_COLS,),
        in_specs=[pl.BlockSpec((n, D_MODEL), lambda j: (0, 0)),
                  pl.BlockSpec((n, 3 * D_MODEL), lambda j: (0, 0)),
                  pl.BlockSpec((None, D_MODEL, SAMPLE_PROJ_COLS), lambda j: (l, 0, j))],
        out_specs=pl.BlockSpec((n, SAMPLE_PROJ_COLS), lambda j: (0, j)),
        out_shape=jax.ShapeDtypeStruct((n, N_IN), f32),
        compiler_params=pltpu.CompilerParams(dimension_semantics=("arbitrary",),
                                             vmem_limit_bytes=VMEM_LIMIT_BYTES),
        name="sample_proj",
    )(x, mod, win)


def _sample_pre_kernel(p_ref, lconv_ref, lh_ref, rshift_ref, cconv_ref,
                       wlru_ref, wlora_ref, seg_ref, v512_ref, mu_ref,
                       ob_ref, aux_ref, vecs_ref, lconv_o, lh_o, rshift_o, cconv_o, gv_o):
    def vec(row):
        return v512_ref[row:row + 1, :]

    xb = p_ref[:, OFF_LRU:OFF_LRU + BR_W]
    xc = vec(V_LCB) + vec(V_LCW + LRU_CONV - 1) * xb
    for j in range(LRU_CONV - 1):
        xc = xc + vec(V_LCW + j) * lconv_ref[j]
    for j in range(LRU_CONV - 2):
        lconv_o[j] = lconv_ref[j + 1]
    lconv_o[LRU_CONV - 2] = xb
    a, u = _lru_gates(xc, jnp.dot(xc.astype(bf16), wlru_ref[...], preferred_element_type=f32), vec)
    hn = a * lh_ref[...] + u
    lh_o[...] = hn
    ob_ref[:, 0:BR_W] = hn * _silu(p_ref[:, OFF_LRU + BR_W:OFF_LRU + 2 * BR_W])

    p = p_ref[:, OFF_RWKV:OFF_RWKV + RWKV_SHIFT_W]
    xs = p + (rshift_ref[...] - p) * mu_ref[...]
    rshift_o[...] = p
    r, k2, v, logw, ag, kkn, bonus = _rwkv_prep(xs, vec, wlora_ref, seg_ref)
    n = RWKV_HD
    for idx, x in enumerate((-kkn, kkn * ag, k2, r, jnp.exp(logw), v)):
        for hd in range(RWKV_HEADS):
            vecs_ref[idx, hd] = x[:, hd * n:(hd + 1) * n].T
    aux_ref[:, 0:BR_W] = bonus
    aux_ref[:, BR_W:2 * BR_W] = _silu(p_ref[:, OFF_RWKV + RWKV_SHIFT_W:OFF_RWKV + RWKV_SHIFT_W + BR_W])

    vn = _layernorm(p_ref[:, OFF_GMLP + BR_W:OFF_GMLP + 2 * BR_W], vec(V_GLG), vec(V_GLB))
    gv_o[...] = vn
    z = vec(V_GWS0) * vn + vec(V_GBS0)
    ob_ref[:, BR_W:2 * BR_W] = (p_ref[:, OFF_GMLP:OFF_GMLP + BR_W] * z
                                * _silu(p_ref[:, OFF_GMLP + 2 * BR_W:OFF_GMLP + 3 * BR_W]))

    glu = p_ref[:, OFF_CONF:OFF_CONF + BR_W] * _sigmoid(p_ref[:, OFF_CONF + BR_W:OFF_CONF + 2 * BR_W])
    y = vec(V_CDB) + vec(V_CDW + CONF_K - 1) * glu
    for j in range(CONF_K - 1):
        y = y + vec(V_CDW + j) * cconv_ref[j]
    for j in range(CONF_K - 2):
        cconv_o[j] = cconv_ref[j + 1]
    cconv_o[CONF_K - 2] = glu
    ob_ref[:, 2 * BR_W:3 * BR_W] = (_silu(_layernorm(y, vec(V_CLG), vec(V_CLB)))
                                    * _silu(p_ref[:, OFF_CONF + 2 * BR_W:OFF_CONF + 3 * BR_W]))


def _sample_rwkv_kernel(s_ref, vecs_ref, s_o, ot_o):
    neg_kk = vecs_ref[0]
    beta = vecs_ref[1]
    k2 = vecs_ref[2]
    r = vecs_ref[3]
    w = vecs_ref[4]
    o_rows = []
    for vi in range(RWKV_HD):
        s0 = s_ref[vi]
        sa = jnp.sum(s0 * neg_kk, axis=0, keepdims=True)
        sn = s0 * w + sa * beta + vecs_ref[5, vi:vi + 1, :] * k2
        s_o[vi] = sn
        o_rows.append(jnp.sum(sn * r, axis=0, keepdims=True))
    ot_o[...] = jnp.concatenate(o_rows, axis=0)


def _sample_pre_call(proj, lconv_t, lh, rshift, cconv_t, lw, l):
    n = proj.shape[0]

    def whole(shape):
        nd = len(shape)
        return pl.BlockSpec(shape, lambda i: (0,) * nd, pipeline_mode=pl.Buffered(1))

    def layer(shape):
        nd = len(shape)
        return pl.BlockSpec((None,) + shape, lambda i: (l,) + (0,) * nd, pipeline_mode=pl.Buffered(1))

    state_shapes = [(LRU_CONV - 1, n, BR_W), (n, BR_W), (n, RWKV_SHIFT_W), (CONF_K - 1, n, BR_W)]
    out_shapes = ([(n, (N_BRANCH - 1) * BR_W), (n, 2 * BR_W), (6, RWKV_HEADS, RWKV_HD, n)]
                  + state_shapes + [(n, BR_W)])
    return pl.pallas_call(
        _sample_pre_kernel,
        grid=(1,),
        in_specs=([pl.BlockSpec((n, OFF_MERGE), lambda i: (0, 0), pipeline_mode=pl.Buffered(1))]
                  + [layer(s) for s in state_shapes]
                  + [whole(lw["wlru"].shape), whole(lw["wlora"].shape), whole(lw["seg"].shape),
                     whole(lw["v512"].shape), whole(lw["mu"].shape)]),
        out_specs=[whole(s) for s in out_shapes],
        out_shape=[jax.ShapeDtypeStruct(s, f32) for s in out_shapes],
        compiler_params=pltpu.CompilerParams(dimension_semantics=("arbitrary",),
                                             vmem_limit_bytes=VMEM_LIMIT_BYTES),
        name="sample_pre",
    )(proj, lconv_t, lh, rshift, cconv_t, lw["wlru"], lw["wlora"], lw["seg"], lw["v512"], lw["mu"])


def _sample_rwkv_call(s_t, vecs, l):
    _, nh, nv, nk, n = s_t.shape
    return pl.pallas_call(
        _sample_rwkv_kernel,
        grid=(nh,),
        in_specs=[pl.BlockSpec((None, None, nv, nk, n), lambda h: (l, h, 0, 0, 0)),
                  pl.BlockSpec((6, None, nk, n), lambda h: (0, h, 0, 0))],
        out_specs=[pl.BlockSpec((None, nv, nk, n), lambda h: (h, 0, 0, 0)),
                   pl.BlockSpec((None, nv, n), lambda h: (h, 0, 0))],
        out_shape=[jax.ShapeDtypeStruct((nh, nv, nk, n), f32), jax.ShapeDtypeStruct((nh, nv, n), f32)],
        compiler_params=pltpu.CompilerParams(dimension_semantics=("arbitrary",),
                                             vmem_limit_bytes=VMEM_LIMIT_BYTES),
        name="sample_rwkv",
    )(s_t, vecs)


def _sample_merge_kernel(x_ref, mod_ref, p_ref, ob_ref, aux_ref, ot_ref, seg_ref, v512_ref, wbr_ref, wout_ref,
                         v1024_ref, y_ref, *, alpha):
    def vec(row):
        return v512_ref[row:row + 1, :]

    o = jnp.concatenate([ot_ref[hd].T for hd in range(RWKV_HEADS)], axis=1)
    ob_rwkv = _rwkv_post(o, aux_ref[:, 0:BR_W], vec, seg_ref) * aux_ref[:, BR_W:2 * BR_W]
    branches = [ob_ref[:, 0:BR_W], ob_rwkv, ob_ref[:, BR_W:2 * BR_W], ob_ref[:, 2 * BR_W:3 * BR_W]]
    m = None
    for nb in range(N_BRANCH):
        off = OFF_MERGE + nb * D_MODEL
        term = _merge_gate_times(p_ref[:, off:off + D_MODEL], _bdot(branches[nb], wbr_ref[nb]))
        m = term if m is None else m + term
    yv = _bdot(m, wout_ref[...]) + v1024_ref[M_BOUT:M_BOUT + 1, :]
    xn = alpha * x_ref[...] + mod_ref[:, 2 * D_MODEL:3 * D_MODEL] * yv
    y_ref[...] = _layernorm(xn, v1024_ref[M_LNG:M_LNG + 1, :], v1024_ref[M_LNB:M_LNB + 1, :])


def _sample_merge_call(x, mod, proj, ob, aux, ot, lw, big, l, alpha):
    n = x.shape[0]
    return pl.pallas_call(
        functools.partial(_sample_merge_kernel, alpha=alpha),
        grid=(1,),
        in_specs=[pl.BlockSpec((n, D_MODEL), lambda i: (0, 0)),
                  pl.BlockSpec((n, 3 * D_MODEL), lambda i: (0, 0)),
                  pl.BlockSpec((n, N_IN), lambda i: (0, 0)),
                  pl.BlockSpec(ob.shape, lambda i: (0, 0)),
                  pl.BlockSpec(aux.shape, lambda i: (0, 0)),
                  pl.BlockSpec(ot.shape, lambda i: (0, 0, 0)),
                  pl.BlockSpec(lw["seg"].shape, lambda i: (0, 0)),
                  pl.BlockSpec(lw["v512"].shape, lambda i: (0, 0)),
                  pl.BlockSpec((None,) + big["wbr"].shape[1:], lambda i: (l, 0, 0, 0)),
                  pl.BlockSpec((None,) + big["wout"].shape[1:], lambda i: (l, 0, 0)),
                  pl.BlockSpec(lw["v1024"].shape, lambda i: (0, 0))],
        out_specs=pl.BlockSpec((n, D_MODEL), lambda i: (0, 0)),
        out_shape=jax.ShapeDtypeStruct((n, D_MODEL), f32),
        compiler_params=pltpu.CompilerParams(dimension_semantics=("arbitrary",),
                                             vmem_limit_bytes=VMEM_LIMIT_BYTES),
        name="sample_merge",
    )(x, mod, proj, ob, aux, ot, lw["seg"], lw["v512"], big["wbr"], big["wout"], lw["v1024"])


def _pack_layer(l, lru_conv_w, lru_conv_b, lru_wr, lru_br, lru_wi, lru_bi, lru_lambda, rwkv_mu,
                rwkv_w0, rwkv_ww, rwkv_a0, rwkv_wa, rwkv_kk, rwkv_ka, rwkv_rk, rwkv_lnx_g, rwkv_lnx_b,
                gmlp_ln_g, gmlp_ln_b, gmlp_ws, gmlp_bs, conf_dw_w, conf_dw_b, conf_ln_g, conf_ln_b,
                b_out, ln_g, ln_b):
    rep = BR_W // GMLP_GROUPS
    wr = block_diag(*[lru_wr[l, h] for h in range(LRU_HEADS)])
    wi = block_diag(*[lru_wi[l, h] for h in range(LRU_HEADS)])
    zero_lora = jnp.zeros((RWKV_RANK, BR_W), f32)
    wlora = jnp.concatenate([jnp.concatenate([rwkv_ww[l], zero_lora], axis=1),
                             jnp.concatenate([zero_lora, rwkv_wa[l]], axis=1)], axis=0)
    tril = jnp.tril(jnp.ones((GMLP_CHUNK, GMLP_CHUNK), dtype=bool))
    wm = jnp.where(tril[None], gmlp_ws[l], 0.0)
    wg = jnp.stack([jnp.concatenate([wm[2 * q], wm[2 * q + 1]], axis=1) for q in range(GMLP_GROUPS // 2)])
    rows = [lru_conv_w[l], lru_conv_b[l][None], lru_br[l][None], lru_bi[l][None], lru_lambda[l][None],
            rwkv_w0[l][None], rwkv_a0[l][None], rwkv_kk[l][None], rwkv_ka[l][None],
            rwkv_rk[l].reshape(1, BR_W), rwkv_lnx_g[l][None], rwkv_lnx_b[l][None],
            gmlp_ln_g[l][None], gmlp_ln_b[l][None], conf_dw_b[l][None], conf_ln_g[l][None],
            conf_ln_b[l][None], jnp.repeat(gmlp_ws[l, :, 0, 0], rep)[None],
            jnp.repeat(gmlp_bs[l, :, 0], rep)[None], jnp.zeros((V_CDW - V_GBS0 - 1, BR_W), f32),
            conf_dw_w[l], jnp.zeros((V_ROWS - V_CDW - CONF_K, BR_W), f32)]
    v1024 = jnp.concatenate([b_out[l][None], ln_g[l][None], ln_b[l][None],
                             jnp.zeros((SUBLANES - 3, D_MODEL), f32)], axis=0)
    nchunk = TIME_TILE // RWKV_CHUNK
    return dict(
        wlru=jnp.concatenate([wr, wi], axis=1).astype(bf16),
        wlora=wlora.astype(bf16),
        wg=wg.astype(bf16),
        bsx=jnp.repeat(gmlp_bs[l].T, rep, axis=1),
        seg=jnp.kron(jnp.eye(RWKV_HEADS, dtype=f32), jnp.ones((RWKV_HD, RWKV_HD), f32)).astype(bf16),
        ltri=jnp.kron(jnp.eye(nchunk, dtype=f32), jnp.tril(jnp.ones((RWKV_CHUNK, RWKV_CHUNK), f32))).astype(bf16),
        v512=jnp.concatenate(rows, axis=0),
        mu=rwkv_mu[l][None],
        v1024=v1024,
    )


def kernel(x_prompt, x_sample, state_lru_conv, state_lru_h, state_rwkv_shift, state_rwkv_S, state_conf_conv, c_prompt, c_sample, w_cond, b_cond, w_in, lru_conv_w, lru_conv_b, lru_wr, lru_br, lru_wi, lru_bi, lru_lambda, rwkv_mu, rwkv_w0, rwkv_ww, rwkv_a0, rwkv_wa, rwkv_kk, rwkv_ka, rwkv_rk, rwkv_lnx_g, rwkv_lnx_b, gmlp_ln_g, gmlp_ln_b, gmlp_ws, gmlp_bs, conf_dw_w, conf_dw_b, conf_ln_g, conf_ln_b, w_branch, w_out, b_out, ln_g, ln_b):
    depth = w_in.shape[0]
    alpha = (2.0 * depth) ** 0.25
    nb = x_prompt.shape[0]
    ns = x_sample.shape[0]

    mod = _cond_call(jnp.concatenate([c_prompt, c_sample], axis=0), w_cond, b_cond)
    col = lax.broadcasted_iota(jnp.int32, (1, 1, N_IN), 2)
    big = dict(win=(w_in * jnp.where(col >= OFF_MERGE, 0.5, 1.0)).astype(bf16),
               wbr=(0.5 * w_branch).astype(bf16), wout=w_out.astype(bf16))
    lconv_t = jnp.transpose(state_lru_conv, (0, 2, 1, 3))
    cconv_t = jnp.transpose(state_conf_conv, (0, 2, 1, 3))
    s_t = jnp.transpose(state_rwkv_S, (0, 2, 3, 4, 1))
    xp = x_prompt
    xs = x_sample.reshape(ns, D_MODEL)
    outs_p, outs_s = [], []
    for l in range(depth):
        lw = _pack_layer(l, lru_conv_w, lru_conv_b, lru_wr, lru_br, lru_wi, lru_bi, lru_lambda, rwkv_mu,
                         rwkv_w0, rwkv_ww, rwkv_a0, rwkv_wa, rwkv_kk, rwkv_ka, rwkv_rk, rwkv_lnx_g,
                         rwkv_lnx_b, gmlp_ln_g, gmlp_ln_b, gmlp_ws, gmlp_bs, conf_dw_w, conf_dw_b,
                         conf_ln_g, conf_ln_b, b_out, ln_g, ln_b)
        mod_p = mod[l, :nb].reshape(nb, 3, D_MODEL)
        mod_s = mod[l, nb:]
        xp, lconv_p, lh_p, rshift_p, rs_p, cconv_p = _prompt_call(xp, mod_p, lw, big, l, alpha)
        outs_p.append((lconv_p, lh_p.reshape(nb, BR_W), rshift_p.reshape(nb, RWKV_SHIFT_W), rs_p, cconv_p))

        proj_s = _sample_proj_call(xs, mod_s, big["win"], l)
        ob, aux, vecs, lconv_s, lh_s, rshift_s, cconv_s, gv_s = _sample_pre_call(
            proj_s, lconv_t, state_lru_h, state_rwkv_shift, cconv_t, lw, l)
        rs_s, ot = _sample_rwkv_call(s_t, vecs, l)
        xs = _sample_merge_call(xs, mod_s, proj_s, ob, aux, ot, lw, big, l, alpha)
        outs_s.append((lconv_s, lh_s, rshift_s, rs_s, cconv_s, gv_s.reshape(ns, 1, BR_W)))

    def stk(outs, j):
        return jnp.stack([o[j] for o in outs])

    return (xp, xs.reshape(ns, 1, D_MODEL),
            stk(outs_p, 0), jnp.transpose(stk(outs_s, 0), (0, 2, 1, 3)),
            stk(outs_p, 1), stk(outs_s, 1),
            stk(outs_p, 2), stk(outs_s, 2),
            stk(outs_p, 3), jnp.transpose(stk(outs_s, 3), (0, 4, 1, 2, 3)),
            stk(outs_p, 4), jnp.transpose(stk(outs_s, 4), (0, 2, 1, 3)),
            stk(outs_s, 5))
```

```python
import functools

import jax
import jax.numpy as jnp
from jax import lax
from jax.experimental import pallas as pl
from jax.experimental.pallas import tpu as pltpu
from jax.scipy.linalg import block_diag

f32 = jnp.float32
bf16 = jnp.bfloat16

D_MODEL = 1024
N_BRANCH = 4
BR_W = D_MODEL // 2
LRU_HEADS = 8
LRU_CONV = 4
LRU_C = 8.0
RWKV_HD = 64
RWKV_HEADS = BR_W // RWKV_HD
RWKV_RANK = D_MODEL // 16
RWKV_SHIFT_W = 3 * BR_W + 2 * RWKV_RANK
RWKV_DECAY_SCALE = 0.606531
RWKV_LNX_EPS = 64e-5
GMLP_CHUNK = 128
GMLP_GROUPS = 8
CONF_K = 31
LN_EPS = 1e-5

OFF_LRU = 0
OFF_RWKV = OFF_LRU + 2 * BR_W
OFF_GMLP = OFF_RWKV + RWKV_SHIFT_W + BR_W
OFF_CONF = OFF_GMLP + 3 * BR_W
OFF_MERGE = OFF_CONF + 3 * BR_W
N_IN = OFF_MERGE + N_BRANCH * D_MODEL

LANES = 128
SUBLANES = 8
VMEM_LIMIT_BYTES = 60 * 1024 * 1024

TIME_TILE = 256
RWKV_CHUNK = 64
CHUNK_SHIFT = RWKV_CHUNK.bit_length() - 1
RWKV_GROUP = 16
CONV_ROWS = 64
PROJ_CHUNK = 256
LRU_HALO = SUBLANES
CONF_HALO = 32

V_LCW, V_LCB, V_LBR, V_LBI, V_LAM = 0, 4, 5, 6, 7
V_W0, V_A0, V_KK, V_KA, V_RK, V_LNXG, V_LNXB = 8, 9, 10, 11, 12, 13, 14
V_GLG, V_GLB, V_CDB, V_CLG, V_CLB, V_GWS0, V_GBS0 = 15, 16, 17, 18, 19, 20, 21
V_CDW = 24
V_ROWS = 56
M_BOUT, M_LNG, M_LNB = 0, 1, 2


def _sigmoid(x):
    return 0.5 * jnp.tanh(0.5 * x) + 0.5


def _silu(x):
    hx = 0.5 * x
    return hx * jnp.tanh(hx) + hx


def _merge_gate_times(p_half, y_half):
    return (jnp.tanh(p_half) + 1.0) * y_half


def _softplus(z):
    return jnp.maximum(z, 0.0) + jnp.log1p(jnp.exp(-jnp.abs(z)))


def _layernorm(x, g, b, eps=LN_EPS):
    mu = jnp.mean(x, axis=-1, keepdims=True)
    xc = x - mu
    var = jnp.mean(xc * xc, axis=-1, keepdims=True)
    return xc * lax.rsqrt(var + eps) * g + b


def _bdot(a, b):
    return jnp.dot(a.astype(bf16), b.astype(bf16), preferred_element_type=f32)


def _bdot_nt(a, b):
    return lax.dot_general(a.astype(bf16), b.astype(bf16), (((1,), (1,)), ((), ())),
                           preferred_element_type=f32)


def _bdot_tn(a, b):
    return lax.dot_general(a.astype(bf16), b.astype(bf16), (((0,), (0,)), ((), ())),
                           preferred_element_type=f32)


def _segsum(x, seg_ref):
    return jnp.dot(x.astype(bf16), seg_ref[...], preferred_element_type=f32)


def _lru_gates(xc, rg, vec):
    r = _sigmoid(rg[:, :BR_W] + vec(V_LBR))
    ig = _sigmoid(rg[:, BR_W:] + vec(V_LBI))
    log_a = -LRU_C * r * _softplus(-vec(V_LAM))
    a = jnp.exp(log_a)
    mult = jnp.sqrt(-jnp.tanh(log_a) * (a * a + 1.0))
    return a, mult * (ig * xc)


def _rwkv_prep(xs, vec, wlora_ref, seg_ref):
    r = xs[:, 0:BR_W]
    k = xs[:, BR_W:2 * BR_W]
    v = xs[:, 2 * BR_W:3 * BR_W]
    dwa = xs[:, 3 * BR_W:3 * BR_W + 2 * RWKV_RANK]
    lane = lax.broadcasted_iota(jnp.int32, dwa.shape, 1)
    lora_in = jnp.where(lane < RWKV_RANK, jnp.tanh(dwa), dwa)
    wa = jnp.dot(lora_in.astype(bf16), wlora_ref[...], preferred_element_type=f32)
    logw = -RWKV_DECAY_SCALE * _sigmoid(vec(V_W0) + wa[:, :BR_W])
    a = _sigmoid(vec(V_A0) + wa[:, BR_W:])
    kk = k * vec(V_KK)
    kkn = kk / jnp.maximum(jnp.sqrt(_segsum(kk * kk, seg_ref)), 1e-12)
    k2 = k * (1.0 + (a - 1.0) * vec(V_KA))
    bonus = _segsum(r * k2 * vec(V_RK), seg_ref) * v
    return r, k2, v, logw, a, kkn, bonus


def _rwkv_post(o, bonus, vec, seg_ref):
    inv_n = 1.0 / RWKV_HD
    mean = _segsum(o, seg_ref) * inv_n
    oc = o - mean
    var = _segsum(oc * oc, seg_ref) * inv_n
    return oc * lax.rsqrt(var + RWKV_LNX_EPS) * vec(V_LNXG) + vec(V_LNXB) + bonus


def _scan_rows(a, u, h0):
    n = a.shape[0]
    in_group = lax.broadcasted_iota(jnp.int32, a.shape, 0) & (SUBLANES - 1)
    d = 1
    while d < SUBLANES:
        keep = in_group >= d
        a_s = jnp.where(keep, pltpu.roll(a, d, axis=0), 1.0)
        u_s = jnp.where(keep, pltpu.roll(u, d, axis=0), 0.0)
        u = u + a * u_s
        a = a * a_s
        d *= 2
    carry = h0
    groups = []
    for g in range(n // SUBLANES):
        rs = slice(g * SUBLANES, (g + 1) * SUBLANES)
        hg = u[rs] + a[rs] * carry
        groups.append(hg)
        carry = hg[SUBLANES - 1:SUBLANES]
    return jnp.concatenate(groups, axis=0)


def _cond_kernel(c_ref, w_ref, b_ref, o_ref):
    o_ref[0] = _bdot(_silu(c_ref[...]), w_ref[0]) + b_ref[0]


def _cond_call(c_all, w_cond, b_cond):
    depth = w_cond.shape[0]
    n = c_all.shape[0]
    return pl.pallas_call(
        _cond_kernel,
        grid=(depth,),
        in_specs=[pl.BlockSpec((n, D_MODEL), lambda l: (0, 0)),
                  pl.BlockSpec((1, D_MODEL, 3 * D_MODEL), lambda l: (l, 0, 0)),
                  pl.BlockSpec((1, 1, 3 * D_MODEL), lambda l: (l, 0, 0))],
        out_specs=pl.BlockSpec((1, n, 3 * D_MODEL), lambda l: (l, 0, 0)),
        out_shape=jax.ShapeDtypeStruct((depth, n, 3 * D_MODEL), f32),
        compiler_params=pltpu.CompilerParams(dimension_semantics=("arbitrary",),
                                             vmem_limit_bytes=VMEM_LIMIT_BYTES),
        name="cond",
    )(c_all, w_cond, b_cond.reshape(depth, 1, 3 * D_MODEL))


def _prompt_kernel(x_ref, mod_ref, win_ref, wlru_ref, wlora_ref, wg_ref, bsx_ref, seg_ref, ltri_ref,
                   wbr_ref, wout_ref, v512_ref, mu_ref, v1024_ref,
                   y_ref, lconv_ref, lh_ref, rshift_ref, rs_ref, cconv_ref,
                   h_ref, lext_ref, lhc_ref, rprev_ref, s_ref, cext_ref, ycv_ref, ob_ref, rw_ref,
                   ecc_ref, o_ref, rhat_ref, mp_ref, nn_ref, shc_ref, *, alpha):
    T = TIME_TILE
    C = RWKV_CHUNK
    npair = RWKV_HEADS // 2
    first = pl.program_id(1) == 0

    def vec(row):
        return v512_ref[row:row + 1, :]

    def carried(x):
        return jnp.where(first, jnp.zeros_like(x), x)

    lext_ref[0:LRU_HALO, :] = carried(lext_ref[0:LRU_HALO, :])
    cext_ref[0:CONF_HALO, :] = carried(cext_ref[0:CONF_HALO, :])
    for q in range(npair):
        s_ref[q] = carried(s_ref[q])

    shift = mod_ref[0, 0:1, :]
    scale = mod_ref[0, 1:2, :]
    gate = mod_ref[0, 2:3, :]
    h_ref[...] = (x_ref[0] * (1.0 + scale) + shift).astype(bf16)

    def proj(off, n):
        return jnp.dot(h_ref[...], win_ref[:, off:off + n], preferred_element_type=f32)

    pa = proj(OFF_LRU, 2 * BR_W)
    lext_ref[LRU_HALO:LRU_HALO + T, :] = pa[:, :BR_W]
    xc = vec(V_LCB)
    for j in range(LRU_CONV):
        xc = xc + vec(V_LCW + j) * lext_ref[pl.ds(LRU_HALO - (LRU_CONV - 1) + j, T), :]
    lext_ref[0:LRU_HALO, :] = lext_ref[T:T + LRU_HALO, :]
    rg = jnp.dot(xc.astype(bf16), wlru_ref[...], preferred_element_type=f32)

    pending = [(key, off + o, min(PROJ_CHUNK, n - o))
               for key, off, n in (("b", OFF_RWKV, RWKV_SHIFT_W + BR_W), ("d", OFF_CONF, 3 * BR_W),
                                   ("c", OFF_GMLP, 3 * BR_W))
               for o in range(0, n, PROJ_CHUNK)]
    chunks = {"b": [], "d": [], "c": []}

    def pump(count):
        for _ in range(count):
            if pending:
                key, off, width = pending.pop(0)
                chunks[key].append(proj(off, width))

    pump(-(-(RWKV_SHIFT_W + BR_W) // PROJ_CHUNK))
    pb = jnp.concatenate(chunks["b"], axis=1)
    per_stage = -(-len(pending) // 4)

    a, u = _lru_gates(xc, rg, vec)
    hfull = _scan_rows(a, u, carried(lhc_ref[0:1, :]))
    lhc_ref[0:1, :] = hfull[T - 1:T, :]
    ob_ref[0] = (hfull * _silu(pa[:, BR_W:])).astype(bf16)

    p = pb[:, :RWKV_SHIFT_W]
    rows = lax.broadcasted_iota(jnp.int32, p.shape, 0)
    prev = jnp.where(rows == 0, carried(rprev_ref[0:1, :]), pltpu.roll(p, 1, axis=0))
    rprev_ref[0:1, :] = p[T - 1:T, :]
    xs = p + (prev - p) * mu_ref[...]
    r = xs[:, 0:BR_W]
    k = xs[:, BR_W:2 * BR_W]
    v = xs[:, 2 * BR_W:3 * BR_W]
    dwa = xs[:, 3 * BR_W:3 * BR_W + 2 * RWKV_RANK]
    lora_in = jnp.where(lax.broadcasted_iota(jnp.int32, dwa.shape, 1) < RWKV_RANK, jnp.tanh(dwa), dwa)
    kk = k * vec(V_KK)
    pump(per_stage)
    wa = jnp.dot(lora_in.astype(bf16), wlora_ref[...], preferred_element_type=f32)
    logw = -RWKV_DECAY_SCALE * _sigmoid(vec(V_W0) + wa[:, :BR_W])
    ag = _sigmoid(vec(V_A0) + wa[:, BR_W:])
    k2 = k * (1.0 + (ag - 1.0) * vec(V_KA))
    pump(per_stage)
    kkn = kk / jnp.maximum(jnp.sqrt(_segsum(kk * kk, seg_ref)), 1e-12)
    bonus = _segsum(r * k2 * vec(V_RK), seg_ref) * v
    lw1 = logw.astype(bf16)
    rem = logw - lw1.astype(f32)
    lw2 = rem.astype(bf16)
    lw3 = (rem - lw2.astype(f32)).astype(bf16)
    pump(per_stage)
    ltri = ltri_ref[...]
    c = (jnp.dot(ltri, lw1, preferred_element_type=f32) + jnp.dot(ltri, lw2, preferred_element_type=f32)
         + jnp.dot(ltri, lw3, preferred_element_type=f32))
    cend = jnp.concatenate(
        [jnp.broadcast_to(c[(n + 1) * C - 1:(n + 1) * C, :], (C, BR_W)) for n in range(T // C)], axis=0)
    for n in range(T // C):
        ecc_ref[n * SUBLANES:(n + 1) * SUBLANES, :] = jnp.broadcast_to(
            jnp.exp(c[(n + 1) * C - 1:(n + 1) * C, :]), (SUBLANES, BR_W))
    e_neg = jnp.exp(-c)
    e_end = jnp.exp(cend - c)
    beta = kkn * ag
    rw_ref[0] = (-kkn * jnp.exp(c - logw)).astype(bf16)
    rw_ref[1] = (r * jnp.exp(c)).astype(bf16)
    rw_ref[2] = (beta * e_neg).astype(bf16)
    rw_ref[3] = (k2 * e_neg).astype(bf16)
    rw_ref[4] = v.astype(bf16)
    rw_ref[5] = (beta * e_end).astype(bf16)
    rw_ref[6] = (k2 * e_end).astype(bf16)
    pump(len(pending))
    pd = jnp.concatenate(chunks["d"], axis=1)
    pc = jnp.concatenate(chunks["c"], axis=1)

    R = lax.broadcasted_iota(jnp.int32, (2 * C, LANES), 0)
    Cc = lax.broadcasted_iota(jnp.int32, (2 * C, LANES), 1)
    t_idx = R & (C - 1)
    s_idx = Cc & (C - 1)
    lower = s_idx < t_idx
    mask_a = jnp.where(R < C, lower.astype(f32), (s_idx <= t_idx).astype(f32)).astype(bf16)
    block_mask = ((R >> CHUNK_SHIFT) == (Cc >> CHUNK_SHIFT)).astype(f32)
    eye = (R == Cc).astype(f32)
    lvl0_mask = (((R >> 1) == (Cc >> 1)) & lower).astype(f32)
    lvl_masks = []
    sh = 1
    while (1 << sh) < C:
        lvl_masks.append((((R >> (sh + 1)) == (Cc >> (sh + 1))) & ((R >> sh) != (Cc >> sh)) & lower)
                         .astype(f32).astype(bf16))
        sh += 1
    left64 = lax.broadcasted_iota(jnp.int32, (C, LANES), 1) < C
    left128 = Cc < C

    def sel_l(x, m):
        return jnp.where(m, x, jnp.zeros_like(x))

    def sel_r(x, m):
        return jnp.where(m, jnp.zeros_like(x), x)

    left256 = (lax.broadcasted_iota(jnp.int32, (C, 2 * LANES), 1) & (LANES - 1)) < C

    def split_lr(x, m):
        return jnp.concatenate([sel_l(x, m), sel_r(x, m)], axis=0)

    cext_ref[CONF_HALO:CONF_HALO + T, :] = pd[:, :BR_W] * _sigmoid(pd[:, BR_W:2 * BR_W])
    conf_gate = _silu(pd[:, 2 * BR_W:])

    vn = _layernorm(pc[:, BR_W:2 * BR_W], vec(V_GLG), vec(V_GLB))
    left_g = lax.broadcasted_iota(jnp.int32, (GMLP_CHUNK, LANES), 1) < (LANES // 2)
    z_rows = []
    for n in range(T // GMLP_CHUNK):
        z_cols = []
        for q in range(GMLP_GROUPS // 2):
            vq = vn[n * GMLP_CHUNK:(n + 1) * GMLP_CHUNK, q * LANES:(q + 1) * LANES].astype(bf16)
            rhs = jnp.concatenate([sel_l(vq, left_g), sel_r(vq, left_g)], axis=0)
            z_cols.append(jnp.dot(wg_ref[q], rhs, preferred_element_type=f32))
        z_rows.append(jnp.concatenate(z_cols, axis=1) + bsx_ref[...])
    z = jnp.concatenate(z_rows, axis=0)
    ob_ref[2] = (pc[:, :BR_W] * z * _silu(pc[:, 2 * BR_W:])).astype(bf16)

    items = [(n, q) for n in range(T // C) for q in range(npair)]
    for g0 in range(0, len(items), RWKV_GROUP):
        grp = items[g0:g0 + RWKV_GROUP]

        def ld(kind):
            return [rw_ref[kind, n * C:(n + 1) * C, q * LANES:(q + 1) * LANES] for n, q in grp]

        at, rt, bt, kt, vv, b_end, k_end = [ld(kind) for kind in range(7)]
        ar = [jnp.concatenate([a_, r_], axis=0) for a_, r_ in zip(at, rt)]
        out_e = [_bdot_nt(sel_l(x, left128), jnp.concatenate([b_, k_], axis=0)) for x, b_, k_ in zip(ar, bt, kt)]
        out_o = [_bdot_nt(sel_r(x, left128), jnp.concatenate([k_, b_], axis=0)) for x, b_, k_ in zip(ar, bt, kt)]
        tm = [eye + jnp.concatenate([sel_l(e_[0:C], left64), sel_r(o_[0:C], left64)], axis=0) * lvl0_mask
              for e_, o_ in zip(out_e, out_o)]
        out_e = [x.astype(bf16) * mask_a for x in out_e]
        out_o = [x.astype(bf16) * mask_a for x in out_o]
        a_pair = [jnp.concatenate([sel_l(e_[0:C], left64), sel_r(o_[0:C], left64)], axis=0)
                  for e_, o_ in zip(out_e, out_o)]
        lhs_ak = [jnp.where(left64, o_[0:C], e_[0:C]) for e_, o_ in zip(out_e, out_o)]
        lhs_rb = [jnp.where(left64, e_[C:2 * C], o_[C:2 * C]) for e_, o_ in zip(out_e, out_o)]
        lhs_rk = [jnp.where(left64, o_[C:2 * C], e_[C:2 * C]) for e_, o_ in zip(out_e, out_o)]
        v_rl = [jnp.concatenate([sel_r(x, left64), sel_l(x, left64)], axis=0) for x in vv]
        av = [jnp.dot(l_, x, preferred_element_type=f32).astype(bf16) for l_, x in zip(lhs_ak, v_rl)]
        ark = [jnp.dot(l_, x, preferred_element_type=f32) for l_, x in zip(lhs_rk, v_rl)]
        for lm in lvl_masks:
            w = [_bdot(x * lm, t_) for x, t_ in zip(a_pair, tm)]
            tm = [t_ + _bdot(t_, w_) for t_, w_ in zip(tm, w)]
        t_row = [(t_[0:C] + t_[C:2 * C]).astype(bf16) for t_ in tm]
        tx = [jnp.dot(t_, split_lr(jnp.concatenate([a_, x], axis=1), left256),
                      preferred_element_type=f32).astype(bf16)
              for t_, a_, x in zip(t_row, at, av)]
        arb = [jnp.dot(l_, split_lr(x, left256), preferred_element_type=f32) for l_, x in zip(lhs_rb, tx)]
        mp = [_bdot_tn(x[:, 0:LANES], b_) * block_mask for x, b_ in zip(tx, b_end)]
        nn = [_bdot_tn(jnp.concatenate([x[:, LANES:2 * LANES], v_], axis=0),
                       jnp.concatenate([b_, k_], axis=0)) * block_mask
              for x, v_, b_, k_ in zip(tx, vv, b_end, k_end)]
        for idx, (n, q) in enumerate(grp):
            rs = slice(n * C, (n + 1) * C)
            ls = slice(q * LANES, (q + 1) * LANES)
            rhat_ref[rs, ls] = (rt[idx].astype(f32) + arb[idx][:, 0:LANES]).astype(bf16)
            o_ref[rs, ls] = arb[idx][:, LANES:2 * LANES] + ark[idx]
            mp_ref[n * npair + q] = mp[idx].astype(bf16)
            nn_ref[n * npair + q] = nn[idx]

    for n in range(T // C):
        rs = slice(n * C, (n + 1) * C)
        sp = [s_ref[q] for q in range(npair)]
        sb = [x.astype(bf16) for x in sp]
        o_blk = [_bdot_nt(rhat_ref[rs, q * LANES:(q + 1) * LANES], sb[q]) for q in range(npair)]
        s_new = [jnp.dot(sb[q], mp_ref[n * npair + q], preferred_element_type=f32) for q in range(npair)]
        for q in range(npair):
            ls = slice(q * LANES, (q + 1) * LANES)
            o_ref[rs, ls] = o_ref[rs, ls] + o_blk[q]
            s_ref[q] = sp[q] * ecc_ref[n * SUBLANES:n * SUBLANES + 1, ls] + s_new[q] + nn_ref[n * npair + q]

    ob_ref[1] = (_rwkv_post(o_ref[...], bonus, vec, seg_ref) * _silu(pb[:, RWKV_SHIFT_W:])).astype(bf16)

    def merge_term(nb):
        return _merge_gate_times(proj(OFF_MERGE + nb * D_MODEL, D_MODEL),
                                 jnp.dot(ob_ref[nb], wbr_ref[nb], preferred_element_type=f32))

    m = merge_term(0) + merge_term(1) + merge_term(2)

    base = CONF_HALO - (CONF_K - 1)
    shift_rows = T + CONF_HALO - SUBLANES
    for lb in range(BR_W // LANES):
        ls = slice(lb * LANES, (lb + 1) * LANES)
        for s in range(1, SUBLANES):
            shc_ref[s, 0:shift_rows, :] = cext_ref[pl.ds(s, shift_rows), ls]
        for rb in range(T // CONV_ROWS):
            acc = jnp.broadcast_to(v512_ref[V_CDB:V_CDB + 1, ls], (CONV_ROWS, LANES))
            for j in range(CONF_K):
                q, s = divmod(base + j, SUBLANES)
                row0 = rb * CONV_ROWS + q * SUBLANES
                if s == 0:
                    tap_in = cext_ref[row0:row0 + CONV_ROWS, ls]
                else:
                    tap_in = shc_ref[s, row0:row0 + CONV_ROWS, :]
                acc = acc + v512_ref[V_CDW + j:V_CDW + j + 1, ls] * tap_in
            ycv_ref[rb * CONV_ROWS:(rb + 1) * CONV_ROWS, ls] = acc
    cext_ref[0:CONF_HALO, :] = cext_ref[T:T + CONF_HALO, :]
    ob_ref[3] = (_silu(_layernorm(ycv_ref[...], vec(V_CLG), vec(V_CLB))) * conf_gate).astype(bf16)

    m = m + merge_term(3)
    yv = jnp.dot(m.astype(bf16), wout_ref[...], preferred_element_type=f32) + v1024_ref[M_BOUT:M_BOUT + 1, :]
    xn = alpha * x_ref[0] + gate * yv
    y_ref[0] = _layernorm(xn, v1024_ref[M_LNG:M_LNG + 1, :], v1024_ref[M_LNB:M_LNB + 1, :])

    lconv_ref[0] = lext_ref[LRU_HALO - (LRU_CONV - 1):LRU_HALO, :]
    lh_ref[0] = lhc_ref[0:1, :]
    rshift_ref[0] = rprev_ref[0:1, :]
    cconv_ref[0] = cext_ref[CONF_HALO - (CONF_K - 1):CONF_HALO, :]
    for q in range(npair):
        sp = s_ref[q]
        rs_ref[0, 2 * q] = sp[0:C, 0:C]
        rs_ref[0, 2 * q + 1] = sp[C:2 * C, C:2 * C]


def _resident(shape):
    nd = len(shape)
    return pl.BlockSpec(shape, lambda b, i: (0,) * nd, pipeline_mode=pl.Buffered(1))


def _resident_layer(shape, l):
    nd = len(shape) - 1
    return pl.BlockSpec((None,) + tuple(shape[1:]), lambda b, i: (l,) + (0,) * nd,
                        pipeline_mode=pl.Buffered(1))


def _prompt_call(x, mod3, lw, big, l, alpha):
    nb, seq, _ = x.shape
    T = TIME_TILE
    assert seq % T == 0 and T % GMLP_CHUNK == 0 and T % RWKV_CHUNK == 0
    nt = seq // T
    in_specs = [
        pl.BlockSpec((1, T, D_MODEL), lambda b, i: (b, i, 0)),
        pl.BlockSpec((1, 3, D_MODEL), lambda b, i: (b, 0, 0)),
        _resident_layer(big["win"].shape, l), _resident(lw["wlru"].shape), _resident(lw["wlora"].shape),
        _resident(lw["wg"].shape), _resident(lw["bsx"].shape), _resident(lw["seg"].shape),
        _resident(lw["ltri"].shape), _resident_layer(big["wbr"].shape, l),
        _resident_layer(big["wout"].shape, l),
        _resident(lw["v512"].shape), _resident(lw["mu"].shape), _resident(lw["v1024"].shape),
    ]
    out_shape = (
        jax.ShapeDtypeStruct((nb, seq, D_MODEL), f32),
        jax.ShapeDtypeStruct((nb, LRU_CONV - 1, BR_W), f32),
        jax.ShapeDtypeStruct((nb, 1, BR_W), f32),
        jax.ShapeDtypeStruct((nb, 1, RWKV_SHIFT_W), f32),
        jax.ShapeDtypeStruct((nb, RWKV_HEADS, RWKV_HD, RWKV_HD), f32),
        jax.ShapeDtypeStruct((nb, CONF_K - 1, BR_W), f32),
    )
    out_specs = (
        pl.BlockSpec((1, T, D_MODEL), lambda b, i: (b, i, 0)),
        pl.BlockSpec((1, LRU_CONV - 1, BR_W), lambda b, i: (b, 0, 0)),
        pl.BlockSpec((1, 1, BR_W), lambda b, i: (b, 0, 0)),
        pl.BlockSpec((1, 1, RWKV_SHIFT_W), lambda b, i: (b, 0, 0)),
        pl.BlockSpec((1, RWKV_HEADS, RWKV_HD, RWKV_HD), lambda b, i: (b, 0, 0, 0)),
        pl.BlockSpec((1, CONF_K - 1, BR_W), lambda b, i: (b, 0, 0)),
    )
    scratch = [
        pltpu.VMEM((T, D_MODEL), bf16),
        pltpu.VMEM((T + LRU_HALO, BR_W), f32),
        pltpu.VMEM((SUBLANES, BR_W), f32),
        pltpu.VMEM((SUBLANES, RWKV_SHIFT_W), f32),
        pltpu.VMEM((RWKV_HEADS // 2, 2 * RWKV_HD, 2 * RWKV_HD), f32),
        pltpu.VMEM((T + CONF_HALO, BR_W), f32),
        pltpu.VMEM((T, BR_W), f32),
        pltpu.VMEM((N_BRANCH, T, BR_W), bf16),
        pltpu.VMEM((7, T, BR_W), bf16),
        pltpu.VMEM((T // RWKV_CHUNK * SUBLANES, BR_W), f32),
        pltpu.VMEM((T, BR_W), f32),
        pltpu.VMEM((T, BR_W), bf16),
        pltpu.VMEM((T // RWKV_CHUNK * (RWKV_HEADS // 2), 2 * RWKV_HD, 2 * RWKV_HD), bf16),
        pltpu.VMEM((T // RWKV_CHUNK * (RWKV_HEADS // 2), 2 * RWKV_HD, 2 * RWKV_HD), f32),
        pltpu.VMEM((SUBLANES, T + CONF_HALO, LANES), f32),
    ]
    return pl.pallas_call(
        functools.partial(_prompt_kernel, alpha=alpha),
        grid=(nb, nt),
        in_specs=in_specs,
        out_specs=out_specs,
        out_shape=out_shape,
        scratch_shapes=scratch,
        compiler_params=pltpu.CompilerParams(dimension_semantics=("arbitrary", "arbitrary"),
                                             vmem_limit_bytes=VMEM_LIMIT_BYTES),
        name="prompt_layer",
    )(x, mod3, big["win"], lw["wlru"], lw["wlora"], lw["wg"], lw["bsx"], lw["seg"], lw["ltri"],
      big["wbr"], big["wout"], lw["v512"], lw["mu"], lw["v1024"])


def _sample_proj_kernel(x_ref, mod_ref, w_ref, o_ref):
    shift = mod_ref[:, 0:D_MODEL]
    scale = mod_ref[:, D_MODEL:2 * D_MODEL]
    h = (x_ref[...] * (1.0 + scale) + shift).astype(bf16)
    o_ref[...] = jnp.dot(h, w_ref[...], preferred_element_type=f32)


SAMPLE_PROJ_COLS = 3456


def _sample_proj_call(x, mod, win, l):
    n = x.shape[0]
    assert N_IN % SAMPLE_PROJ_COLS == 0
    return pl.pallas_call(
        _sample_proj_kernel,
        grid=(N_IN // SAMPLE_PROJ_COLS,),
        in_specs=[pl.BlockSpec((n, D_MODEL), lambda j: (0, 0)),
                  pl.BlockSpec((n, 3 * D_MODEL), lambda j: (0, 0)),
                  pl.BlockSpec((None, D_MODEL, SAMPLE_PROJ_COLS), lambda j: (l, 0, j))],
        out_specs=pl.BlockSpec((n, SAMPLE_PROJ_COLS), lambda j: (0, j)),
        out_shape=jax.ShapeDtypeStruct((n, N_IN), f32),
        compiler_params=pltpu.CompilerParams(dimension_semantics=("arbitrary",),
                                             vmem_limit_bytes=VMEM_LIMIT_BYTES),
        name="sample_proj",
    )(x, mod, win)


def _sample_pre_kernel(p_ref, lconv_ref, lh_ref, rshift_ref, cconv_ref,
                       wlru_ref, wlora_ref, seg_ref, v512_ref, mu_ref,
                       ob_ref, aux_ref, vecs_ref, lconv_o, lh_o, rshift_o, cconv_o, gv_o):
    def vec(row):
        return v512_ref[row:row + 1, :]

    xb = p_ref[:, OFF_LRU:OFF_LRU + BR_W]
    xc = vec(V_LCB) + vec(V_LCW + LRU_CONV - 1) * xb
    for j in range(LRU_CONV - 1):
        xc = xc + vec(V_LCW + j) * lconv_ref[j]
    for j in range(LRU_CONV - 2):
        lconv_o[j] = lconv_ref[j + 1]
    lconv_o[LRU_CONV - 2] = xb
    a, u = _lru_gates(xc, jnp.dot(xc.astype(bf16), wlru_ref[...], preferred_element_type=f32), vec)
    hn = a * lh_ref[...] + u
    lh_o[...] = hn
    ob_ref[:, 0:BR_W] = hn * _silu(p_ref[:, OFF_LRU + BR_W:OFF_LRU + 2 * BR_W])

    p = p_ref[:, OFF_RWKV:OFF_RWKV + RWKV_SHIFT_W]
    xs = p + (rshift_ref[...] - p) * mu_ref[...]
    rshift_o[...] = p
    r, k2, v, logw, ag, kkn, bonus = _rwkv_prep(xs, vec, wlora_ref, seg_ref)
    n = RWKV_HD
    for idx, x in enumerate((-kkn, kkn * ag, k2, r, jnp.exp(logw), v)):
        for hd in range(RWKV_HEADS):
            vecs_ref[idx, hd] = x[:, hd * n:(hd + 1) * n].T
    aux_ref[:, 0:BR_W] = bonus
    aux_ref[:, BR_W:2 * BR_W] = _silu(p_ref[:, OFF_RWKV + RWKV_SHIFT_W:OFF_RWKV + RWKV_SHIFT_W + BR_W])

    vn = _layernorm(p_ref[:, OFF_GMLP + BR_W:OFF_GMLP + 2 * BR_W], vec(V_GLG), vec(V_GLB))
    gv_o[...] = vn
    z = vec(V_GWS0) * vn + vec(V_GBS0)
    ob_ref[:, BR_W:2 * BR_W] = (p_ref[:, OFF_GMLP:OFF_GMLP + BR_W] * z
                                * _silu(p_ref[:, OFF_GMLP + 2 * BR_W:OFF_GMLP + 3 * BR_W]))

    glu = p_ref[:, OFF_CONF:OFF_CONF + BR_W] * _sigmoid(p_ref[:, OFF_CONF + BR_W:OFF_CONF + 2 * BR_W])
    y = vec(V_CDB) + vec(V_CDW + CONF_K - 1) * glu
    for j in range(CONF_K - 1):
        y = y + vec(V_CDW + j) * cconv_ref[j]
    for j in range(CONF_K - 2):
        cconv_o[j] = cconv_ref[j + 1]
    cconv_o[CONF_K - 2] = glu
    ob_ref[:, 2 * BR_W:3 * BR_W] = (_silu(_layernorm(y, vec(V_CLG), vec(V_CLB)))
                                    * _silu(p_ref[:, OFF_CONF + 2 * BR_W:OFF_CONF + 3 * BR_W]))


def _sample_rwkv_kernel(s_ref, vecs_ref, s_o, ot_o):
    neg_kk = vecs_ref[0]
    beta = vecs_ref[1]
    k2 = vecs_ref[2]
    r = vecs_ref[3]
    w = vecs_ref[4]
    o_rows = []
    for vi in range(RWKV_HD):
        s0 = s_ref[vi]
        sa = jnp.sum(s0 * neg_kk, axis=0, keepdims=True)
        sn = s0 * w + sa * beta + vecs_ref[5, vi:vi + 1, :] * k2
        s_o[vi] = sn
        o_rows.append(jnp.sum(sn * r, axis=0, keepdims=True))
    ot_o[...] = jnp.concatenate(o_rows, axis=0)


def _sample_pre_call(proj, lconv_t, lh, rshift, cconv_t, lw, l):
    n = proj.shape[0]

    def whole(shape):
        nd = len(shape)
        return pl.BlockSpec(shape, lambda i: (0,) * nd, pipeline_mode=pl.Buffered(1))

    def layer(shape):
        nd = len(shape)
        return pl.BlockSpec((None,) + shape, lambda i: (l,) + (0,) * nd, pipeline_mode=pl.Buffered(1))

    state_shapes = [(LRU_CONV - 1, n, BR_W), (n, BR_W), (n, RWKV_SHIFT_W), (CONF_K - 1, n, BR_W)]
    out_shapes = ([(n, (N_BRANCH - 1) * BR_W), (n, 2 * BR_W), (6, RWKV_HEADS, RWKV_HD, n)]
                  + state_shapes + [(n, BR_W)])
    return pl.pallas_call(
        _sample_pre_kernel,
        grid=(1,),
        in_specs=([pl.BlockSpec((n, OFF_MERGE), lambda i: (0, 0), pipeline_mode=pl.Buffered(1))]
                  + [layer(s) for s in state_shapes]
                  + [whole(lw["wlru"].shape), whole(lw["wlora"].shape), whole(lw["seg"].shape),
                     whole(lw["v512"].shape), whole(lw["mu"].shape)]),
        out_specs=[whole(s) for s in out_shapes],
        out_shape=[jax.ShapeDtypeStruct(s, f32) for s in out_shapes],
        compiler_params=pltpu.CompilerParams(dimension_semantics=("arbitrary",),
                                             vmem_limit_bytes=VMEM_LIMIT_BYTES),
        name="sample_pre",
    )(proj, lconv_t, lh, rshift, cconv_t, lw["wlru"], lw["wlora"], lw["seg"], lw["v512"], lw["mu"])


def _sample_rwkv_call(s_t, vecs, l):
    _, nh, nv, nk, n = s_t.shape
    return pl.pallas_call(
        _sample_rwkv_kernel,
        grid=(nh,),
        in_specs=[pl.BlockSpec((None, None, nv, nk, n), lambda h: (l, h, 0, 0, 0)),
                  pl.BlockSpec((6, None, nk, n), lambda h: (0, h, 0, 0))],
        out_specs=[pl.BlockSpec((None, nv, nk, n), lambda h: (h, 0, 0, 0)),
                   pl.BlockSpec((None, nv, n), lambda h: (h, 0, 0))],
        out_shape=[jax.ShapeDtypeStruct((nh, nv, nk, n), f32), jax.ShapeDtypeStruct((nh, nv, n), f32)],
        compiler_params=pltpu.CompilerParams(dimension_semantics=("arbitrary",),
                                             vmem_limit_bytes=VMEM_LIMIT_BYTES),
        name="sample_rwkv",
    )(s_t, vecs)


def _sample_merge_kernel(x_ref, mod_ref, p_ref, ob_ref, aux_ref, ot_ref, seg_ref, v512_ref, wbr_ref, wout_ref,
                         v1024_ref, y_ref, *, alpha):
    def vec(row):
        return v512_ref[row:row + 1, :]

    o = jnp.concatenate([ot_ref[hd].T for hd in range(RWKV_HEADS)], axis=1)
    ob_rwkv = _rwkv_post(o, aux_ref[:, 0:BR_W], vec, seg_ref) * aux_ref[:, BR_W:2 * BR_W]
    branches = [ob_ref[:, 0:BR_W], ob_rwkv, ob_ref[:, BR_W:2 * BR_W], ob_ref[:, 2 * BR_W:3 * BR_W]]
    m = None
    for nb in range(N_BRANCH):
        off = OFF_MERGE + nb * D_MODEL
        term = _merge_gate_times(p_ref[:, off:off + D_MODEL], _bdot(branches[nb], wbr_ref[nb]))
        m = term if m is None else m + term
    yv = _bdot(m, wout_ref[...]) + v1024_ref[M_BOUT:M_BOUT + 1, :]
    xn = alpha * x_ref[...] + mod_ref[:, 2 * D_MODEL:3 * D_MODEL] * yv
    y_ref[...] = _layernorm(xn, v1024_ref[M_LNG:M_LNG + 1, :], v1024_ref[M_LNB:M_LNB + 1, :])


def _sample_merge_call(x, mod, proj, ob, aux, ot, lw, big, l, alpha):
    n = x.shape[0]
    return pl.pallas_call(
        functools.partial(_sample_merge_kernel, alpha=alpha),
        grid=(1,),
        in_specs=[pl.BlockSpec((n, D_MODEL), lambda i: (0, 0)),
                  pl.BlockSpec((n, 3 * D_MODEL), lambda i: (0, 0)),
                  pl.BlockSpec((n, N_IN), lambda i: (0, 0)),
                  pl.BlockSpec(ob.shape, lambda i: (0, 0)),
                  pl.BlockSpec(aux.shape, lambda i: (0, 0)),
                  pl.BlockSpec(ot.shape, lambda i: (0, 0, 0)),
                  pl.BlockSpec(lw["seg"].shape, lambda i: (0, 0)),
                  pl.BlockSpec(lw["v512"].shape, lambda i: (0, 0)),
                  pl.BlockSpec((None,) + big["wbr"].shape[1:], lambda i: (l, 0, 0, 0)),
                  pl.BlockSpec((None,) + big["wout"].shape[1:], lambda i: (l, 0, 0)),
                  pl.BlockSpec(lw["v1024"].shape, lambda i: (0, 0))],
        out_specs=pl.BlockSpec((n, D_MODEL), lambda i: (0, 0)),
        out_shape=jax.ShapeDtypeStruct((n, D_MODEL), f32),
        compiler_params=pltpu.CompilerParams(dimension_semantics=("arbitrary",),
                                             vmem_limit_bytes=VMEM_LIMIT_BYTES),
        name="sample_merge",
    )(x, mod, proj, ob, aux, ot, lw["seg"], lw["v512"], big["wbr"], big["wout"], lw["v1024"])


def _pack_layer(l, lru_conv_w, lru_conv_b, lru_wr, lru_br, lru_wi, lru_bi, lru_lambda, rwkv_mu,
                rwkv_w0, rwkv_ww, rwkv_a0, rwkv_wa, rwkv_kk, rwkv_ka, rwkv_rk, rwkv_lnx_g, rwkv_lnx_b,
                gmlp_ln_g, gmlp_ln_b, gmlp_ws, gmlp_bs, conf_dw_w, conf_dw_b, conf_ln_g, conf_ln_b,
                b_out, ln_g, ln_b):
    rep = BR_W // GMLP_GROUPS
    wr = block_diag(*[lru_wr[l, h] for h in range(LRU_HEADS)])
    wi = block_diag(*[lru_wi[l, h] for h in range(LRU_HEADS)])
    zero_lora = jnp.zeros((RWKV_RANK, BR_W), f32)
    wlora = jnp.concatenate([jnp.concatenate([rwkv_ww[l], zero_lora], axis=1),
                             jnp.concatenate([zero_lora, rwkv_wa[l]], axis=1)], axis=0)
    tril = jnp.tril(jnp.ones((GMLP_CHUNK, GMLP_CHUNK), dtype=bool))
    wm = jnp.where(tril[None], gmlp_ws[l], 0.0)
    wg = jnp.stack([jnp.concatenate([wm[2 * q], wm[2 * q + 1]], axis=1) for q in range(GMLP_GROUPS // 2)])
    rows = [lru_conv_w[l], lru_conv_b[l][None], lru_br[l][None], lru_bi[l][None], lru_lambda[l][None],
            rwkv_w0[l][None], rwkv_a0[l][None], rwkv_kk[l][None], rwkv_ka[l][None],
            rwkv_rk[l].reshape(1, BR_W), rwkv_lnx_g[l][None], rwkv_lnx_b[l][None],
            gmlp_ln_g[l][None], gmlp_ln_b[l][None], conf_dw_b[l][None], conf_ln_g[l][None],
            conf_ln_b[l][None], jnp.repeat(gmlp_ws[l, :, 0, 0], rep)[None],
            jnp.repeat(gmlp_bs[l, :, 0], rep)[None], jnp.zeros((V_CDW - V_GBS0 - 1, BR_W), f32),
            conf_dw_w[l], jnp.zeros((V_ROWS - V_CDW - CONF_K, BR_W), f32)]
    v1024 = jnp.concatenate([b_out[l][None], ln_g[l][None], ln_b[l][None],
                             jnp.zeros((SUBLANES - 3, D_MODEL), f32)], axis=0)
    nchunk = TIME_TILE // RWKV_CHUNK
    return dict(
        wlru=jnp.concatenate([wr, wi], axis=1).astype(bf16),
        wlora=wlora.astype(bf16),
        wg=wg.astype(bf16),
        bsx=jnp.repeat(gmlp_bs[l].T, rep, axis=1),
        seg=jnp.kron(jnp.eye(RWKV_HEADS, dtype=f32), jnp.ones((RWKV_HD, RWKV_HD), f32)).astype(bf16),
        ltri=jnp.kron(jnp.eye(nchunk, dtype=f32), jnp.tril(jnp.ones((RWKV_CHUNK, RWKV_CHUNK), f32))).astype(bf16),
        v512=jnp.concatenate(rows, axis=0),
        mu=rwkv_mu[l][None],
        v1024=v1024,
    )


def kernel(x_prompt, x_sample, state_lru_conv, state_lru_h, state_rwkv_shift, state_rwkv_S, state_conf_conv, c_prompt, c_sample, w_cond, b_cond, w_in, lru_conv_w, lru_conv_b, lru_wr, lru_br, lru_wi, lru_bi, lru_lambda, rwkv_mu, rwkv_w0, rwkv_ww, rwkv_a0, rwkv_wa, rwkv_kk, rwkv_ka, rwkv_rk, rwkv_lnx_g, rwkv_lnx_b, gmlp_ln_g, gmlp_ln_b, gmlp_ws, gmlp_bs, conf_dw_w, conf_dw_b, conf_ln_g, conf_ln_b, w_branch, w_out, b_out, ln_g, ln_b):
    depth = w_in.shape[0]
    alpha = (2.0 * depth) ** 0.25
    nb = x_prompt.shape[0]
    ns = x_sample.shape[0]

    mod = _cond_call(jnp.concatenate([c_prompt, c_sample], axis=0), w_cond, b_cond)
    col = lax.broadcasted_iota(jnp.int32, (1, 1, N_IN), 2)
    big = dict(win=(w_in * jnp.where(col >= OFF_MERGE, 0.5, 1.0)).astype(bf16),
               wbr=(0.5 * w_branch).astype(bf16), wout=w_out.astype(bf16))
    lconv_t = jnp.transpose(state_lru_conv, (0, 2, 1, 3))
    cconv_t = jnp.transpose(state_conf_conv, (0, 2, 1, 3))
    s_t = jnp.transpose(state_rwkv_S, (0, 2, 3, 4, 1))
    xp = x_prompt
    xs = x_sample.reshape(ns, D_MODEL)
    outs_p, outs_s = [], []
    for l in range(depth):
        lw = _pack_layer(l, lru_conv_w, lru_conv_b, lru_wr, lru_br, lru_wi, lru_bi, lru_lambda, rwkv_mu,
                         rwkv_w0, rwkv_ww, rwkv_a0, rwkv_wa, rwkv_kk, rwkv_ka, rwkv_rk, rwkv_lnx_g,
                         rwkv_lnx_b, gmlp_ln_g, gmlp_ln_b, gmlp_ws, gmlp_bs, conf_dw_w, conf_dw_b,
                         conf_ln_g, conf_ln_b, b_out, ln_g, ln_b)
        mod_p = mod[l, :nb].reshape(nb, 3, D_MODEL)
        mod_s = mod[l, nb:]
        xp, lconv_p, lh_p, rshift_p, rs_p, cconv_p = _prompt_call(xp, mod_p, lw, big, l, alpha)
        outs_p.append((lconv_p, lh_p.reshape(nb, BR_W), rshift_p.reshape(nb, RWKV_SHIFT_W), rs_p, cconv_p))

        proj_s = _sample_proj_call(xs, mod_s, big["win"], l)
        ob, aux, vecs, lconv_s, lh_s, rshift_s, cconv_s, gv_s = _sample_pre_call(
            proj_s, lconv_t, state_lru_h, state_rwkv_shift, cconv_t, lw, l)
        rs_s, ot = _sample_rwkv_call(s_t, vecs, l)
        xs = _sample_merge_call(xs, mod_s, proj_s, ob, aux, ot, lw, big, l, alpha)
        outs_s.append((lconv_s, lh_s, rshift_s, rs_s, cconv_s, gv_s.reshape(ns, 1, BR_W)))

    def stk(outs, j):
        return jnp.stack([o[j] for o in outs])

    return (xp, xs.reshape(ns, 1, D_MODEL),
            stk(outs_p, 0), jnp.transpose(stk(outs_s, 0), (0, 2, 1, 3)),
            stk(outs_p, 1), stk(outs_s, 1),
            stk(outs_p, 2), stk(outs_s, 2),
            stk(outs_p, 3), jnp.transpose(stk(outs_s, 3), (0, 4, 1, 2, 3)),
            stk(outs_p, 4), jnp.transpose(stk(outs_s, 4), (0, 2, 1, 3)),
            stk(outs_s, 5))
```

```python
import functools

import jax
import jax.numpy as jnp
from jax import lax
from jax.experimental import pallas as pl
from jax.experimental.pallas import tpu as pltpu
from jax.scipy.linalg import block_diag

f32 = jnp.float32
bf16 = jnp.bfloat16

D_MODEL = 1024
N_BRANCH = 4
BR_W = D_MODEL // 2
LRU_HEADS = 8
LRU_CONV = 4
LRU_C = 8.0
RWKV_HD = 64
RWKV_HEADS = BR_W // RWKV_HD
RWKV_RANK = D_MODEL // 16
RWKV_SHIFT_W = 3 * BR_W + 2 * RWKV_RANK
RWKV_DECAY_SCALE = 0.606531
RWKV_LNX_EPS = 64e-5
GMLP_CHUNK = 128
GMLP_GROUPS = 8
CONF_K = 31
LN_EPS = 1e-5

OFF_LRU = 0
OFF_RWKV = OFF_LRU + 2 * BR_W
OFF_GMLP = OFF_RWKV + RWKV_SHIFT_W + BR_W
OFF_CONF = OFF_GMLP + 3 * BR_W
OFF_MERGE = OFF_CONF + 3 * BR_W
N_IN = OFF_MERGE + N_BRANCH * D_MODEL

LANES = 128
SUBLANES = 8
VMEM_LIMIT_BYTES = 60 * 1024 * 1024

TIME_TILE = 256
RWKV_CHUNK = 64
CHUNK_SHIFT = RWKV_CHUNK.bit_length() - 1
RWKV_GROUP = 16
CONV_ROWS = 64
PROJ_CHUNK = 256
LRU_HALO = SUBLANES
CONF_HALO = 32

V_LCW, V_LCB, V_LBR, V_LBI, V_LAM = 0, 4, 5, 6, 7
V_W0, V_A0, V_KK, V_KA, V_RK, V_LNXG, V_LNXB = 8, 9, 10, 11, 12, 13, 14
V_GLG, V_GLB, V_CDB, V_CLG, V_CLB, V_GWS0, V_GBS0 = 15, 16, 17, 18, 19, 20, 21
V_CDW = 24
V_ROWS = 56
M_BOUT, M_LNG, M_LNB = 0, 1, 2


def _sigmoid(x):
    return 0.5 * jnp.tanh(0.5 * x) + 0.5


def _silu(x):
    hx = 0.5 * x
    return hx * jnp.tanh(hx) + hx


def _merge_gate_times(p_half, y_half):
    return (jnp.tanh(p_half) + 1.0) * y_half


def _softplus(z):
    return jnp.maximum(z, 0.0) + jnp.log1p(jnp.exp(-jnp.abs(z)))


def _layernorm(x, g, b, eps=LN_EPS):
    mu = jnp.mean(x, axis=-1, keepdims=True)
    xc = x - mu
    var = jnp.mean(xc * xc, axis=-1, keepdims=True)
    return xc * lax.rsqrt(var + eps) * g + b


def _bdot(a, b):
    return jnp.dot(a.astype(bf16), b.astype(bf16), preferred_element_type=f32)


def _bdot_nt(a, b):
    return lax.dot_general(a.astype(bf16), b.astype(bf16), (((1,), (1,)), ((), ())),
                           preferred_element_type=f32)


def _bdot_tn(a, b):
    return lax.dot_general(a.astype(bf16), b.astype(bf16), (((0,), (0,)), ((), ())),
                           preferred_element_type=f32)


def _segsum(x, seg_ref):
    return jnp.dot(x.astype(bf16), seg_ref[...], preferred_element_type=f32)


def _lru_gates(xc, rg, vec):
    r = _sigmoid(rg[:, :BR_W] + vec(V_LBR))
    ig = _sigmoid(rg[:, BR_W:] + vec(V_LBI))
    log_a = -LRU_C * r * _softplus(-vec(V_LAM))
    a = jnp.exp(log_a)
    mult = jnp.sqrt(-jnp.tanh(log_a) * (a * a + 1.0))
    return a, mult * (ig * xc)


def _rwkv_prep(xs, vec, wlora_ref, seg_ref):
    r = xs[:, 0:BR_W]
    k = xs[:, BR_W:2 * BR_W]
    v = xs[:, 2 * BR_W:3 * BR_W]
    dwa = xs[:, 3 * BR_W:3 * BR_W + 2 * RWKV_RANK]
    lane = lax.broadcasted_iota(jnp.int32, dwa.shape, 1)
    lora_in = jnp.where(lane < RWKV_RANK, jnp.tanh(dwa), dwa)
    wa = jnp.dot(lora_in.astype(bf16), wlora_ref[...], preferred_element_type=f32)
    logw = -RWKV_DECAY_SCALE * _sigmoid(vec(V_W0) + wa[:, :BR_W])
    a = _sigmoid(vec(V_A0) + wa[:, BR_W:])
    kk = k * vec(V_KK)
    kkn = kk / jnp.maximum(jnp.sqrt(_segsum(kk * kk, seg_ref)), 1e-12)
    k2 = k * (1.0 + (a - 1.0) * vec(V_KA))
    bonus = _segsum(r * k2 * vec(V_RK), seg_ref) * v
    return r, k2, v, logw, a, kkn, bonus


def _rwkv_post(o, bonus, vec, seg_ref):
    inv_n = 1.0 / RWKV_HD
    mean = _segsum(o, seg_ref) * inv_n
    oc = o - mean
    var = _segsum(oc * oc, seg_ref) * inv_n
    return oc * lax.rsqrt(var + RWKV_LNX_EPS) * vec(V_LNXG) + vec(V_LNXB) + bonus


def _scan_rows(a, u, h0):
    n = a.shape[0]
    in_group = lax.broadcasted_iota(jnp.int32, a.shape, 0) & (SUBLANES - 1)
    d = 1
    while d < SUBLANES:
        keep = in_group >= d
        a_s = jnp.where(keep, pltpu.roll(a, d, axis=0), 1.0)
        u_s = jnp.where(keep, pltpu.roll(u, d, axis=0), 0.0)
        u = u + a * u_s
        a = a * a_s
        d *= 2
    carry = h0
    groups = []
    for g in range(n // SUBLANES):
        rs = slice(g * SUBLANES, (g + 1) * SUBLANES)
        hg = u[rs] + a[rs] * carry
        groups.append(hg)
        carry = hg[SUBLANES - 1:SUBLANES]
    return jnp.concatenate(groups, axis=0)


def _cond_kernel(c_ref, w_ref, b_ref, o_ref):
    o_ref[0] = _bdot(_silu(c_ref[...]), w_ref[0]) + b_ref[0]


def _cond_call(c_all, w_cond, b_cond):
    depth = w_cond.shape[0]
    n = c_all.shape[0]
    return pl.pallas_call(
        _cond_kernel,
        grid=(depth,),
        in_specs=[pl.BlockSpec((n, D_MODEL), lambda l: (0, 0)),
                  pl.BlockSpec((1, D_MODEL, 3 * D_MODEL), lambda l: (l, 0, 0)),
                  pl.BlockSpec((1, 1, 3 * D_MODEL), lambda l: (l, 0, 0))],
        out_specs=pl.BlockSpec((1, n, 3 * D_MODEL), lambda l: (l, 0, 0)),
        out_shape=jax.ShapeDtypeStruct((depth, n, 3 * D_MODEL), f32),
        compiler_params=pltpu.CompilerParams(dimension_semantics=("arbitrary",),
                                             vmem_limit_bytes=VMEM_LIMIT_BYTES),
        name="cond",
    )(c_all, w_cond, b_cond.reshape(depth, 1, 3 * D_MODEL))


def _prompt_kernel(x_ref, mod_ref, win_ref, wlru_ref, wlora_ref, wg_ref, bsx_ref, seg_ref, ltri_ref,
                   wbr_ref, wout_ref, v512_ref, mu_ref, v1024_ref,
                   y_ref, lconv_ref, lh_ref, rshift_ref, rs_ref, cconv_ref,
                   h_ref, lext_ref, lhc_ref, rprev_ref, s_ref, cext_ref, ycv_ref, ob_ref, rw_ref,
                   ecc_ref, o_ref, rhat_ref, mp_ref, nn_ref, shc_ref, *, alpha):
    T = TIME_TILE
    C = RWKV_CHUNK
    npair = RWKV_HEADS // 2
    first = pl.program_id(1) == 0

    def vec(row):
        return v512_ref[row:row + 1, :]

    def carried(x):
        return jnp.where(first, jnp.zeros_like(x), x)

    lext_ref[0:LRU_HALO, :] = carried(lext_ref[0:LRU_HALO, :])
    cext_ref[0:CONF_HALO, :] = carried(cext_ref[0:CONF_HALO, :])
    for q in range(npair):
        s_ref[q] = carried(s_ref[q])

    shift = mod_ref[0, 0:1, :]
    scale = mod_ref[0, 1:2, :]
    gate = mod_ref[0, 2:3, :]
    h_ref[...] = (x_ref[0] * (1.0 + scale) + shift).astype(bf16)

    def proj(off, n):
        return jnp.dot(h_ref[...], win_ref[:, off:off + n], preferred_element_type=f32)

    pa = proj(OFF_LRU, 2 * BR_W)
    lext_ref[LRU_HALO:LRU_HALO + T, :] = pa[:, :BR_W]
    xc = vec(V_LCB)
    for j in range(LRU_CONV):
        xc = xc + vec(V_LCW + j) * lext_ref[pl.ds(LRU_HALO - (LRU_CONV - 1) + j, T), :]
    lext_ref[0:LRU_HALO, :] = lext_ref[T:T + LRU_HALO, :]
    rg = jnp.dot(xc.astype(bf16), wlru_ref[...], preferred_element_type=f32)

    pending = [(key, off + o, min(PROJ_CHUNK, n - o))
               for key, off, n in (("b", OFF_RWKV, RWKV_SHIFT_W + BR_W), ("d", OFF_CONF, 3 * BR_W),
                                   ("c", OFF_GMLP, 3 * BR_W))
               for o in range(0, n, PROJ_CHUNK)]
    chunks = {"b": [], "d": [], "c": []}

    def pump(count):
        for _ in range(count):
            if pending:
                key, off, width = pending.pop(0)
                chunks[key].append(proj(off, width))

    pump(-(-(RWKV_SHIFT_W + BR_W) // PROJ_CHUNK))
    pb = jnp.concatenate(chunks["b"], axis=1)
    per_stage = -(-len(pending) // 4)

    a, u = _lru_gates(xc, rg, vec)
    hfull = _scan_rows(a, u, carried(lhc_ref[0:1, :]))
    lhc_ref[0:1, :] = hfull[T - 1:T, :]
    ob_ref[0] = (hfull * _silu(pa[:, BR_W:])).astype(bf16)

    p = pb[:, :RWKV_SHIFT_W]
    rows = lax.broadcasted_iota(jnp.int32, p.shape, 0)
    prev = jnp.where(rows == 0, carried(rprev_ref[0:1, :]), pltpu.roll(p, 1, axis=0))
    rprev_ref[0:1, :] = p[T - 1:T, :]
    xs = p + (prev - p) * mu_ref[...]
    r = xs[:, 0:BR_W]
    k = xs[:, BR_W:2 * BR_W]
    v = xs[:, 2 * BR_W:3 * BR_W]
    dwa = xs[:, 3 * BR_W:3 * BR_W + 2 * RWKV_RANK]
    lora_in = jnp.where(lax.broadcasted_iota(jnp.int32, dwa.shape, 1) < RWKV_RANK, jnp.tanh(dwa), dwa)
    kk = k * vec(V_KK)
    pump(per_stage)
    wa = jnp.dot(lora_in.astype(bf16), wlora_ref[...], preferred_element_type=f32)
    logw = -RWKV_DECAY_SCALE * _sigmoid(vec(V_W0) + wa[:, :BR_W])
    ag = _sigmoid(vec(V_A0) + wa[:, BR_W:])
    k2 = k * (1.0 + (ag - 1.0) * vec(V_KA))
    pump(per_stage)
    kkn = kk / jnp.maximum(jnp.sqrt(_segsum(kk * kk, seg_ref)), 1e-12)
    bonus = _segsum(r * k2 * vec(V_RK), seg_ref) * v
    lw1 = logw.astype(bf16)
    rem = logw - lw1.astype(f32)
    lw2 = rem.astype(bf16)
    lw3 = (rem - lw2.astype(f32)).astype(bf16)
    pump(per_stage)
    ltri = ltri_ref[...]
    c = (jnp.dot(ltri, lw1, preferred_element_type=f32) + jnp.dot(ltri, lw2, preferred_element_type=f32)
         + jnp.dot(ltri, lw3, preferred_element_type=f32))
    cend = jnp.concatenate(
        [jnp.broadcast_to(c[(n + 1) * C - 1:(n + 1) * C, :], (C, BR_W)) for n in range(T // C)], axis=0)
    for n in range(T // C):
        ecc_ref[n * SUBLANES:(n + 1) * SUBLANES, :] = jnp.broadcast_to(
            jnp.exp(c[(n + 1) * C - 1:(n + 1) * C, :]), (SUBLANES, BR_W))
    e_neg = jnp.exp(-c)
    e_end = jnp.exp(cend - c)
    beta = kkn * ag
    rw_ref[0] = (-kkn * jnp.exp(c - logw)).astype(bf16)
    rw_ref[1] = (r * jnp.exp(c)).astype(bf16)
    rw_ref[2] = (beta * e_neg).astype(bf16)
    rw_ref[3] = (k2 * e_neg).astype(bf16)
    rw_ref[4] = v.astype(bf16)
    rw_ref[5] = (beta * e_end).astype(bf16)
    rw_ref[6] = (k2 * e_end).astype(bf16)
    pump(len(pending))
    pd = jnp.concatenate(chunks["d"], axis=1)
    pc = jnp.concatenate(chunks["c"], axis=1)

    R = lax.broadcasted_iota(jnp.int32, (2 * C, LANES), 0)
    Cc = lax.broadcasted_iota(jnp.int32, (2 * C, LANES), 1)
    t_idx = R & (C - 1)
    s_idx = Cc & (C - 1)
    lower = s_idx < t_idx
    mask_a = jnp.where(R < C, lower.astype(f32), (s_idx <= t_idx).astype(f32)).astype(bf16)
    block_mask = ((R >> CHUNK_SHIFT) == (Cc >> CHUNK_SHIFT)).astype(f32)
    eye = (R == Cc).astype(f32)
    lvl0_mask = (((R >> 1) == (Cc >> 1)) & lower).astype(f32)
    lvl_masks = []
    sh = 1
    while (1 << sh) < C:
        lvl_masks.append((((R >> (sh + 1)) == (Cc >> (sh + 1))) & ((R >> sh) != (Cc >> sh)) & lower)
                         .astype(f32).astype(bf16))
        sh += 1
    left64 = lax.broadcasted_iota(jnp.int32, (C, LANES), 1) < C
    left128 = Cc < C

    def sel_l(x, m):
        return jnp.where(m, x, jnp.zeros_like(x))

    def sel_r(x, m):
        return jnp.where(m, jnp.zeros_like(x), x)

    left256 = (lax.broadcasted_iota(jnp.int32, (C, 2 * LANES), 1) & (LANES - 1)) < C

    def split_lr(x, m):
        return jnp.concatenate([sel_l(x, m), sel_r(x, m)], axis=0)

    cext_ref[CONF_HALO:CONF_HALO + T, :] = pd[:, :BR_W] * _sigmoid(pd[:, BR_W:2 * BR_W])
    conf_gate = _silu(pd[:, 2 * BR_W:])

    vn = _layernorm(pc[:, BR_W:2 * BR_W], vec(V_GLG), vec(V_GLB))
    left_g = lax.broadcasted_iota(jnp.int32, (GMLP_CHUNK, LANES), 1) < (LANES // 2)
    z_rows = []
    for n in range(T // GMLP_CHUNK):
        z_cols = []
        for q in range(GMLP_GROUPS // 2):
            vq = vn[n * GMLP_CHUNK:(n + 1) * GMLP_CHUNK, q * LANES:(q + 1) * LANES].astype(bf16)
            rhs = jnp.concatenate([sel_l(vq, left_g), sel_r(vq, left_g)], axis=0)
            z_cols.append(jnp.dot(wg_ref[q], rhs, preferred_element_type=f32))
        z_rows.append(jnp.concatenate(z_cols, axis=1) + bsx_ref[...])
    z = jnp.concatenate(z_rows, axis=0)
    ob_ref[2] = (pc[:, :BR_W] * z * _silu(pc[:, 2 * BR_W:])).astype(bf16)

    items = [(n, q) for n in range(T // C) for q in range(npair)]
    for g0 in range(0, len(items), RWKV_GROUP):
        grp = items[g0:g0 + RWKV_GROUP]

        def ld(kind):
            return [rw_ref[kind, n * C:(n + 1) * C, q * LANES:(q + 1) * LANES] for n, q in grp]

        at, rt, bt, kt, vv, b_end, k_end = [ld(kind) for kind in range(7)]
        ar = [jnp.concatenate([a_, r_], axis=0) for a_, r_ in zip(at, rt)]
        out_e = [_bdot_nt(sel_l(x, left128), jnp.concatenate([b_, k_], axis=0)) for x, b_, k_ in zip(ar, bt, kt)]
        out_o = [_bdot_nt(sel_r(x, left128), jnp.concatenate([k_, b_], axis=0)) for x, b_, k_ in zip(ar, bt, kt)]
        tm = [eye + jnp.concatenate([sel_l(e_[0:C], left64), sel_r(o_[0:C], left64)], axis=0) * lvl0_mask
              for e_, o_ in zip(out_e, out_o)]
        out_e = [x.astype(bf16) * mask_a for x in out_e]
        out_o = [x.astype(bf16) * mask_a for x in out_o]
        a_pair = [jnp.concatenate([sel_l(e_[0:C], left64), sel_r(o_[0:C], left64)], axis=0)
                  for e_, o_ in zip(out_e, out_o)]
        lhs_ak = [jnp.where(left64, o_[0:C], e_[0:C]) for e_, o_ in zip(out_e, out_o)]
        lhs_rb = [jnp.where(left64, e_[C:2 * C], o_[C:2 * C]) for e_, o_ in zip(out_e, out_o)]
        lhs_rk = [jnp.where(left64, o_[C:2 * C], e_[C:2 * C]) for e_, o_ in zip(out_e, out_o)]
        v_rl = [jnp.concatenate([sel_r(x, left64), sel_l(x, left64)], axis=0) for x in vv]
        av = [jnp.dot(l_, x, preferred_element_type=f32).astype(bf16) for l_, x in zip(lhs_ak, v_rl)]
        ark = [jnp.dot(l_, x, preferred_element_type=f32) for l_, x in zip(lhs_rk, v_rl)]
        for lm in lvl_masks:
            w = [_bdot(x * lm, t_) for x, t_ in zip(a_pair, tm)]
            tm = [t_ + _bdot(t_, w_) for t_, w_ in zip(tm, w)]
        t_row = [(t_[0:C] + t_[C:2 * C]).astype(bf16) for t_ in tm]
        tx = [jnp.dot(t_, split_lr(jnp.concatenate([a_, x], axis=1), left256),
                      preferred_element_type=f32).astype(bf16)
              for t_, a_, x in zip(t_row, at, av)]
        arb = [jnp.dot(l_, split_lr(x, left256), preferred_element_type=f32) for l_, x in zip(lhs_rb, tx)]
        mp = [_bdot_tn(x[:, 0:LANES], b_) * block_mask for x, b_ in zip(tx, b_end)]
        nn = [_bdot_tn(jnp.concatenate([x[:, LANES:2 * LANES], v_], axis=0),
                       jnp.concatenate([b_, k_], axis=0)) * block_mask
              for x, v_, b_, k_ in zip(tx, vv, b_end, k_end)]
        for idx, (n, q) in enumerate(grp):
            rs = slice(n * C, (n + 1) * C)
            ls = slice(q * LANES, (q + 1) * LANES)
            rhat_ref[rs, ls] = (rt[idx].astype(f32) + arb[idx][:, 0:LANES]).astype(bf16)
            o_ref[rs, ls] = arb[idx][:, LANES:2 * LANES] + ark[idx]
            mp_ref[n * npair + q] = mp[idx].astype(bf16)
            nn_ref[n * npair + q] = nn[idx]

    merge_cols = {0: [], 2: []}
    merge_queue = [(nb, o) for nb in (0, 2) for o in range(0, D_MODEL, PROJ_CHUNK)]

    def pump_merge(count):
        for _ in range(count):
            if merge_queue:
                nb, o = merge_queue.pop(0)
                merge_cols[nb].append(_merge_gate_times(
                    proj(OFF_MERGE + nb * D_MODEL + o, PROJ_CHUNK),
                    jnp.dot(ob_ref[nb], wbr_ref[nb, :, o:o + PROJ_CHUNK], preferred_element_type=f32)))

    for n in range(T // C):
        rs = slice(n * C, (n + 1) * C)
        sp = [s_ref[q] for q in range(npair)]
        sb = [x.astype(bf16) for x in sp]
        o_blk = [_bdot_nt(rhat_ref[rs, q * LANES:(q + 1) * LANES], sb[q]) for q in range(npair)]
        s_new = [jnp.dot(sb[q], mp_ref[n * npair + q], preferred_element_type=f32) for q in range(npair)]
        pump_merge(-(-len(merge_cols) * (D_MODEL // PROJ_CHUNK) // (T // C)))
        for q in range(npair):
            ls = slice(q * LANES, (q + 1) * LANES)
            o_ref[rs, ls] = o_ref[rs, ls] + o_blk[q]
            s_ref[q] = sp[q] * ecc_ref[n * SUBLANES:n * SUBLANES + 1, ls] + s_new[q] + nn_ref[n * npair + q]
    pump_merge(len(merge_queue))

    ob_ref[1] = (_rwkv_post(o_ref[...], bonus, vec, seg_ref) * _silu(pb[:, RWKV_SHIFT_W:])).astype(bf16)

    def merge_term(nb):
        return _merge_gate_times(proj(OFF_MERGE + nb * D_MODEL, D_MODEL),
                                 jnp.dot(ob_ref[nb], wbr_ref[nb], preferred_element_type=f32))

    m = (jnp.concatenate(merge_cols[0], axis=1) + merge_term(1)) + jnp.concatenate(merge_cols[2], axis=1)

    base = CONF_HALO - (CONF_K - 1)
    shift_rows = T + CONF_HALO - SUBLANES
    for lb in range(BR_W // LANES):
        ls = slice(lb * LANES, (lb + 1) * LANES)
        for s in range(1, SUBLANES):
            shc_ref[s, 0:shift_rows, :] = cext_ref[pl.ds(s, shift_rows), ls]
        for rb in range(T // CONV_ROWS):
            acc = jnp.broadcast_to(v512_ref[V_CDB:V_CDB + 1, ls], (CONV_ROWS, LANES))
            for j in range(CONF_K):
                q, s = divmod(base + j, SUBLANES)
                row0 = rb * CONV_ROWS + q * SUBLANES
                if s == 0:
                    tap_in = cext_ref[row0:row0 + CONV_ROWS, ls]
                else:
                    tap_in = shc_ref[s, row0:row0 + CONV_ROWS, :]
                acc = acc + v512_ref[V_CDW + j:V_CDW + j + 1, ls] * tap_in
            ycv_ref[rb * CONV_ROWS:(rb + 1) * CONV_ROWS, ls] = acc
    cext_ref[0:CONF_HALO, :] = cext_ref[T:T + CONF_HALO, :]
    ob_ref[3] = (_silu(_layernorm(ycv_ref[...], vec(V_CLG), vec(V_CLB))) * conf_gate).astype(bf16)

    m = m + merge_term(3)
    yv = jnp.dot(m.astype(bf16), wout_ref[...], preferred_element_type=f32) + v1024_ref[M_BOUT:M_BOUT + 1, :]
    xn = alpha * x_ref[0] + gate * yv
    y_ref[0] = _layernorm(xn, v1024_ref[M_LNG:M_LNG + 1, :], v1024_ref[M_LNB:M_LNB + 1, :])

    lconv_ref[0] = lext_ref[LRU_HALO - (LRU_CONV - 1):LRU_HALO, :]
    lh_ref[0] = lhc_ref[0:1, :]
    rshift_ref[0] = rprev_ref[0:1, :]
    cconv_ref[0] = cext_ref[CONF_HALO - (CONF_K - 1):CONF_HALO, :]
    for q in range(npair):
        sp = s_ref[q]
        rs_ref[0, 2 * q] = sp[0:C, 0:C]
        rs_ref[0, 2 * q + 1] = sp[C:2 * C, C:2 * C]


def _resident(shape):
    nd = len(shape)
    return pl.BlockSpec(shape, lambda b, i: (0,) * nd, pipeline_mode=pl.Buffered(1))


def _resident_layer(shape, l):
    nd = len(shape) - 1
    return pl.BlockSpec((None,) + tuple(shape[1:]), lambda b, i: (l,) + (0,) * nd,
                        pipeline_mode=pl.Buffered(1))


def _prompt_call(x, mod3, lw, big, l, alpha):
    nb, seq, _ = x.shape
    T = TIME_TILE
    assert seq % T == 0 and T % GMLP_CHUNK == 0 and T % RWKV_CHUNK == 0
    nt = seq // T
    in_specs = [
        pl.BlockSpec((1, T, D_MODEL), lambda b, i: (b, i, 0)),
        pl.BlockSpec((1, 3, D_MODEL), lambda b, i: (b, 0, 0)),
        _resident_layer(big["win"].shape, l), _resident(lw["wlru"].shape), _resident(lw["wlora"].shape),
        _resident(lw["wg"].shape), _resident(lw["bsx"].shape), _resident(lw["seg"].shape),
        _resident(lw["ltri"].shape), _resident_layer(big["wbr"].shape, l),
        _resident_layer(big["wout"].shape, l),
        _resident(lw["v512"].shape), _resident(lw["mu"].shape), _resident(lw["v1024"].shape),
    ]
    out_shape = (
        jax.ShapeDtypeStruct((nb, seq, D_MODEL), f32),
        jax.ShapeDtypeStruct((nb, LRU_CONV - 1, BR_W), f32),
        jax.ShapeDtypeStruct((nb, 1, BR_W), f32),
        jax.ShapeDtypeStruct((nb, 1, RWKV_SHIFT_W), f32),
        jax.ShapeDtypeStruct((nb, RWKV_HEADS, RWKV_HD, RWKV_HD), f32),
        jax.ShapeDtypeStruct((nb, CONF_K - 1, BR_W), f32),
    )
    out_specs = (
        pl.BlockSpec((1, T, D_MODEL), lambda b, i: (b, i, 0)),
        pl.BlockSpec((1, LRU_CONV - 1, BR_W), lambda b, i: (b, 0, 0)),
        pl.BlockSpec((1, 1, BR_W), lambda b, i: (b, 0, 0)),
        pl.BlockSpec((1, 1, RWKV_SHIFT_W), lambda b, i: (b, 0, 0)),
        pl.BlockSpec((1, RWKV_HEADS, RWKV_HD, RWKV_HD), lambda b, i: (b, 0, 0, 0)),
        pl.BlockSpec((1, CONF_K - 1, BR_W), lambda b, i: (b, 0, 0)),
    )
    scratch = [
        pltpu.VMEM((T, D_MODEL), bf16),
        pltpu.VMEM((T + LRU_HALO, BR_W), f32),
        pltpu.VMEM((SUBLANES, BR_W), f32),
        pltpu.VMEM((SUBLANES, RWKV_SHIFT_W), f32),
        pltpu.VMEM((RWKV_HEADS // 2, 2 * RWKV_HD, 2 * RWKV_HD), f32),
        pltpu.VMEM((T + CONF_HALO, BR_W), f32),
        pltpu.VMEM((T, BR_W), f32),
        pltpu.VMEM((N_BRANCH, T, BR_W), bf16),
        pltpu.VMEM((7, T, BR_W), bf16),
        pltpu.VMEM((T // RWKV_CHUNK * SUBLANES, BR_W), f32),
        pltpu.VMEM((T, BR_W), f32),
        pltpu.VMEM((T, BR_W), bf16),
        pltpu.VMEM((T // RWKV_CHUNK * (RWKV_HEADS // 2), 2 * RWKV_HD, 2 * RWKV_HD), bf16),
        pltpu.VMEM((T // RWKV_CHUNK * (RWKV_HEADS // 2), 2 * RWKV_HD, 2 * RWKV_HD), f32),
        pltpu.VMEM((SUBLANES, T + CONF_HALO, LANES), f32),
    ]
    return pl.pallas_call(
        functools.partial(_prompt_kernel, alpha=alpha),
        grid=(nb, nt),
        in_specs=in_specs,
        out_specs=out_specs,
        out_shape=out_shape,
        scratch_shapes=scratch,
        compiler_params=pltpu.CompilerParams(dimension_semantics=("arbitrary", "arbitrary"),
                                             vmem_limit_bytes=VMEM_LIMIT_BYTES),
        name="prompt_layer",
    )(x, mod3, big["win"], lw["wlru"], lw["wlora"], lw["wg"], lw["bsx"], lw["seg"], lw["ltri"],
      big["wbr"], big["wout"], lw["v512"], lw["mu"], lw["v1024"])


def _sample_proj_kernel(x_ref, mod_ref, w_ref, o_ref):
    shift = mod_ref[:, 0:D_MODEL]
    scale = mod_ref[:, D_MODEL:2 * D_MODEL]
    h = (x_ref[...] * (1.0 + scale) + shift).astype(bf16)
    o_ref[...] = jnp.dot(h, w_ref[...], preferred_element_type=f32)


SAMPLE_PROJ_COLS = 3456


def _sample_proj_call(x, mod, win, l):
    n = x.shape[0]
    assert N_IN % SAMPLE_PROJ_COLS == 0
    return pl.pallas_call(
        _sample_proj_kernel,
        grid=(N_IN // SAMPLE_PROJ_COLS,),
        in_specs=[pl.BlockSpec((n, D_MODEL), lambda j: (0, 0)),
                  pl.BlockSpec((n, 3 * D_MODEL), lambda j: (0, 0)),
                  pl.BlockSpec((None, D_MODEL, SAMPLE_PROJ_COLS), lambda j: (l, 0, j))],
        out_specs=pl.BlockSpec((n, SAMPLE_PROJ_COLS), lambda j: (0, j)),
        out_shape=jax.ShapeDtypeStruct((n, N_IN), f32),
        compiler_params=pltpu.CompilerParams(dimension_semantics=("arbitrary",),
                                             vmem_limit_bytes=VMEM_LIMIT_BYTES),
        name="sample_proj",
    )(x, mod, win)


def _sample_pre_kernel(p_ref, lconv_ref, lh_ref, rshift_ref, cconv_ref,
                       wlru_ref, wlora_ref, seg_ref, v512_ref, mu_ref,
                       ob_ref, aux_ref, vecs_ref, lconv_o, lh_o, rshift_o, cconv_o, gv_o):
    def vec(row):
        return v512_ref[row:row + 1, :]

    xb = p_ref[:, OFF_LRU:OFF_LRU + BR_W]
    xc = vec(V_LCB) + vec(V_LCW + LRU_CONV - 1) * xb
    for j in range(LRU_CONV - 1):
        xc = xc + vec(V_LCW + j) * lconv_ref[j]
    for j in range(LRU_CONV - 2):
        lconv_o[j] = lconv_ref[j + 1]
    lconv_o[LRU_CONV - 2] = xb
    a, u = _lru_gates(xc, jnp.dot(xc.astype(bf16), wlru_ref[...], preferred_element_type=f32), vec)
    hn = a * lh_ref[...] + u
    lh_o[...] = hn
    ob_ref[:, 0:BR_W] = hn * _silu(p_ref[:, OFF_LRU + BR_W:OFF_LRU + 2 * BR_W])

    p = p_ref[:, OFF_RWKV:OFF_RWKV + RWKV_SHIFT_W]
    xs = p + (rshift_ref[...] - p) * mu_ref[...]
    rshift_o[...] = p
    r, k2, v, logw, ag, kkn, bonus = _rwkv_prep(xs, vec, wlora_ref, seg_ref)
    n = RWKV_HD
    for idx, x in enumerate((-kkn, kkn * ag, k2, r, jnp.exp(logw), v)):
        for hd in range(RWKV_HEADS):
            vecs_ref[idx, hd] = x[:, hd * n:(hd + 1) * n].T
    aux_ref[:, 0:BR_W] = bonus
    aux_ref[:, BR_W:2 * BR_W] = _silu(p_ref[:, OFF_RWKV + RWKV_SHIFT_W:OFF_RWKV + RWKV_SHIFT_W + BR_W])

    vn = _layernorm(p_ref[:, OFF_GMLP + BR_W:OFF_GMLP + 2 * BR_W], vec(V_GLG), vec(V_GLB))
    gv_o[...] = vn
    z = vec(V_GWS0) * vn + vec(V_GBS0)
    ob_ref[:, BR_W:2 * BR_W] = (p_ref[:, OFF_GMLP:OFF_GMLP + BR_W] * z
                                * _silu(p_ref[:, OFF_GMLP + 2 * BR_W:OFF_GMLP + 3 * BR_W]))

    glu = p_ref[:, OFF_CONF:OFF_CONF + BR_W] * _sigmoid(p_ref[:, OFF_CONF + BR_W:OFF_CONF + 2 * BR_W])
    y = vec(V_CDB) + vec(V_CDW + CONF_K - 1) * glu
    for j in range(CONF_K - 1):
        y = y + vec(V_CDW + j) * cconv_ref[j]
    for j in range(CONF_K - 2):
        cconv_o[j] = cconv_ref[j + 1]
    cconv_o[CONF_K - 2] = glu
    ob_ref[:, 2 * BR_W:3 * BR_W] = (_silu(_layernorm(y, vec(V_CLG), vec(V_CLB)))
                                    * _silu(p_ref[:, OFF_CONF + 2 * BR_W:OFF_CONF + 3 * BR_W]))


def _sample_rwkv_kernel(s_ref, vecs_ref, s_o, ot_o):
    neg_kk = vecs_ref[0]
    beta = vecs_ref[1]
    k2 = vecs_ref[2]
    r = vecs_ref[3]
    w = vecs_ref[4]
    o_rows = []
    for vi in range(RWKV_HD):
        s0 = s_ref[vi]
        sa = jnp.sum(s0 * neg_kk, axis=0, keepdims=True)
        sn = s0 * w + sa * beta + vecs_ref[5, vi:vi + 1, :] * k2
        s_o[vi] = sn
        o_rows.append(jnp.sum(sn * r, axis=0, keepdims=True))
    ot_o[...] = jnp.concatenate(o_rows, axis=0)


def _sample_pre_call(proj, lconv_t, lh, rshift, cconv_t, lw, l):
    n = proj.shape[0]

    def whole(shape):
        nd = len(shape)
        return pl.BlockSpec(shape, lambda i: (0,) * nd, pipeline_mode=pl.Buffered(1))

    def layer(shape):
        nd = len(shape)
        return pl.BlockSpec((None,) + shape, lambda i: (l,) + (0,) * nd, pipeline_mode=pl.Buffered(1))

    state_shapes = [(LRU_CONV - 1, n, BR_W), (n, BR_W), (n, RWKV_SHIFT_W), (CONF_K - 1, n, BR_W)]
    out_shapes = ([(n, (N_BRANCH - 1) * BR_W), (n, 2 * BR_W), (6, RWKV_HEADS, RWKV_HD, n)]
                  + state_shapes + [(n, BR_W)])
    return pl.pallas_call(
        _sample_pre_kernel,
        grid=(1,),
        in_specs=([pl.BlockSpec((n, OFF_MERGE), lambda i: (0, 0), pipeline_mode=pl.Buffered(1))]
                  + [layer(s) for s in state_shapes]
                  + [whole(lw["wlru"].shape), whole(lw["wlora"].shape), whole(lw["seg"].shape),
                     whole(lw["v512"].shape), whole(lw["mu"].shape)]),
        out_specs=[whole(s) for s in out_shapes],
        out_shape=[jax.ShapeDtypeStruct(s, f32) for s in out_shapes],
        compiler_params=pltpu.CompilerParams(dimension_semantics=("arbitrary",),
                                             vmem_limit_bytes=VMEM_LIMIT_BYTES),
        name="sample_pre",
    )(proj, lconv_t, lh, rshift, cconv_t, lw["wlru"], lw["wlora"], lw["seg"], lw["v512"], lw["mu"])


def _sample_rwkv_call(s_t, vecs, l):
    _, nh, nv, nk, n = s_t.shape
    return pl.pallas_call(
        _sample_rwkv_kernel,
        grid=(nh,),
        in_specs=[pl.BlockSpec((None, None, nv, nk, n), lambda h: (l, h, 0, 0, 0)),
                  pl.BlockSpec((6, None, nk, n), lambda h: (0, h, 0, 0))],
        out_specs=[pl.BlockSpec((None, nv, nk, n), lambda h: (h, 0, 0, 0)),
                   pl.BlockSpec((None, nv, n), lambda h: (h, 0, 0))],
        out_shape=[jax.ShapeDtypeStruct((nh, nv, nk, n), f32), jax.ShapeDtypeStruct((nh, nv, n), f32)],
        compiler_params=pltpu.CompilerParams(dimension_semantics=("arbitrary",),
                                             vmem_limit_bytes=VMEM_LIMIT_BYTES),
        name="sample_rwkv",
    )(s_t, vecs)


def _sample_merge_kernel(x_ref, mod_ref, p_ref, ob_ref, aux_ref, ot_ref, seg_ref, v512_ref, wbr_ref, wout_ref,
                         v1024_ref, y_ref, *, alpha):
    def vec(row):
        return v512_ref[row:row + 1, :]

    o = jnp.concatenate([ot_ref[hd].T for hd in range(RWKV_HEADS)], axis=1)
    ob_rwkv = _rwkv_post(o, aux_ref[:, 0:BR_W], vec, seg_ref) * aux_ref[:, BR_W:2 * BR_W]
    branches = [ob_ref[:, 0:BR_W], ob_rwkv, ob_ref[:, BR_W:2 * BR_W], ob_ref[:, 2 * BR_W:3 * BR_W]]
    m = None
    for nb in range(N_BRANCH):
        off = OFF_MERGE + nb * D_MODEL
        term = _merge_gate_times(p_ref[:, off:off + D_MODEL], _bdot(branches[nb], wbr_ref[nb]))
        m = term if m is None else m + term
    yv = _bdot(m, wout_ref[...]) + v1024_ref[M_BOUT:M_BOUT + 1, :]
    xn = alpha * x_ref[...] + mod_ref[:, 2 * D_MODEL:3 * D_MODEL] * yv
    y_ref[...] = _layernorm(xn, v1024_ref[M_LNG:M_LNG + 1, :], v1024_ref[M_LNB:M_LNB + 1, :])


def _sample_merge_call(x, mod, proj, ob, aux, ot, lw, big, l, alpha):
    n = x.shape[0]
    return pl.pallas_call(
        functools.partial(_sample_merge_kernel, alpha=alpha),
        grid=(1,),
        in_specs=[pl.BlockSpec((n, D_MODEL), lambda i: (0, 0)),
                  pl.BlockSpec((n, 3 * D_MODEL), lambda i: (0, 0)),
                  pl.BlockSpec((n, N_IN), lambda i: (0, 0)),
                  pl.BlockSpec(ob.shape, lambda i: (0, 0)),
                  pl.BlockSpec(aux.shape, lambda i: (0, 0)),
                  pl.BlockSpec(ot.shape, lambda i: (0, 0, 0)),
                  pl.BlockSpec(lw["seg"].shape, lambda i: (0, 0)),
                  pl.BlockSpec(lw["v512"].shape, lambda i: (0, 0)),
                  pl.BlockSpec((None,) + big["wbr"].shape[1:], lambda i: (l, 0, 0, 0)),
                  pl.BlockSpec((None,) + big["wout"].shape[1:], lambda i: (l, 0, 0)),
                  pl.BlockSpec(lw["v1024"].shape, lambda i: (0, 0))],
        out_specs=pl.BlockSpec((n, D_MODEL), lambda i: (0, 0)),
        out_shape=jax.ShapeDtypeStruct((n, D_MODEL), f32),
        compiler_params=pltpu.CompilerParams(dimension_semantics=("arbitrary",),
                                             vmem_limit_bytes=VMEM_LIMIT_BYTES),
        name="sample_merge",
    )(x, mod, proj, ob, aux, ot, lw["seg"], lw["v512"], big["wbr"], big["wout"], lw["v1024"])


def _pack_layer(l, lru_conv_w, lru_conv_b, lru_wr, lru_br, lru_wi, lru_bi, lru_lambda, rwkv_mu,
                rwkv_w0, rwkv_ww, rwkv_a0, rwkv_wa, rwkv_kk, rwkv_ka, rwkv_rk, rwkv_lnx_g, rwkv_lnx_b,
                gmlp_ln_g, gmlp_ln_b, gmlp_ws, gmlp_bs, conf_dw_w, conf_dw_b, conf_ln_g, conf_ln_b,
                b_out, ln_g, ln_b):
    rep = BR_W // GMLP_GROUPS
    wr = block_diag(*[lru_wr[l, h] for h in range(LRU_HEADS)])
    wi = block_diag(*[lru_wi[l, h] for h in range(LRU_HEADS)])
    zero_lora = jnp.zeros((RWKV_RANK, BR_W), f32)
    wlora = jnp.concatenate([jnp.concatenate([rwkv_ww[l], zero_lora], axis=1),
                             jnp.concatenate([zero_lora, rwkv_wa[l]], axis=1)], axis=0)
    tril = jnp.tril(jnp.ones((GMLP_CHUNK, GMLP_CHUNK), dtype=bool))
    wm = jnp.where(tril[None], gmlp_ws[l], 0.0)
    wg = jnp.stack([jnp.concatenate([wm[2 * q], wm[2 * q + 1]], axis=1) for q in range(GMLP_GROUPS // 2)])
    rows = [lru_conv_w[l], lru_conv_b[l][None], lru_br[l][None], lru_bi[l][None], lru_lambda[l][None],
            rwkv_w0[l][None], rwkv_a0[l][None], rwkv_kk[l][None], rwkv_ka[l][None],
            rwkv_rk[l].reshape(1, BR_W), rwkv_lnx_g[l][None], rwkv_lnx_b[l][None],
            gmlp_ln_g[l][None], gmlp_ln_b[l][None], conf_dw_b[l][None], conf_ln_g[l][None],
            conf_ln_b[l][None], jnp.repeat(gmlp_ws[l, :, 0, 0], rep)[None],
            jnp.repeat(gmlp_bs[l, :, 0], rep)[None], jnp.zeros((V_CDW - V_GBS0 - 1, BR_W), f32),
            conf_dw_w[l], jnp.zeros((V_ROWS - V_CDW - CONF_K, BR_W), f32)]
    v1024 = jnp.concatenate([b_out[l][None], ln_g[l][None], ln_b[l][None],
                             jnp.zeros((SUBLANES - 3, D_MODEL), f32)], axis=0)
    nchunk = TIME_TILE // RWKV_CHUNK
    return dict(
        wlru=jnp.concatenate([wr, wi], axis=1).astype(bf16),
        wlora=wlora.astype(bf16),
        wg=wg.astype(bf16),
        bsx=jnp.repeat(gmlp_bs[l].T, rep, axis=1),
        seg=jnp.kron(jnp.eye(RWKV_HEADS, dtype=f32), jnp.ones((RWKV_HD, RWKV_HD), f32)).astype(bf16),
        ltri=jnp.kron(jnp.eye(nchunk, dtype=f32), jnp.tril(jnp.ones((RWKV_CHUNK, RWKV_CHUNK), f32))).astype(bf16),
        v512=jnp.concatenate(rows, axis=0),
        mu=rwkv_mu[l][None],
        v1024=v1024,
    )


def kernel(x_prompt, x_sample, state_lru_conv, state_lru_h, state_rwkv_shift, state_rwkv_S, state_conf_conv, c_prompt, c_sample, w_cond, b_cond, w_in, lru_conv_w, lru_conv_b, lru_wr, lru_br, lru_wi, lru_bi, lru_lambda, rwkv_mu, rwkv_w0, rwkv_ww, rwkv_a0, rwkv_wa, rwkv_kk, rwkv_ka, rwkv_rk, rwkv_lnx_g, rwkv_lnx_b, gmlp_ln_g, gmlp_ln_b, gmlp_ws, gmlp_bs, conf_dw_w, conf_dw_b, conf_ln_g, conf_ln_b, w_branch, w_out, b_out, ln_g, ln_b):
    depth = w_in.shape[0]
    alpha = (2.0 * depth) ** 0.25
    nb = x_prompt.shape[0]
    ns = x_sample.shape[0]

    mod = _cond_call(jnp.concatenate([c_prompt, c_sample], axis=0), w_cond, b_cond)
    col = lax.broadcasted_iota(jnp.int32, (1, 1, N_IN), 2)
    big = dict(win=(w_in * jnp.where(col >= OFF_MERGE, 0.5, 1.0)).astype(bf16),
               wbr=(0.5 * w_branch).astype(bf16), wout=w_out.astype(bf16))
    lconv_t = jnp.transpose(state_lru_conv, (0, 2, 1, 3))
    cconv_t = jnp.transpose(state_conf_conv, (0, 2, 1, 3))
    s_t = jnp.transpose(state_rwkv_S, (0, 2, 3, 4, 1))
    xp = x_prompt
    xs = x_sample.reshape(ns, D_MODEL)
    outs_p, outs_s = [], []
    for l in range(depth):
        lw = _pack_layer(l, lru_conv_w, lru_conv_b, lru_wr, lru_br, lru_wi, lru_bi, lru_lambda, rwkv_mu,
                         rwkv_w0, rwkv_ww, rwkv_a0, rwkv_wa, rwkv_kk, rwkv_ka, rwkv_rk, rwkv_lnx_g,
                         rwkv_lnx_b, gmlp_ln_g, gmlp_ln_b, gmlp_ws, gmlp_bs, conf_dw_w, conf_dw_b,
                         conf_ln_g, conf_ln_b, b_out, ln_g, ln_b)
        mod_p = mod[l, :nb].reshape(nb, 3, D_MODEL)
        mod_s = mod[l, nb:]
        xp, lconv_p, lh_p, rshift_p, rs_p, cconv_p = _prompt_call(xp, mod_p, lw, big, l, alpha)
        outs_p.append((lconv_p, lh_p.reshape(nb, BR_W), rshift_p.reshape(nb, RWKV_SHIFT_W), rs_p, cconv_p))

        proj_s = _sample_proj_call(xs, mod_s, big["win"], l)
        ob, aux, vecs, lconv_s, lh_s, rshift_s, cconv_s, gv_s = _sample_pre_call(
            proj_s, lconv_t, state_lru_h, state_rwkv_shift, cconv_t, lw, l)
        rs_s, ot = _sample_rwkv_call(s_t, vecs, l)
        xs = _sample_merge_call(xs, mod_s, proj_s, ob, aux, ot, lw, big, l, alpha)
        outs_s.append((lconv_s, lh_s, rshift_s, rs_s, cconv_s, gv_s.reshape(ns, 1, BR_W)))

    def stk(outs, j):
        return jnp.stack([o[j] for o in outs])

    return (xp, xs.reshape(ns, 1, D_MODEL),
            stk(outs_p, 0), jnp.transpose(stk(outs_s, 0), (0, 2, 1, 3)),
            stk(outs_p, 1), stk(outs_s, 1),
            stk(outs_p, 2), stk(outs_s, 2),
            stk(outs_p, 3), jnp.transpose(stk(outs_s, 3), (0, 4, 1, 2, 3)),
            stk(outs_p, 4), jnp.transpose(stk(outs_s, 4), (0, 2, 1, 3)),
            stk(outs_s, 5))
```
